```python
import jax, jax.numpy as jnp
from jax import lax
import numpy as np

D_MODEL = 2048
BATCH = 8
SEQ = 2048
DEPTH = 2
DEC_BATCH = 32
DEC_SEQ = 32
PAST_LEN = 2048

CHUNK = 64
D_MIX = D_MODEL
CONV_DIM = D_MIX // 4
CONV_WIDTH = 3
SGU_DIM = D_MIX // 4
SGU_HEADS = 4
SGU_HEAD_DIM = SGU_DIM // SGU_HEADS
MLP_CHUNK = 128
MLA_HEADS = 8
QK_NOPE = 128
QK_ROPE = 64
V_HEAD = 128
Q_LORA = 768
KV_LORA = 512
ROPE_THETA = 10000.0
Q_BLOCK = 128
SM_SCALE = (QK_NOPE + QK_ROPE) ** -0.5
OUT_HEAD_DIM = 128
OUT_HEADS = D_MIX // OUT_HEAD_DIM
D_IN = 3 * CONV_DIM + 2 * SGU_DIM + Q_LORA + KV_LORA + QK_ROPE
D_FF = 5632
N_EXPERTS = 8
TOP_K = 2
EXPERT_FF = 5632
N_DENSE = (DEPTH + 1) // 2
N_MOE = DEPTH // 2
ALPHA = (2 * DEPTH) ** 0.25
BETA = (8 * DEPTH) ** -0.25

kernel_name = 'hybrid_conv_sgu_mla_deepnorm_stream_step'


def _rms(x, g, eps=1e-6):
    xf = x.astype(jnp.float32)
    y = xf * lax.rsqrt(jnp.mean(xf * xf, axis=-1, keepdims=True) + eps)
    return (y * g).astype(x.dtype)


def _layernorm(x, g, b, eps=1e-5):
    xf = x.astype(jnp.float32)
    xc = xf - jnp.mean(xf, axis=-1, keepdims=True)
    var = jnp.mean(xc * xc, axis=-1, keepdims=True)
    return (xc * lax.rsqrt(var + eps) * g + b).astype(x.dtype)


def _rope(x, pos):
    half = QK_ROPE // 2
    inv = ROPE_THETA ** (-jnp.arange(half, dtype=jnp.float32) / half)
    ang = pos.astype(jnp.float32)[:, None] * inv[None, :]
    shape = (pos.shape[0],) + (1,) * (x.ndim - 3) + (QK_ROPE,)
    cos = jnp.concatenate([jnp.cos(ang), jnp.cos(ang)], axis=-1).reshape(shape)
    sin = jnp.concatenate([jnp.sin(ang), jnp.sin(ang)], axis=-1).reshape(shape)
    xf = x.astype(jnp.float32)
    rot = jnp.concatenate([-xf[..., half:], xf[..., :half]], axis=-1)
    return (xf * cos + rot * sin).astype(x.dtype)


def _split_offsets():
    a = CONV_DIM
    return [a, 2 * a, 3 * a, 3 * a + SGU_DIM, 3 * a + 2 * SGU_DIM,
            3 * a + 2 * SGU_DIM + Q_LORA, 3 * a + 2 * SGU_DIM + Q_LORA + KV_LORA]


def _short_conv(b_gate, c_gate, h, conv_w_l, conv_state):
    z = c_gate * h
    t = z.shape[1]
    zf = jnp.concatenate([conv_state.astype(z.dtype), z], axis=1)
    y = sum(conv_w_l[k] * zf[:, k:k + t] for k in range(CONV_WIDTH))
    return b_gate * y, zf[:, -(CONV_WIDTH - 1):]


def _sgu(u, v, ln_g, ln_b, sgu_w_l, sgu_b_l):
    bn, t, _ = v.shape
    vh = _layernorm(v.reshape(bn, t, SGU_HEADS, SGU_HEAD_DIM),
                    ln_g.reshape(SGU_HEADS, SGU_HEAD_DIM), ln_b.reshape(SGU_HEADS, SGU_HEAD_DIM))
    L = MLP_CHUNK if t >= MLP_CHUNK else t
    nc = t // L
    wm = jnp.tril(sgu_w_l[:, :L, :L])
    bias = jnp.transpose(sgu_b_l[:, :L])
    vc = vh.reshape(bn, nc, L, SGU_HEADS, SGU_HEAD_DIM)
    s = jnp.einsum('hpq,bnqhc->bnphc', wm, vc) + bias[:, :, None]
    out = u.reshape(bn, nc, L, SGU_HEADS, SGU_HEAD_DIM) * s
    return out.reshape(bn, t, SGU_DIM), vh.reshape(bn, t, SGU_DIM)


def _mla_prompt(q_nope, q_rope, k_nope, k_rope, v):
    bn, t, h, _ = q_nope.shape
    key_chunk = jnp.arange(t) // CHUNK

    def block(i):
        qs = i * Q_BLOCK
        qn = lax.dynamic_slice_in_dim(q_nope, qs, Q_BLOCK, axis=1)
        qr = lax.dynamic_slice_in_dim(q_rope, qs, Q_BLOCK, axis=1)
        s = (jnp.einsum('bqhd,bkhd->bhqk', qn, k_nope)
             + jnp.einsum('bqhr,bkr->bhqk', qr, k_rope)).astype(jnp.float32) * SM_SCALE
        q_chunk = (qs + jnp.arange(Q_BLOCK)) // CHUNK
        s = jnp.where(key_chunk[None, :] <= q_chunk[:, None], s, -jnp.inf)
        p = jax.nn.softmax(s, axis=-1).astype(v.dtype)
        return jnp.einsum('bhqk,bkhd->bqhd', p, v)

    o = lax.map(block, jnp.arange(t // Q_BLOCK))
    return jnp.transpose(o, (1, 0, 2, 3, 4)).reshape(bn, t, h * V_HEAD)


def _mla_sample(q_nope, q_rope, ckv_all, krope_all, w_uk_l, w_uv_l):
    bn, s_len, h, _ = q_nope.shape
    q_lat = jnp.einsum('bshd,chd->bshc', q_nope, w_uk_l)
    s = (jnp.einsum('bshc,bkc->bhsk', q_lat, ckv_all)
         + jnp.einsum('bshr,bkr->bhsk', q_rope, krope_all)).astype(jnp.float32) * SM_SCALE
    p = jax.nn.softmax(s, axis=-1).astype(ckv_all.dtype)
    o_lat = jnp.einsum('bhsk,bkc->bshc', p, ckv_all)
    return jnp.einsum('bshc,chd->bshd', o_lat, w_uv_l).reshape(bn, s_len, h * V_HEAD)


def _token_mixers(x, pos, conv_state, past_ckv, past_krope, w_in_l, conv_w_l, sgu_ln_g_l,
                  sgu_ln_b_l, sgu_w_l, sgu_b_l, q_norm_g_l, w_uq_l, kv_norm_g_l, w_uk_l, w_uv_l,
                  mix_norm_g_l, w_o_l):
    bn, t, _ = x.shape
    proj = x @ w_in_l
    b_g, c_g, h_c, u, v, c_q, c_kv, k_r = jnp.split(proj, _split_offsets(), axis=-1)
    a_out, conv_new = _short_conv(b_g, c_g, h_c, conv_w_l, conv_state)
    b_out, v_rows = _sgu(jax.nn.gelu(u), jax.nn.gelu(v), sgu_ln_g_l, sgu_ln_b_l, sgu_w_l, sgu_b_l)
    q = jnp.einsum('btc,chd->bthd', _rms(c_q, q_norm_g_l), w_uq_l)
    q_nope = q[..., :QK_NOPE]
    q_rope = _rope(q[..., QK_NOPE:], pos)
    ckv = _rms(c_kv, kv_norm_g_l)
    krope = _rope(k_r, pos)
    if past_ckv is None:
        k_nope = jnp.einsum('btc,chd->bthd', ckv, w_uk_l)
        v_att = jnp.einsum('btc,chd->bthd', ckv, w_uv_l)
        c_out = _mla_prompt(q_nope, q_rope, k_nope, krope, v_att)
    else:
        ckv_all = jnp.concatenate([past_ckv.astype(ckv.dtype), ckv], axis=1)
        krope_all = jnp.concatenate([past_krope.astype(krope.dtype), krope], axis=1)
        c_out = _mla_sample(q_nope, q_rope, ckv_all, krope_all, w_uk_l, w_uv_l)
    mixed = jnp.concatenate([a_out, b_out, c_out], axis=-1)
    mixed = _rms(mixed.reshape(bn, t, OUT_HEADS, OUT_HEAD_DIM),
                 mix_norm_g_l.reshape(OUT_HEADS, OUT_HEAD_DIM)).reshape(bn, t, D_MIX)
    return mixed @ w_o_l, conv_new, ckv, krope, v_rows


def _swiglu(x, wg, wu, wd):
    return (jax.nn.silu(x @ wg) * (x @ wu)) @ wd


def _moe(x, router_w_l, wg, wu, wd):
    logits = (x @ router_w_l).astype(jnp.float32)
    top_v, top_i = lax.top_k(logits, TOP_K)
    gates = jax.nn.softmax(top_v, axis=-1)
    dense_g = jnp.sum(jax.nn.one_hot(top_i, N_EXPERTS, dtype=jnp.float32) * gates[..., None],
                      axis=-2).astype(x.dtype)
    y = jnp.zeros_like(x)
    for e in range(N_EXPERTS):
        y = y + dense_g[..., e:e + 1] * _swiglu(x, wg[e], wu[e], wd[e])
    return y


def setup_inputs(seed: int = 0) -> dict:
    key = jax.random.key(seed)
    ks = iter(jax.random.split(key, 40))

    def nrm(shape, scale=1.0):
        return jax.random.normal(next(ks), shape, jnp.float32) * scale

    def gain(shape):
        return 1.0 + nrm(shape, 0.1)

    return {
        'x_prompt': nrm((BATCH, SEQ, D_MODEL)),
        'x_sample': nrm((DEC_BATCH, DEC_SEQ, D_MODEL)),
        'state_conv': nrm((DEPTH, DEC_BATCH, CONV_WIDTH - 1, CONV_DIM)),
        'cache_ckv': nrm((DEPTH, DEC_BATCH, PAST_LEN, KV_LORA)),
        'cache_krope': nrm((DEPTH, DEC_BATCH, PAST_LEN, QK_ROPE)),
        'w_in': nrm((DEPTH, D_MODEL, D_IN), D_MODEL ** -0.5),
        'conv_w': nrm((DEPTH, CONV_WIDTH, CONV_DIM), CONV_WIDTH ** -0.5),
        'sgu_ln_g': gain((DEPTH, SGU_DIM)),
        'sgu_ln_b': nrm((DEPTH, SGU_DIM), 0.1),
        'sgu_w': nrm((DEPTH, SGU_HEADS, MLP_CHUNK, MLP_CHUNK), MLP_CHUNK ** -0.5),
        'sgu_b': gain((DEPTH, SGU_HEADS, MLP_CHUNK)),
        'q_norm_g': gain((DEPTH, Q_LORA)),
        'w_uq': nrm((DEPTH, Q_LORA, MLA_HEADS, QK_NOPE + QK_ROPE), Q_LORA ** -0.5),
        'kv_norm_g': gain((DEPTH, KV_LORA)),
        'w_uk': nrm((DEPTH, KV_LORA, MLA_HEADS, QK_NOPE), KV_LORA ** -0.5),
        'w_uv': nrm((DEPTH, KV_LORA, MLA_HEADS, V_HEAD), KV_LORA ** -0.5),
        'mix_norm_g': gain((DEPTH, D_MIX)),
        'w_o': nrm((DEPTH, D_MIX, D_MODEL), BETA * D_MIX ** -0.5),
        'ln1_g': gain((DEPTH, D_MODEL)),
        'ln1_b': nrm((DEPTH, D_MODEL), 0.02),
        'ln2_g': gain((DEPTH, D_MODEL)),
        'ln2_b': nrm((DEPTH, D_MODEL), 0.02),
        'ffn_w_gate': nrm((N_DENSE, D_MODEL, D_FF), D_MODEL ** -0.5),
        'ffn_w_up': nrm((N_DENSE, D_MODEL, D_FF), D_MODEL ** -0.5),
        'ffn_w_down': nrm((N_DENSE, D_FF, D_MODEL), BETA * D_FF ** -0.5),
        'router_w': nrm((N_MOE, D_MODEL, N_EXPERTS), D_MODEL ** -0.5),
        'moe_w_gate': nrm((N_MOE, N_EXPERTS, D_MODEL, EXPERT_FF), D_MODEL ** -0.5),
        'moe_w_up': nrm((N_MOE, N_EXPERTS, D_MODEL, EXPERT_FF), D_MODEL ** -0.5),
        'moe_w_down': nrm((N_MOE, N_EXPERTS, EXPERT_FF, D_MODEL), BETA * EXPERT_FF ** -0.5),
    }


def reference(x_prompt, x_sample, state_conv, cache_ckv, cache_krope, w_in, conv_w, sgu_ln_g,
              sgu_ln_b, sgu_w, sgu_b, q_norm_g, w_uq, kv_norm_g, w_uk, w_uv, mix_norm_g, w_o,
              ln1_g, ln1_b, ln2_g, ln2_b, ffn_w_gate, ffn_w_up, ffn_w_down, router_w,
              moe_w_gate, moe_w_up, moe_w_down):
    pos_p = jnp.arange(x_prompt.shape[1])
    pos_s = PAST_LEN + jnp.arange(x_sample.shape[1])
    zero_conv = jnp.zeros((x_prompt.shape[0], CONV_WIDTH - 1, CONV_DIM), x_prompt.dtype)
    hp, hs = x_prompt, x_sample
    conv_p, ckv_p, kr_p, conv_s, ckv_s, kr_s, v_s = [], [], [], [], [], [], []
    for l in range(DEPTH):
        lp = (w_in[l], conv_w[l], sgu_ln_g[l], sgu_ln_b[l], sgu_w[l], sgu_b[l], q_norm_g[l],
              w_uq[l], kv_norm_g[l], w_uk[l], w_uv[l], mix_norm_g[l], w_o[l])
        mp, c1, k1, r1, _ = _token_mixers(hp, pos_p, zero_conv, None, None, *lp)
        ms, c2, k2, r2, v2 = _token_mixers(hs, pos_s, state_conv[l], cache_ckv[l],
                                           cache_krope[l], *lp)
        hp = _layernorm(ALPHA * hp + mp, ln1_g[l], ln1_b[l])
        hs = _layernorm(ALPHA * hs + ms, ln1_g[l], ln1_b[l])
        i = l // 2
        if l % 2 == 0:
            fp = _swiglu(hp, ffn_w_gate[i], ffn_w_up[i], ffn_w_down[i])
            fs = _swiglu(hs, ffn_w_gate[i], ffn_w_up[i], ffn_w_down[i])
        else:
            fp = _moe(hp, router_w[i], moe_w_gate[i], moe_w_up[i], moe_w_down[i])
            fs = _moe(hs, router_w[i], moe_w_gate[i], moe_w_up[i], moe_w_down[i])
        hp = _layernorm(ALPHA * hp + fp, ln2_g[l], ln2_b[l])
        hs = _layernorm(ALPHA * hs + fs, ln2_g[l], ln2_b[l])
        conv_p.append(c1)
        ckv_p.append(k1)
        kr_p.append(r1)
        conv_s.append(c2)
        ckv_s.append(k2)
        kr_s.append(r2)
        v_s.append(v2)
    return (hp, hs, jnp.stack(conv_p), jnp.stack(ckv_p), jnp.stack(kr_p),
            jnp.stack(conv_s), jnp.stack(ckv_s), jnp.stack(kr_s), jnp.stack(v_s))
```

```python
import functools

import numpy as np
import jax
import jax.numpy as jnp
from jax import lax
from jax.experimental import pallas as pl
from jax.experimental.pallas import tpu as pltpu

F32 = jnp.float32
BF16 = jnp.bfloat16

D_MODEL = 2048
CONV_DIM = 512
CONV_WIDTH = 3
SGU_DIM = 512
SGU_HEADS = 4
MLP_CHUNK = 128
MLA_HEADS = 8
QK_NOPE = 128
QK_ROPE = 64
V_HEAD = 128
Q_LORA = 768
KV_LORA = 512
ROPE_THETA = 10000.0
CHUNK = 64
HEAD = 128
N_EXPERTS = 8
DEPTH = 2
SM_SCALE = (QK_NOPE + QK_ROPE) ** -0.5
ALPHA = (2 * DEPTH) ** 0.25

COL_BG, COL_CG, COL_HC, COL_U, COL_V = 0, 512, 1024, 1536, 2048
COL_CKV, COL_CQ, COL_KR = 2560, 3072, 3840
D_IN_PAD = 4096
QK_PAD = 256

LANE = 128
VMEM_LIMIT = 56 * 1024 * 1024


def _params(n_axes):
    return pltpu.CompilerParams(dimension_semantics=("arbitrary",) * n_axes,
                                vmem_limit_bytes=VMEM_LIMIT)


def _tile(n, pref):
    t = pref
    while n % t:
        t //= 2
    assert t >= 8, (n, pref)
    return t


def _rms(x, g, eps=1e-6):
    return x * lax.rsqrt(jnp.mean(x * x, axis=-1, keepdims=True) + eps) * g


def _gelu(x):
    c = np.sqrt(2.0 / np.pi).astype(np.float32)
    return 0.5 * x * (1.0 + jnp.tanh(c * (x + 0.044715 * (x * x * x))))


def _dot(a, b):
    return jnp.dot(a, b, preferred_element_type=F32)


def _dot_nt(a, b):
    return lax.dot_general(a, b, (((1,), (1,)), ((), ())), preferred_element_type=F32)


def _mm_kernel(x_ref, w_ref, o_ref):
    o_ref[...] = _dot(x_ref[...], w_ref[...]).astype(o_ref.dtype)


def _matmul(x, w, out_dtype, tm_pref=1024, tn_pref=512):
    m, k = x.shape
    n = w.shape[1]
    tm, tn = _tile(m, tm_pref), _tile(n, tn_pref)
    return pl.pallas_call(
        _mm_kernel,
        grid=(m // tm, n // tn),
        in_specs=[pl.BlockSpec((tm, k), lambda i, j: (i, 0)),
                  pl.BlockSpec((k, tn), lambda i, j: (0, j))],
        out_specs=pl.BlockSpec((tm, tn), lambda i, j: (i, j)),
        out_shape=jax.ShapeDtypeStruct((m, n), out_dtype),
        compiler_params=_params(2),
        name="in_proj",
    )(x, w)


def _conv_kernel(bg_ref, cg_ref, hc_ref, st_ref, w_ref, g_ref, *rest):
    o_ref, ns_ref = rest[-2], rest[-1]
    t = bg_ref.shape[0]
    z = cg_ref[...].astype(F32) * hc_ref[...].astype(F32)
    st = st_ref[0].astype(F32)
    w = w_ref[...]
    row = lax.broadcasted_iota(jnp.int32, z.shape, 0)
    z1 = jnp.where(row == 0, st[1:2], pltpu.roll(z, 1, 0))
    z2 = jnp.where(row == 0, st[0:1],
                   jnp.where(row == 1, st[1:2], pltpu.roll(z, 2, 0)))
    y = w[0:1] * z2 + w[1:2] * z1 + w[2:3] * z
    a = bg_ref[...].astype(F32) * y
    o_ref[...] = _rms(a, g_ref[...]).astype(o_ref.dtype)
    ns_ref[0] = z[t - 2:t, :]


def _conv_mixer(proj, state, conv_w, g_mix, mixed, nb, t, row_blk0):
    m = proj.shape[0]
    ng = CONV_DIM // LANE

    def col(c0):
        return pl.BlockSpec((t, LANE), lambda b, g: (row_blk0 + b, c0 // LANE + g))

    in_specs = [col(COL_BG), col(COL_CG), col(COL_HC),
                pl.BlockSpec((1, CONV_WIDTH - 1, LANE), lambda b, g: (b, 0, g)),
                pl.BlockSpec((CONV_WIDTH, LANE), lambda b, g: (0, g)),
                pl.BlockSpec((1, LANE), lambda b, g: (0, g))]
    args = [proj, proj, proj, state, conv_w, g_mix]
    aliases = {}
    if mixed is not None:
        in_specs.append(pl.BlockSpec(memory_space=pl.ANY))
        args.append(mixed)
        aliases = {6: 0}
    return pl.pallas_call(
        _conv_kernel,
        grid=(nb, ng),
        in_specs=in_specs,
        out_specs=[pl.BlockSpec((t, LANE), lambda b, g: (row_blk0 + b, g)),
                   pl.BlockSpec((1, CONV_WIDTH - 1, LANE), lambda b, g: (b, 0, g))],
        out_shape=[jax.ShapeDtypeStruct((m, D_MODEL), BF16),
                   jax.ShapeDtypeStruct((nb, CONV_WIDTH - 1, CONV_DIM), F32)],
        input_output_aliases=aliases,
        compiler_params=_params(2),
        name="conv_mixer",
    )(*args)


def _sgu_kernel(u_ref, v_ref, w_ref, bt_ref, lg_ref, lb_ref, g_ref, mixed_in_ref, o_ref,
                *vr_refs, chunk):
    del mixed_in_ref
    tr = u_ref.shape[0]
    r = lax.broadcasted_iota(jnp.int32, (chunk, chunk), 0)
    c = lax.broadcasted_iota(jnp.int32, (chunk, chunk), 1)
    for h in range(SGU_HEADS):
        cs = slice(h * HEAD, (h + 1) * HEAD)
        wm = jnp.where(c <= r, w_ref[h][:chunk, :chunk], 0.0).astype(BF16)
        bias = bt_ref[:chunk, h:h + 1]
        for k in range(tr // chunk):
            rs = slice(k * chunk, (k + 1) * chunk)
            vg = _gelu(v_ref[rs, cs].astype(F32))
            xc = vg - jnp.mean(vg, axis=-1, keepdims=True)
            var = jnp.mean(xc * xc, axis=-1, keepdims=True)
            vh = xc * lax.rsqrt(var + 1e-5) * lg_ref[:, cs] + lb_ref[:, cs]
            if vr_refs:
                vr_refs[0][rs, cs] = vh
            s = _dot(wm, vh.astype(BF16)) + bias
            out = _gelu(u_ref[rs, cs].astype(F32)) * s
            o_ref[rs, cs] = _rms(out, g_ref[:, cs]).astype(o_ref.dtype)


def _sgu_mixer(proj, sgu_w, sgu_bt, ln_g, ln_b, g_mix, mixed, nrows, row0, chunk, tr, want_v):
    m = proj.shape[0]
    rb0 = row0 // tr
    full = lambda shape: pl.BlockSpec(shape, lambda i: (0,) * len(shape))
    out_specs = [pl.BlockSpec((tr, SGU_DIM), lambda i: (rb0 + i, 1))]
    out_shape = [jax.ShapeDtypeStruct((m, D_MODEL), BF16)]
    if want_v:
        out_specs.append(pl.BlockSpec((tr, SGU_DIM), lambda i: (i, 0)))
        out_shape.append(jax.ShapeDtypeStruct((nrows, SGU_DIM), F32))
    return pl.pallas_call(
        functools.partial(_sgu_kernel, chunk=chunk),
        grid=(nrows // tr,),
        in_specs=[pl.BlockSpec((tr, SGU_DIM), lambda i: (rb0 + i, COL_U // SGU_DIM)),
                  pl.BlockSpec((tr, SGU_DIM), lambda i: (rb0 + i, COL_V // SGU_DIM)),
                  full((SGU_HEADS, MLP_CHUNK, MLP_CHUNK)),
                  full((MLP_CHUNK, SGU_HEADS)),
                  full((1, SGU_DIM)), full((1, SGU_DIM)),
                  pl.BlockSpec((1, SGU_DIM), lambda i: (0, 1)),
                  pl.BlockSpec(memory_space=pl.ANY)],
        out_specs=out_specs,
        out_shape=out_shape,
        input_output_aliases={7: 0},
        compiler_params=_params(1),
        name="sgu_mixer",
    )(proj, proj, sgu_w, sgu_bt, ln_g, ln_b, g_mix, mixed)


def _rope_q(qm, qr, cos, sin, h):
    nope = qm[:, h * QK_PAD:h * QK_PAD + QK_NOPE]
    rope = (qm[:, h * QK_PAD + QK_NOPE:(h + 1) * QK_PAD] * cos
            + qr[:, h * LANE:(h + 1) * LANE] * sin)
    return nope * SM_SCALE, rope * SM_SCALE


def _rope_k(kr_ref, cos, sin):
    blk = kr_ref[...].astype(F32)
    return blk * cos + pltpu.roll(blk, QK_ROPE, 1) * sin


def _mla_pre_prompt_kernel(cq_ref, ckv_ref, kr_ref, qg_ref, kvg_ref, wqm_ref, wqr_ref,
                           wuk_ref, wuv_ref, cos_ref, sin_ref,
                           q_ref, k_ref, v_ref, ckv_out_ref, kr_out_ref):
    cos, sin = cos_ref[...], sin_ref[...]
    cqn = _rms(cq_ref[...].astype(F32), qg_ref[...]).astype(BF16)
    qm = _dot(cqn, wqm_ref[...])
    qr = _dot(cqn, wqr_ref[...])
    ckv = _rms(ckv_ref[...].astype(F32), kvg_ref[...])
    ckv_out_ref[...] = ckv
    ckb = ckv.astype(BF16)
    kn = _dot(ckb, wuk_ref[...])
    v_ref[...] = _dot(ckb, wuv_ref[...]).astype(v_ref.dtype)
    krp = _rope_k(kr_ref, cos, sin)
    kr_out_ref[...] = krp[:, :QK_ROPE]
    krb = krp.astype(BF16)
    for h in range(MLA_HEADS):
        nope, rope = _rope_q(qm, qr, cos, sin, h)
        q_ref[:, h * QK_PAD:h * QK_PAD + QK_NOPE] = nope.astype(BF16)
        q_ref[:, h * QK_PAD + QK_NOPE:(h + 1) * QK_PAD] = rope.astype(BF16)
        k_ref[:, h * QK_PAD:h * QK_PAD + QK_NOPE] = kn[:, h * LANE:(h + 1) * LANE].astype(BF16)
        k_ref[:, h * QK_PAD + QK_NOPE:(h + 1) * QK_PAD] = krb


def _mla_pre_prompt(proj, lw, cos, sin, mp, t):
    tm = _tile(t, 512)
    nt = t // tm
    full = lambda shape: pl.BlockSpec(shape, lambda i: (0,) * len(shape))
    hq, hv = MLA_HEADS * QK_PAD, MLA_HEADS * V_HEAD
    return pl.pallas_call(
        _mla_pre_prompt_kernel,
        grid=(mp // tm,),
        in_specs=[pl.BlockSpec((tm, Q_LORA), lambda i: (i, COL_CQ // Q_LORA)),
                  pl.BlockSpec((tm, KV_LORA), lambda i: (i, COL_CKV // KV_LORA)),
                  pl.BlockSpec((tm, LANE), lambda i: (i, COL_KR // LANE)),
                  full((1, Q_LORA)), full((1, KV_LORA)),
                  full((Q_LORA, hq)), full((Q_LORA, MLA_HEADS * LANE)),
                  full((KV_LORA, hv)), full((KV_LORA, hv)),
                  pl.BlockSpec((tm, LANE), lambda i: (i % nt, 0)),
                  pl.BlockSpec((tm, LANE), lambda i: (i % nt, 0))],
        out_specs=[pl.BlockSpec((tm, hq), lambda i: (i, 0)),
                   pl.BlockSpec((tm, hq), lambda i: (i, 0)),
                   pl.BlockSpec((tm, hv), lambda i: (i, 0)),
                   pl.BlockSpec((tm, KV_LORA), lambda i: (i, 0)),
                   pl.BlockSpec((tm, QK_ROPE), lambda i: (i, 0))],
        out_shape=[jax.ShapeDtypeStruct((mp, hq), BF16),
                   jax.ShapeDtypeStruct((mp, hq), BF16),
                   jax.ShapeDtypeStruct((mp, hv), BF16),
                   jax.ShapeDtypeStruct((mp, KV_LORA), F32),
                   jax.ShapeDtypeStruct((mp, QK_ROPE), F32)],
        compiler_params=_params(1),
        name="mla_pre_prompt",
    )(proj, proj, proj, lw["q_norm_g"], lw["kv_norm_g"], lw["wq_main"], lw["wq_rot"],
      lw["w_uk"], lw["w_uv"], cos, sin)


def _mla_pre_sample_kernel(cq_ref, ckv_ref, kr_ref, qg_ref, kvg_ref, wqm_ref, wqr_ref,
                           wukt_ref, cos_ref, sin_ref,
                           ql_ref, qr_ref, ckv_out_ref, kr_out_ref):
    cos, sin = cos_ref[...], sin_ref[...]
    cqn = _rms(cq_ref[...].astype(F32), qg_ref[...]).astype(BF16)
    qm = _dot(cqn, wqm_ref[...])
    qr = _dot(cqn, wqr_ref[...])
    ckv_out_ref[...] = _rms(ckv_ref[...].astype(F32), kvg_ref[...])
    kr_out_ref[...] = _rope_k(kr_ref, cos, sin)[:, :QK_ROPE]
    for h in range(MLA_HEADS):
        nope, rope = _rope_q(qm, qr, cos, sin, h)
        ql_ref[h] = _dot(nope.astype(BF16), wukt_ref[h]).astype(BF16)
        qr_ref[h] = rope[:, :QK_ROPE].astype(BF16)


def _mla_pre_sample(proj, lw, cos, sin, mp, ms, tm):
    rb0 = mp // tm
    full = lambda shape: pl.BlockSpec(shape, lambda i: (0,) * len(shape))
    return pl.pallas_call(
        _mla_pre_sample_kernel,
        grid=(ms // tm,),
        in_specs=[pl.BlockSpec((tm, Q_LORA), lambda i: (rb0 + i, COL_CQ // Q_LORA)),
                  pl.BlockSpec((tm, KV_LORA), lambda i: (rb0 + i, COL_CKV // KV_LORA)),
                  pl.BlockSpec((tm, LANE), lambda i: (rb0 + i, COL_KR // LANE)),
                  full((1, Q_LORA)), full((1, KV_LORA)),
                  full((Q_LORA, MLA_HEADS * QK_PAD)), full((Q_LORA, MLA_HEADS * LANE)),
                  full((MLA_HEADS, QK_NOPE, KV_LORA)),
                  full((tm, LANE)), full((tm, LANE))],
        out_specs=[pl.BlockSpec((MLA_HEADS, tm, KV_LORA), lambda i: (0, i, 0)),
                   pl.BlockSpec((MLA_HEADS, tm, QK_ROPE), lambda i: (0, i, 0)),
                   pl.BlockSpec((tm, KV_LORA), lambda i: (i, 0)),
                   pl.BlockSpec((tm, QK_ROPE), lambda i: (i, 0))],
        out_shape=[jax.ShapeDtypeStruct((MLA_HEADS, ms, KV_LORA), BF16),
                   jax.ShapeDtypeStruct((MLA_HEADS, ms, QK_ROPE), BF16),
                   jax.ShapeDtypeStruct((ms, KV_LORA), F32),
                   jax.ShapeDtypeStruct((ms, QK_ROPE), F32)],
        compiler_params=_params(1),
        name="mla_pre_sample",
    )(proj, proj, proj, lw["q_norm_g"], lw["kv_norm_g"], lw["wq_main"], lw["wq_rot"],
      lw["w_ukt"], cos, sin)


def _attn_prompt_kernel(q_ref, k_ref, v_ref, g_ref, mixed_in_ref, o_ref, *, tq):
    del mixed_in_ref
    t = q_ref.shape[0]
    rc = lax.broadcasted_iota(jnp.int32, (tq, tq), 0) // CHUNK
    cc = lax.broadcasted_iota(jnp.int32, (tq, tq), 1) // CHUNK
    visible = cc <= rc
    g = g_ref[...]
    for i in range(t // tq):
        lo, hi = i * tq, (i + 1) * tq
        q = q_ref[lo:hi, :]
        sd = jnp.where(visible, _dot_nt(q, k_ref[lo:hi, :]), -jnp.inf)
        m = jnp.max(sd, axis=-1, keepdims=True)
        if i:
            so = _dot_nt(q, k_ref[0:lo, :])
            m = jnp.maximum(m, jnp.max(so, axis=-1, keepdims=True))
        pd = jnp.exp(sd - m)
        l = jnp.sum(pd, axis=-1, keepdims=True)
        o = _dot(pd.astype(BF16), v_ref[lo:hi, :])
        if i:
            po = jnp.exp(so - m)
            l = l + jnp.sum(po, axis=-1, keepdims=True)
            o = o + _dot(po.astype(BF16), v_ref[0:lo, :])
        o_ref[lo:hi, :] = _rms(o / l, g).astype(o_ref.dtype)


def _attn_prompt(q, k, v, g_mix, mixed, nb, t):
    tq = _tile(t, 256)
    c0 = (CONV_DIM + SGU_DIM) // HEAD
    return pl.pallas_call(
        functools.partial(_attn_prompt_kernel, tq=tq),
        grid=(nb, MLA_HEADS),
        in_specs=[pl.BlockSpec((t, QK_PAD), lambda b, h: (b, h)),
                  pl.BlockSpec((t, QK_PAD), lambda b, h: (b, h)),
                  pl.BlockSpec((t, V_HEAD), lambda b, h: (b, h)),
                  pl.BlockSpec((1, HEAD), lambda b, h: (0, c0 + h)),
                  pl.BlockSpec(memory_space=pl.ANY)],
        out_specs=pl.BlockSpec((t, HEAD), lambda b, h: (b, c0 + h)),
        out_shape=jax.ShapeDtypeStruct(mixed.shape, mixed.dtype),
        input_output_aliases={4: 0},
        compiler_params=_params(2),
        name="attn_prompt",
    )(q, k, v, g_mix, mixed)


def _attn_sample_kernel(ql_ref, qr_ref, cc_ref, ck_ref, nc_ref, nk_ref, wuv_ref, g_ref,
                        mixed_in_ref, o_ref):
    del mixed_in_ref
    s_len = ql_ref.shape[1]
    ql = ql_ref[...].reshape(MLA_HEADS * s_len, KV_LORA)
    qr = qr_ref[...].reshape(MLA_HEADS * s_len, QK_ROPE)
    cc = cc_ref[0, 0].astype(BF16)
    ck = ck_ref[0, 0].astype(BF16)
    nc = nc_ref[...].astype(BF16)
    nk = nk_ref[...].astype(BF16)
    sp = _dot_nt(ql, cc) + _dot_nt(qr, ck)
    sn = _dot_nt(ql, nc) + _dot_nt(qr, nk)
    m = jnp.maximum(jnp.max(sp, axis=-1, keepdims=True), jnp.max(sn, axis=-1, keepdims=True))
    pp, pn = jnp.exp(sp - m), jnp.exp(sn - m)
    l = jnp.sum(pp, axis=-1, keepdims=True) + jnp.sum(pn, axis=-1, keepdims=True)
    o_lat = ((_dot(pp.astype(BF16), cc) + _dot(pn.astype(BF16), nc)) / l).astype(BF16)
    for h in range(MLA_HEADS):
        oh = _dot(o_lat[h * s_len:(h + 1) * s_len], wuv_ref[h])
        o_ref[:, h * HEAD:(h + 1) * HEAD] = _rms(oh, g_ref[:, h * HEAD:(h + 1) * HEAD]
                                                 ).astype(o_ref.dtype)


def _attn_sample(q_lat, q_rope, cache_ckv, cache_krope, ckv_new, kr_new, wuv_h, g_mix, mixed,
                 layer, mp, nb, s_len):
    past = cache_ckv.shape[2]
    rb0 = mp // s_len
    full = lambda shape: pl.BlockSpec(shape, lambda b: (0,) * len(shape))
    return pl.pallas_call(
        _attn_sample_kernel,
        grid=(nb,),
        in_specs=[pl.BlockSpec((MLA_HEADS, s_len, KV_LORA), lambda b: (0, b, 0)),
                  pl.BlockSpec((MLA_HEADS, s_len, QK_ROPE), lambda b: (0, b, 0)),
                  pl.BlockSpec((1, 1, past, KV_LORA), lambda b: (layer, b, 0, 0)),
                  pl.BlockSpec((1, 1, past, QK_ROPE), lambda b: (layer, b, 0, 0)),
                  pl.BlockSpec((s_len, KV_LORA), lambda b: (b, 0)),
                  pl.BlockSpec((s_len, QK_ROPE), lambda b: (b, 0)),
                  full((MLA_HEADS, KV_LORA, V_HEAD)),
                  pl.BlockSpec((1, MLA_HEADS * HEAD), lambda b: (0, 1)),
                  pl.BlockSpec(memory_space=pl.ANY)],
        out_specs=pl.BlockSpec((s_len, MLA_HEADS * HEAD), lambda b: (rb0 + b, 1)),
        out_shape=jax.ShapeDtypeStruct(mixed.shape, mixed.dtype),
        input_output_aliases={8: 0},
        compiler_params=_params(1),
        name="attn_sample",
    )(q_lat, q_rope, cache_ckv, cache_krope, ckv_new, kr_new, wuv_h, g_mix, mixed)


def _layernorm(y, g, b, eps=1e-5):
    yc = y - jnp.mean(y, axis=-1, keepdims=True)
    var = jnp.mean(yc * yc, axis=-1, keepdims=True)
    return yc * lax.rsqrt(var + eps) * g + b


def _out_proj_kernel(mx_ref, w_ref, x_ref, g_ref, b_ref, o_ref, ob_ref):
    y = ALPHA * x_ref[...] + _dot(mx_ref[...], w_ref[...])
    out = _layernorm(y, g_ref[...], b_ref[...])
    o_ref[...] = out
    ob_ref[...] = out.astype(BF16)


def _out_proj(mixed, w_o, x, g, b):
    m, d = x.shape
    tm = _tile(m, 256)
    row = lambda i: (i, 0)
    full = lambda shape: pl.BlockSpec(shape, lambda i: (0,) * len(shape))
    return pl.pallas_call(
        _out_proj_kernel,
        grid=(m // tm,),
        in_specs=[pl.BlockSpec((tm, d), row), full((d, d)), pl.BlockSpec((tm, d), row),
                  full((1, d)), full((1, d))],
        out_specs=[pl.BlockSpec((tm, d), row), pl.BlockSpec((tm, d), row)],
        out_shape=[jax.ShapeDtypeStruct((m, d), F32), jax.ShapeDtypeStruct((m, d), BF16)],
        compiler_params=_params(1),
        name="out_proj_ln",
    )(mixed, w_o, x, g, b)


def _ffn_kernel(xb_ref, wg_ref, wu_ref, wd_ref, gate_ref, x_ref, g_ref, b_ref,
                o_ref, ob_ref, acc_ref, *, gated):
    e, f = pl.program_id(1), pl.program_id(2)

    @pl.when((e == 0) & (f == 0))
    def _():
        acc_ref[...] = jnp.zeros_like(acc_ref)

    xb = xb_ref[...]
    a = _dot(xb, wg_ref[0])
    u = _dot(xb, wu_ref[0])
    h = a * (1.0 / (1.0 + jnp.exp(-a))) * u
    if gated:
        lane = lax.broadcasted_iota(jnp.int32, gate_ref.shape, 1)
        h = h * jnp.sum(jnp.where(lane == e, gate_ref[...], 0.0), axis=-1, keepdims=True)
    acc_ref[...] += _dot(h.astype(BF16), wd_ref[0])

    @pl.when((e == pl.num_programs(1) - 1) & (f == pl.num_programs(2) - 1))
    def _():
        out = _layernorm(ALPHA * x_ref[...] + acc_ref[...], g_ref[...], b_ref[...])
        o_ref[...] = out
        ob_ref[...] = out.astype(BF16)


def _ffn(xb, x, wg, wu, wd, gates, g, b, gated):
    m, d = x.shape
    ne, _, ff = wg.shape
    tm, tf = _tile(m, 512), _tile(ff, 512)
    row = lambda i, e, f: (i, 0)
    vec = pl.BlockSpec((1, d), lambda i, e, f: (0, 0))
    return pl.pallas_call(
        functools.partial(_ffn_kernel, gated=gated),
        grid=(m // tm, ne, ff // tf),
        in_specs=[pl.BlockSpec((tm, d), row),
                  pl.BlockSpec((1, d, tf), lambda i, e, f: (e, 0, f)),
                  pl.BlockSpec((1, d, tf), lambda i, e, f: (e, 0, f)),
                  pl.BlockSpec((1, tf, d), lambda i, e, f: (e, f, 0)),
                  pl.BlockSpec((tm, LANE), row),
                  pl.BlockSpec((tm, d), row), vec, vec],
        out_specs=[pl.BlockSpec((tm, d), row), pl.BlockSpec((tm, d), row)],
        out_shape=[jax.ShapeDtypeStruct((m, d), F32), jax.ShapeDtypeStruct((m, d), BF16)],
        scratch_shapes=[pltpu.VMEM((tm, d), F32)],
        compiler_params=_params(3),
        name="ffn_gated" if gated else "ffn_dense",
    )(xb, wg, wu, wd, gates, x, g, b)


def _router_kernel(x_ref, wh_ref, wl_ref, o_ref):
    x = x_ref[...]
    xh = x.astype(BF16)
    xl = (x - xh.astype(F32)).astype(BF16)
    logits = _dot(xh, wh_ref[...]) + (_dot(xl, wh_ref[...]) + _dot(xh, wl_ref[...]))
    lane = lax.broadcasted_iota(jnp.int32, logits.shape, 1)
    logits = jnp.where(lane < N_EXPERTS, logits, -jnp.inf)
    m1 = jnp.max(logits, axis=-1, keepdims=True)
    i1 = jnp.min(jnp.where(logits == m1, lane, LANE), axis=-1, keepdims=True)
    rest = jnp.where(lane == i1, -jnp.inf, logits)
    m2 = jnp.max(rest, axis=-1, keepdims=True)
    i2 = jnp.min(jnp.where(rest == m2, lane, LANE), axis=-1, keepdims=True)
    e2 = jnp.exp(m2 - m1)
    g1 = 1.0 / (1.0 + e2)
    g2 = e2 / (1.0 + e2)
    o_ref[...] = jnp.where(lane == i1, g1, jnp.where(lane == i2, g2, 0.0))


def _router(x, wh, wl):
    m, d = x.shape
    tm = _tile(m, 512)
    full = pl.BlockSpec((d, LANE), lambda i: (0, 0))
    return pl.pallas_call(
        _router_kernel,
        grid=(m // tm,),
        in_specs=[pl.BlockSpec((tm, d), lambda i: (i, 0)), full, full],
        out_specs=pl.BlockSpec((tm, LANE), lambda i: (i, 0)),
        out_shape=jax.ShapeDtypeStruct((m, LANE), F32),
        compiler_params=_params(1),
        name="router",
    )(x, wh, wl)


def _rot_cols(w):
    half = QK_ROPE // 2
    return jnp.concatenate([-w[..., half:], w[..., :half]], axis=-1)


def _layer_weights(l, w_in, conv_w, sgu_ln_g, sgu_ln_b, sgu_w, sgu_b, q_norm_g, w_uq, kv_norm_g,
                   w_uk, w_uv, mix_norm_g, w_o):
    d = w_in.shape[1]
    wi = w_in[l]
    o_cq, o_ckv, o_kr = 2560, 3328, 3840
    k_r = wi[:, o_kr:o_kr + QK_ROPE]
    w_in_pad = jnp.concatenate(
        [wi[:, :o_cq], wi[:, o_ckv:o_kr], wi[:, o_cq:o_ckv], k_r, _rot_cols(k_r),
         jnp.zeros((d, D_IN_PAD - COL_KR - 2 * QK_ROPE), F32)], axis=1).astype(BF16)
    uq = w_uq[l]
    zeros = jnp.zeros((Q_LORA, MLA_HEADS, QK_PAD - QK_NOPE - QK_ROPE), F32)
    wq_main = jnp.concatenate([uq, zeros], axis=-1).reshape(Q_LORA, MLA_HEADS * QK_PAD)
    wq_rot = jnp.concatenate([_rot_cols(uq[..., QK_NOPE:]), zeros], axis=-1
                             ).reshape(Q_LORA, MLA_HEADS * LANE)
    return {
        "w_in": w_in_pad,
        "conv_w": conv_w[l],
        "sgu_w": sgu_w[l],
        "sgu_bt": jnp.transpose(sgu_b[l]),
        "sgu_ln_g": sgu_ln_g[l][None], "sgu_ln_b": sgu_ln_b[l][None],
        "q_norm_g": q_norm_g[l][None], "kv_norm_g": kv_norm_g[l][None],
        "wq_main": wq_main.astype(BF16), "wq_rot": wq_rot.astype(BF16),
        "w_uk": w_uk[l].reshape(KV_LORA, MLA_HEADS * QK_NOPE).astype(BF16),
        "w_uv": w_uv[l].reshape(KV_LORA, MLA_HEADS * V_HEAD).astype(BF16),
        "w_ukt": jnp.transpose(w_uk[l], (1, 2, 0)).astype(BF16),
        "w_uv_h": jnp.transpose(w_uv[l], (1, 0, 2)).astype(BF16),
        "g_mix": mix_norm_g[l][None],
        "w_o": w_o[l].astype(BF16),
    }


def _rope_tables(pos):
    half = QK_ROPE // 2
    inv = ROPE_THETA ** (-jnp.arange(half, dtype=F32) / half)
    ang = pos.astype(F32)[:, None] * inv[None, :]
    zeros = jnp.zeros((pos.shape[0], LANE - QK_ROPE), F32)
    cos = jnp.concatenate([jnp.cos(ang), jnp.cos(ang), zeros], axis=-1)
    sin = jnp.concatenate([jnp.sin(ang), jnp.sin(ang), zeros], axis=-1)
    return cos, sin


def kernel(x_prompt, x_sample, state_conv, cache_ckv, cache_krope, w_in, conv_w, sgu_ln_g,
           sgu_ln_b, sgu_w, sgu_b, q_norm_g, w_uq, kv_norm_g, w_uk, w_uv, mix_norm_g, w_o,
           ln1_g, ln1_b, ln2_g, ln2_b, ffn_w_gate, ffn_w_up, ffn_w_down, router_w,
           moe_w_gate, moe_w_up, moe_w_down):
    nbp, t, d = x_prompt.shape
    nbs, s_len, _ = x_sample.shape
    past = cache_ckv.shape[2]
    depth = w_in.shape[0]
    mp, ms = nbp * t, nbs * s_len

    x = jnp.concatenate([x_prompt.reshape(mp, d), x_sample.reshape(ms, d)], axis=0)
    xb = x.astype(BF16)

    cos_p, sin_p = _rope_tables(jnp.arange(t))
    tms = _tile(ms, 256)
    cos_s, sin_s = _rope_tables(past + jnp.arange(s_len))
    cos_s, sin_s = jnp.tile(cos_s, (tms // s_len, 1)), jnp.tile(sin_s, (tms // s_len, 1))
    zero_state = jnp.zeros((nbp, CONV_WIDTH - 1, CONV_DIM), F32)

    outs = [[] for _ in range(7)]
    for l in range(depth):
        lw = _layer_weights(l, w_in, conv_w, sgu_ln_g, sgu_ln_b, sgu_w, sgu_b, q_norm_g, w_uq,
                            kv_norm_g, w_uk, w_uv, mix_norm_g, w_o)
        proj = _matmul(xb, lw["w_in"], BF16)

        mixed, conv_p = _conv_mixer(proj, zero_state, lw["conv_w"], lw["g_mix"], None, nbp, t, 0)
        mixed, conv_s = _conv_mixer(proj, state_conv[l], lw["conv_w"], lw["g_mix"], mixed,
                                    nbs, s_len, mp // s_len)
        sgu_args = (lw["sgu_w"], lw["sgu_bt"], lw["sgu_ln_g"], lw["sgu_ln_b"], lw["g_mix"])
        chunk_p = min(MLP_CHUNK, t)
        (mixed,) = _sgu_mixer(proj, *sgu_args, mixed, mp, 0, chunk_p, _tile(t, 512) if t >= 512
                              else chunk_p, False)
        chunk_s = min(MLP_CHUNK, s_len)
        mixed, v_rows = _sgu_mixer(proj, *sgu_args, mixed, ms, mp, chunk_s, chunk_s, True)
        q, k, v, ckv_p, kr_p = _mla_pre_prompt(proj, lw, cos_p, sin_p, mp, t)
        mixed = _attn_prompt(q, k, v, lw["g_mix"], mixed, nbp, t)
        q_lat, q_rope, ckv_s, kr_s = _mla_pre_sample(proj, lw, cos_s, sin_s, mp, ms, tms)
        mixed = _attn_sample(q_lat, q_rope, cache_ckv, cache_krope, ckv_s, kr_s, lw["w_uv_h"],
                             lw["g_mix"], mixed, l, mp, nbs, s_len)

        x, xb = _out_proj(mixed, lw["w_o"], x, ln1_g[l][None], ln1_b[l][None])

        i = l // 2
        if l % 2 == 0:
            gates = jnp.zeros((mp + ms, LANE), F32)
            x, xb = _ffn(xb, x, ffn_w_gate[i][None].astype(BF16), ffn_w_up[i][None].astype(BF16),
                         ffn_w_down[i][None].astype(BF16), gates, ln2_g[l][None], ln2_b[l][None],
                         False)
        else:
            rw = jnp.pad(router_w[i], ((0, 0), (0, LANE - N_EXPERTS)))
            rwh = rw.astype(BF16)
            rwl = (rw - rwh.astype(F32)).astype(BF16)
            gates = _router(x, rwh, rwl)
            x, xb = _ffn(xb, x, moe_w_gate[i].astype(BF16), moe_w_up[i].astype(BF16),
                         moe_w_down[i].astype(BF16), gates, ln2_g[l][None], ln2_b[l][None], True)

        for lst, val in zip(outs, (conv_p, ckv_p.reshape(nbp, t, KV_LORA),
                                   kr_p.reshape(nbp, t, QK_ROPE), conv_s,
                                   ckv_s.reshape(nbs, s_len, KV_LORA),
                                   kr_s.reshape(nbs, s_len, QK_ROPE),
                                   v_rows.reshape(nbs, s_len, SGU_DIM))):
            lst.append(val)

    return (x[:mp].reshape(nbp, t, d), x[mp:].reshape(nbs, s_len, d),
            *[jnp.stack(o) for o in outs])
```

```python
import functools

import numpy as np
import jax
import jax.numpy as jnp
from jax import lax
from jax.experimental import pallas as pl
from jax.experimental.pallas import tpu as pltpu

F32 = jnp.float32
BF16 = jnp.bfloat16

D_MODEL = 2048
CONV_DIM = 512
CONV_WIDTH = 3
SGU_DIM = 512
SGU_HEADS = 4
MLP_CHUNK = 128
MLA_HEADS = 8
QK_NOPE = 128
QK_ROPE = 64
V_HEAD = 128
Q_LORA = 768
KV_LORA = 512
ROPE_THETA = 10000.0
CHUNK = 64
HEAD = 128
N_EXPERTS = 8
DEPTH = 2
SM_SCALE = (QK_NOPE + QK_ROPE) ** -0.5
ALPHA = (2 * DEPTH) ** 0.25

COL_BG, COL_CG, COL_HC, COL_U, COL_V = 0, 512, 1024, 1536, 2048
COL_CKV, COL_CQ, COL_KR = 2560, 3072, 3840
D_IN_PAD = 4096
QK_PAD = 256

LANE = 128
VMEM_LIMIT = 56 * 1024 * 1024


def _params(n_axes):
    return pltpu.CompilerParams(dimension_semantics=("arbitrary",) * n_axes,
                                vmem_limit_bytes=VMEM_LIMIT)


def _tile(n, pref):
    t = pref
    while n % t:
        t //= 2
    assert t >= 8, (n, pref)
    return t


def _rms(x, g, eps=1e-6):
    return x * lax.rsqrt(jnp.mean(x * x, axis=-1, keepdims=True) + eps) * g


def _gelu(x):
    c = np.sqrt(2.0 / np.pi).astype(np.float32)
    return 0.5 * x * (1.0 + jnp.tanh(c * (x + 0.044715 * (x * x * x))))


def _dot(a, b):
    return jnp.dot(a, b, preferred_element_type=F32)


def _dot_nt(a, b):
    return lax.dot_general(a, b, (((1,), (1,)), ((), ())), preferred_element_type=F32)


def _mm_kernel(x_ref, w_ref, o_ref):
    o_ref[...] = _dot(x_ref[...], w_ref[...]).astype(o_ref.dtype)


def _matmul(x, w, out_dtype, tm_pref=1024, tn_pref=512):
    m, k = x.shape
    n = w.shape[1]
    tm, tn = _tile(m, tm_pref), _tile(n, tn_pref)
    return pl.pallas_call(
        _mm_kernel,
        grid=(m // tm, n // tn),
        in_specs=[pl.BlockSpec((tm, k), lambda i, j: (i, 0)),
                  pl.BlockSpec((k, tn), lambda i, j: (0, j))],
        out_specs=pl.BlockSpec((tm, tn), lambda i, j: (i, j)),
        out_shape=jax.ShapeDtypeStruct((m, n), out_dtype),
        compiler_params=_params(2),
        name="in_proj",
    )(x, w)


def _conv_kernel(bg_ref, cg_ref, hc_ref, st_ref, w_ref, g_ref, *rest):
    o_ref, ns_ref = rest[-2], rest[-1]
    t = bg_ref.shape[0]
    z = cg_ref[...].astype(F32) * hc_ref[...].astype(F32)
    st = st_ref[0].astype(F32)
    w = w_ref[...]
    row = lax.broadcasted_iota(jnp.int32, z.shape, 0)
    z1 = jnp.where(row == 0, st[1:2], pltpu.roll(z, 1, 0))
    z2 = jnp.where(row == 0, st[0:1],
                   jnp.where(row == 1, st[1:2], pltpu.roll(z, 2, 0)))
    y = w[0:1] * z2 + w[1:2] * z1 + w[2:3] * z
    a = bg_ref[...].astype(F32) * y
    o_ref[...] = _rms(a, g_ref[...]).astype(o_ref.dtype)
    ns_ref[0] = z[t - 2:t, :]


def _conv_mixer(proj, state, conv_w, g_mix, mixed, nb, t, row_blk0):
    m = proj.shape[0]
    ng = CONV_DIM // LANE

    def col(c0):
        return pl.BlockSpec((t, LANE), lambda b, g: (row_blk0 + b, c0 // LANE + g))

    in_specs = [col(COL_BG), col(COL_CG), col(COL_HC),
                pl.BlockSpec((1, CONV_WIDTH - 1, LANE), lambda b, g: (b, 0, g)),
                pl.BlockSpec((CONV_WIDTH, LANE), lambda b, g: (0, g)),
                pl.BlockSpec((1, LANE), lambda b, g: (0, g))]
    args = [proj, proj, proj, state, conv_w, g_mix]
    aliases = {}
    if mixed is not None:
        in_specs.append(pl.BlockSpec(memory_space=pl.ANY))
        args.append(mixed)
        aliases = {6: 0}
    return pl.pallas_call(
        _conv_kernel,
        grid=(nb, ng),
        in_specs=in_specs,
        out_specs=[pl.BlockSpec((t, LANE), lambda b, g: (row_blk0 + b, g)),
                   pl.BlockSpec((1, CONV_WIDTH - 1, LANE), lambda b, g: (b, 0, g))],
        out_shape=[jax.ShapeDtypeStruct((m, D_MODEL), BF16),
                   jax.ShapeDtypeStruct((nb, CONV_WIDTH - 1, CONV_DIM), F32)],
        input_output_aliases=aliases,
        compiler_params=_params(2),
        name="conv_mixer",
    )(*args)


def _sgu_kernel(u_ref, v_ref, w_ref, bt_ref, lg_ref, lb_ref, g_ref, mixed_in_ref, o_ref,
                *vr_refs, chunk):
    del mixed_in_ref
    tr = u_ref.shape[0]
    r = lax.broadcasted_iota(jnp.int32, (chunk, chunk), 0)
    c = lax.broadcasted_iota(jnp.int32, (chunk, chunk), 1)
    for h in range(SGU_HEADS):
        cs = slice(h * HEAD, (h + 1) * HEAD)
        wm = jnp.where(c <= r, w_ref[h][:chunk, :chunk], 0.0).astype(BF16)
        bias = bt_ref[:chunk, h:h + 1]
        for k in range(tr // chunk):
            rs = slice(k * chunk, (k + 1) * chunk)
            vg = _gelu(v_ref[rs, cs].astype(F32))
            xc = vg - jnp.mean(vg, axis=-1, keepdims=True)
            var = jnp.mean(xc * xc, axis=-1, keepdims=True)
            vh = xc * lax.rsqrt(var + 1e-5) * lg_ref[:, cs] + lb_ref[:, cs]
            if vr_refs:
                vr_refs[0][rs, cs] = vh
            s = _dot(wm, vh.astype(BF16)) + bias
            out = _gelu(u_ref[rs, cs].astype(F32)) * s
            o_ref[rs, cs] = _rms(out, g_ref[:, cs]).astype(o_ref.dtype)


def _sgu_mixer(proj, sgu_w, sgu_bt, ln_g, ln_b, g_mix, mixed, nrows, row0, chunk, tr, want_v):
    m = proj.shape[0]
    rb0 = row0 // tr
    full = lambda shape: pl.BlockSpec(shape, lambda i: (0,) * len(shape))
    out_specs = [pl.BlockSpec((tr, SGU_DIM), lambda i: (rb0 + i, 1))]
    out_shape = [jax.ShapeDtypeStruct((m, D_MODEL), BF16)]
    if want_v:
        out_specs.append(pl.BlockSpec((tr, SGU_DIM), lambda i: (i, 0)))
        out_shape.append(jax.ShapeDtypeStruct((nrows, SGU_DIM), F32))
    return pl.pallas_call(
        functools.partial(_sgu_kernel, chunk=chunk),
        grid=(nrows // tr,),
        in_specs=[pl.BlockSpec((tr, SGU_DIM), lambda i: (rb0 + i, COL_U // SGU_DIM)),
                  pl.BlockSpec((tr, SGU_DIM), lambda i: (rb0 + i, COL_V // SGU_DIM)),
                  full((SGU_HEADS, MLP_CHUNK, MLP_CHUNK)),
                  full((MLP_CHUNK, SGU_HEADS)),
                  full((1, SGU_DIM)), full((1, SGU_DIM)),
                  pl.BlockSpec((1, SGU_DIM), lambda i: (0, 1)),
                  pl.BlockSpec(memory_space=pl.ANY)],
        out_specs=out_specs,
        out_shape=out_shape,
        input_output_aliases={7: 0},
        compiler_params=_params(1),
        name="sgu_mixer",
    )(proj, proj, sgu_w, sgu_bt, ln_g, ln_b, g_mix, mixed)


def _rope_q(qm, qr, cos, sin, h):
    nope = qm[:, h * QK_PAD:h * QK_PAD + QK_NOPE]
    rope = (qm[:, h * QK_PAD + QK_NOPE:(h + 1) * QK_PAD] * cos
            + qr[:, h * LANE:(h + 1) * LANE] * sin)
    return nope * SM_SCALE, rope * SM_SCALE


def _rope_k(kr_ref, cos, sin):
    blk = kr_ref[...].astype(F32)
    return blk * cos + pltpu.roll(blk, QK_ROPE, 1) * sin


def _mla_pre_prompt_kernel(cq_ref, ckv_ref, kr_ref, qg_ref, kvg_ref, wqm_ref, wqr_ref,
                           wuk_ref, wuv_ref, cos_ref, sin_ref,
                           q_ref, k_ref, v_ref, ckv_out_ref, kr_out_ref):
    cos, sin = cos_ref[...], sin_ref[...]
    cqn = _rms(cq_ref[...].astype(F32), qg_ref[...]).astype(BF16)
    qm = _dot(cqn, wqm_ref[...])
    qr = _dot(cqn, wqr_ref[...])
    ckv = _rms(ckv_ref[...].astype(F32), kvg_ref[...])
    ckv_out_ref[...] = ckv
    ckb = ckv.astype(BF16)
    kn = _dot(ckb, wuk_ref[...])
    v_ref[...] = _dot(ckb, wuv_ref[...]).astype(v_ref.dtype)
    krp = _rope_k(kr_ref, cos, sin)
    kr_out_ref[...] = krp[:, :QK_ROPE]
    krb = krp.astype(BF16)
    for h in range(MLA_HEADS):
        nope, rope = _rope_q(qm, qr, cos, sin, h)
        q_ref[:, h * QK_PAD:h * QK_PAD + QK_NOPE] = nope.astype(BF16)
        q_ref[:, h * QK_PAD + QK_NOPE:(h + 1) * QK_PAD] = rope.astype(BF16)
        k_ref[:, h * QK_PAD:h * QK_PAD + QK_NOPE] = kn[:, h * LANE:(h + 1) * LANE].astype(BF16)
        k_ref[:, h * QK_PAD + QK_NOPE:(h + 1) * QK_PAD] = krb


def _mla_pre_prompt(proj, lw, cos, sin, mp, t):
    tm = _tile(t, 512)
    nt = t // tm
    full = lambda shape: pl.BlockSpec(shape, lambda i: (0,) * len(shape))
    hq, hv = MLA_HEADS * QK_PAD, MLA_HEADS * V_HEAD
    return pl.pallas_call(
        _mla_pre_prompt_kernel,
        grid=(mp // tm,),
        in_specs=[pl.BlockSpec((tm, Q_LORA), lambda i: (i, COL_CQ // Q_LORA)),
                  pl.BlockSpec((tm, KV_LORA), lambda i: (i, COL_CKV // KV_LORA)),
                  pl.BlockSpec((tm, LANE), lambda i: (i, COL_KR // LANE)),
                  full((1, Q_LORA)), full((1, KV_LORA)),
                  full((Q_LORA, hq)), full((Q_LORA, MLA_HEADS * LANE)),
                  full((KV_LORA, hv)), full((KV_LORA, hv)),
                  pl.BlockSpec((tm, LANE), lambda i: (i % nt, 0)),
                  pl.BlockSpec((tm, LANE), lambda i: (i % nt, 0))],
        out_specs=[pl.BlockSpec((tm, hq), lambda i: (i, 0)),
                   pl.BlockSpec((tm, hq), lambda i: (i, 0)),
                   pl.BlockSpec((tm, hv), lambda i: (i, 0)),
                   pl.BlockSpec((tm, KV_LORA), lambda i: (i, 0)),
                   pl.BlockSpec((tm, QK_ROPE), lambda i: (i, 0))],
        out_shape=[jax.ShapeDtypeStruct((mp, hq), BF16),
                   jax.ShapeDtypeStruct((mp, hq), BF16),
                   jax.ShapeDtypeStruct((mp, hv), BF16),
                   jax.ShapeDtypeStruct((mp, KV_LORA), F32),
                   jax.ShapeDtypeStruct((mp, QK_ROPE), F32)],
        compiler_params=_params(1),
        name="mla_pre_prompt",
    )(proj, proj, proj, lw["q_norm_g"], lw["kv_norm_g"], lw["wq_main"], lw["wq_rot"],
      lw["w_uk"], lw["w_uv"], cos, sin)


def _mla_pre_sample_kernel(cq_ref, ckv_ref, kr_ref, qg_ref, kvg_ref, wqm_ref, wqr_ref,
                           wukt_ref, cos_ref, sin_ref,
                           ql_ref, qr_ref, ckv_out_ref, kr_out_ref):
    cos, sin = cos_ref[...], sin_ref[...]
    cqn = _rms(cq_ref[...].astype(F32), qg_ref[...]).astype(BF16)
    qm = _dot(cqn, wqm_ref[...])
    qr = _dot(cqn, wqr_ref[...])
    ckv_out_ref[...] = _rms(ckv_ref[...].astype(F32), kvg_ref[...])
    kr_out_ref[...] = _rope_k(kr_ref, cos, sin)[:, :QK_ROPE]
    for h in range(MLA_HEADS):
        nope, rope = _rope_q(qm, qr, cos, sin, h)
        ql_ref[h] = _dot(nope.astype(BF16), wukt_ref[h]).astype(BF16)
        qr_ref[h] = rope[:, :QK_ROPE].astype(BF16)


def _mla_pre_sample(proj, lw, cos, sin, mp, ms, tm):
    rb0 = mp // tm
    full = lambda shape: pl.BlockSpec(shape, lambda i: (0,) * len(shape))
    return pl.pallas_call(
        _mla_pre_sample_kernel,
        grid=(ms // tm,),
        in_specs=[pl.BlockSpec((tm, Q_LORA), lambda i: (rb0 + i, COL_CQ // Q_LORA)),
                  pl.BlockSpec((tm, KV_LORA), lambda i: (rb0 + i, COL_CKV // KV_LORA)),
                  pl.BlockSpec((tm, LANE), lambda i: (rb0 + i, COL_KR // LANE)),
                  full((1, Q_LORA)), full((1, KV_LORA)),
                  full((Q_LORA, MLA_HEADS * QK_PAD)), full((Q_LORA, MLA_HEADS * LANE)),
                  full((MLA_HEADS, QK_NOPE, KV_LORA)),
                  full((tm, LANE)), full((tm, LANE))],
        out_specs=[pl.BlockSpec((MLA_HEADS, tm, KV_LORA), lambda i: (0, i, 0)),
                   pl.BlockSpec((MLA_HEADS, tm, QK_ROPE), lambda i: (0, i, 0)),
                   pl.BlockSpec((tm, KV_LORA), lambda i: (i, 0)),
                   pl.BlockSpec((tm, QK_ROPE), lambda i: (i, 0))],
        out_shape=[jax.ShapeDtypeStruct((MLA_HEADS, ms, KV_LORA), BF16),
                   jax.ShapeDtypeStruct((MLA_HEADS, ms, QK_ROPE), BF16),
                   jax.ShapeDtypeStruct((ms, KV_LORA), F32),
                   jax.ShapeDtypeStruct((ms, QK_ROPE), F32)],
        compiler_params=_params(1),
        name="mla_pre_sample",
    )(proj, proj, proj, lw["q_norm_g"], lw["kv_norm_g"], lw["wq_main"], lw["wq_rot"],
      lw["w_ukt"], cos, sin)


def _attn_prompt_kernel(q_ref, k_ref, v_ref, g_ref, mixed_in_ref, o_ref, *, tq):
    del mixed_in_ref
    t = q_ref.shape[0]
    rc = lax.broadcasted_iota(jnp.int32, (tq, tq), 0) // CHUNK
    cc = lax.broadcasted_iota(jnp.int32, (tq, tq), 1) // CHUNK
    visible = cc <= rc
    g = g_ref[...]
    for i in range(t // tq):
        lo, hi = i * tq, (i + 1) * tq
        q = q_ref[lo:hi, :]
        sd = jnp.where(visible, _dot_nt(q, k_ref[lo:hi, :]), -jnp.inf)
        m = jnp.max(sd, axis=-1, keepdims=True)
        if i:
            so = _dot_nt(q, k_ref[0:lo, :])
            m = jnp.maximum(m, jnp.max(so, axis=-1, keepdims=True))
        pd = jnp.exp(sd - m)
        l = jnp.sum(pd, axis=-1, keepdims=True)
        o = _dot(pd.astype(BF16), v_ref[lo:hi, :])
        if i:
            po = jnp.exp(so - m)
            l = l + jnp.sum(po, axis=-1, keepdims=True)
            o = o + _dot(po.astype(BF16), v_ref[0:lo, :])
        o_ref[lo:hi, :] = _rms(o / l, g).astype(o_ref.dtype)


def _attn_prompt(q, k, v, g_mix, mixed, nb, t):
    tq = _tile(t, 256)
    c0 = (CONV_DIM + SGU_DIM) // HEAD
    return pl.pallas_call(
        functools.partial(_attn_prompt_kernel, tq=tq),
        grid=(nb, MLA_HEADS),
        in_specs=[pl.BlockSpec((t, QK_PAD), lambda b, h: (b, h)),
                  pl.BlockSpec((t, QK_PAD), lambda b, h: (b, h)),
                  pl.BlockSpec((t, V_HEAD), lambda b, h: (b, h)),
                  pl.BlockSpec((1, HEAD), lambda b, h: (0, c0 + h)),
                  pl.BlockSpec(memory_space=pl.ANY)],
        out_specs=pl.BlockSpec((t, HEAD), lambda b, h: (b, c0 + h)),
        out_shape=jax.ShapeDtypeStruct(mixed.shape, mixed.dtype),
        input_output_aliases={4: 0},
        compiler_params=_params(2),
        name="attn_prompt",
    )(q, k, v, g_mix, mixed)


def _attn_sample_kernel(ql_ref, qr_ref, cc_ref, ck_ref, nc_ref, nk_ref, wuv_ref, g_ref,
                        mixed_in_ref, o_ref):
    del mixed_in_ref
    s_len = ql_ref.shape[1]
    ql = ql_ref[...].reshape(MLA_HEADS * s_len, KV_LORA)
    qr = qr_ref[...].reshape(MLA_HEADS * s_len, QK_ROPE)
    cc = cc_ref[0, 0].astype(BF16)
    ck = ck_ref[0, 0].astype(BF16)
    nc = nc_ref[...].astype(BF16)
    nk = nk_ref[...].astype(BF16)
    sp = _dot_nt(ql, cc) + _dot_nt(qr, ck)
    sn = _dot_nt(ql, nc) + _dot_nt(qr, nk)
    m = jnp.maximum(jnp.max(sp, axis=-1, keepdims=True), jnp.max(sn, axis=-1, keepdims=True))
    pp, pn = jnp.exp(sp - m), jnp.exp(sn - m)
    l = jnp.sum(pp, axis=-1, keepdims=True) + jnp.sum(pn, axis=-1, keepdims=True)
    o_lat = ((_dot(pp.astype(BF16), cc) + _dot(pn.astype(BF16), nc)) / l).astype(BF16)
    for h in range(MLA_HEADS):
        oh = _dot(o_lat[h * s_len:(h + 1) * s_len], wuv_ref[h])
        o_ref[:, h * HEAD:(h + 1) * HEAD] = _rms(oh, g_ref[:, h * HEAD:(h + 1) * HEAD]
                                                 ).astype(o_ref.dtype)


def _attn_sample(q_lat, q_rope, cache_ckv, cache_krope, ckv_new, kr_new, wuv_h, g_mix, mixed,
                 layer, mp, nb, s_len):
    past = cache_ckv.shape[2]
    rb0 = mp // s_len
    full = lambda shape: pl.BlockSpec(shape, lambda b: (0,) * len(shape))
    return pl.pallas_call(
        _attn_sample_kernel,
        grid=(nb,),
        in_specs=[pl.BlockSpec((MLA_HEADS, s_len, KV_LORA), lambda b: (0, b, 0)),
                  pl.BlockSpec((MLA_HEADS, s_len, QK_ROPE), lambda b: (0, b, 0)),
                  pl.BlockSpec((1, 1, past, KV_LORA), lambda b: (layer, b, 0, 0)),
                  pl.BlockSpec((1, 1, past, QK_ROPE), lambda b: (layer, b, 0, 0)),
                  pl.BlockSpec((s_len, KV_LORA), lambda b: (b, 0)),
                  pl.BlockSpec((s_len, QK_ROPE), lambda b: (b, 0)),
                  full((MLA_HEADS, KV_LORA, V_HEAD)),
                  pl.BlockSpec((1, MLA_HEADS * HEAD), lambda b: (0, 1)),
                  pl.BlockSpec(memory_space=pl.ANY)],
        out_specs=pl.BlockSpec((s_len, MLA_HEADS * HEAD), lambda b: (rb0 + b, 1)),
        out_shape=jax.ShapeDtypeStruct(mixed.shape, mixed.dtype),
        input_output_aliases={8: 0},
        compiler_params=_params(1),
        name="attn_sample",
    )(q_lat, q_rope, cache_ckv, cache_krope, ckv_new, kr_new, wuv_h, g_mix, mixed)


def _layernorm(y, g, b, eps=1e-5):
    yc = y - jnp.mean(y, axis=-1, keepdims=True)
    var = jnp.mean(yc * yc, axis=-1, keepdims=True)
    return yc * lax.rsqrt(var + eps) * g + b


def _out_proj_kernel(mx_ref, w_ref, x_ref, g_ref, b_ref, o_ref, ob_ref):
    y = ALPHA * x_ref[...] + _dot(mx_ref[...], w_ref[...])
    out = _layernorm(y, g_ref[...], b_ref[...])
    o_ref[...] = out
    ob_ref[...] = out.astype(BF16)


def _out_proj(mixed, w_o, x, g, b):
    m, d = x.shape
    tm = _tile(m, 256)
    row = lambda i: (i, 0)
    full = lambda shape: pl.BlockSpec(shape, lambda i: (0,) * len(shape))
    return pl.pallas_call(
        _out_proj_kernel,
        grid=(m // tm,),
        in_specs=[pl.BlockSpec((tm, d), row), full((d, d)), pl.BlockSpec((tm, d), row),
                  full((1, d)), full((1, d))],
        out_specs=[pl.BlockSpec((tm, d), row), pl.BlockSpec((tm, d), row)],
        out_shape=[jax.ShapeDtypeStruct((m, d), F32), jax.ShapeDtypeStruct((m, d), BF16)],
        compiler_params=_params(1),
        name="out_proj_ln",
    )(mixed, w_o, x, g, b)


def _swiglu(xb, wg, wu):
    a = _dot(xb, wg)
    return a * (1.0 / (1.0 + jnp.exp(-a))) * _dot(xb, wu)


def _ffn_kernel(xb_ref, wg_ref, wu_ref, wd_ref, x_ref, g_ref, b_ref, o_ref, ob_ref, acc_ref):
    f = pl.program_id(1)

    @pl.when(f == 0)
    def _():
        acc_ref[...] = jnp.zeros_like(acc_ref)

    h = _swiglu(xb_ref[...], wg_ref[...], wu_ref[...])
    acc_ref[...] += _dot(h.astype(BF16), wd_ref[...])

    @pl.when(f == pl.num_programs(1) - 1)
    def _():
        out = _layernorm(ALPHA * x_ref[...] + acc_ref[...], g_ref[...], b_ref[...])
        o_ref[...] = out
        ob_ref[...] = out.astype(BF16)


def _ffn(xb, x, wg, wu, wd, g, b):
    m, d = x.shape
    ff = wg.shape[1]
    tm, tf = _tile(m, 512), _tile(ff, 512)
    row = lambda i, f: (i, 0)
    vec = pl.BlockSpec((1, d), lambda i, f: (0, 0))
    return pl.pallas_call(
        _ffn_kernel,
        grid=(m // tm, ff // tf),
        in_specs=[pl.BlockSpec((tm, d), row),
                  pl.BlockSpec((d, tf), lambda i, f: (0, f)),
                  pl.BlockSpec((d, tf), lambda i, f: (0, f)),
                  pl.BlockSpec((tf, d), lambda i, f: (f, 0)),
                  pl.BlockSpec((tm, d), row), vec, vec],
        out_specs=[pl.BlockSpec((tm, d), row), pl.BlockSpec((tm, d), row)],
        out_shape=[jax.ShapeDtypeStruct((m, d), F32), jax.ShapeDtypeStruct((m, d), BF16)],
        scratch_shapes=[pltpu.VMEM((tm, d), F32)],
        compiler_params=_params(2),
        name="ffn_dense",
    )(xb, wg, wu, wd, x, g, b)


EXPERT_ROWS = 16


def _router_kernel(x_ref, wh_ref, wl_ref, gate_ref, rank_ref, cnt_ref, carry_ref):
    @pl.when(pl.program_id(0) == 0)
    def _():
        carry_ref[...] = jnp.zeros_like(carry_ref)

    x = x_ref[...]
    ts = x.shape[0]
    xh = x.astype(BF16)
    xl = (x - xh.astype(F32)).astype(BF16)
    wh, wl = wh_ref[...], wl_ref[...]
    logits = _dot_nt(wh, xh) + (_dot_nt(wh, xl) + _dot_nt(wl, xh))
    row = lax.broadcasted_iota(jnp.int32, logits.shape, 0)
    logits = jnp.where(row < N_EXPERTS, logits, -jnp.inf)
    m1 = jnp.max(logits, axis=0, keepdims=True)
    i1 = jnp.min(jnp.where(logits == m1, row, EXPERT_ROWS), axis=0, keepdims=True)
    rest = jnp.where(row == i1, -jnp.inf, logits)
    m2 = jnp.max(rest, axis=0, keepdims=True)
    i2 = jnp.min(jnp.where(rest == m2, row, EXPERT_ROWS), axis=0, keepdims=True)
    e2 = jnp.exp(m2 - m1)
    g1 = 1.0 / (1.0 + e2)
    g2 = e2 / (1.0 + e2)
    gate_ref[...] = jnp.where(row == i1, g1, jnp.where(row == i2, g2, 0.0))
    sel = jnp.where(row == i1, 1.0, jnp.where(row == i2, 1.0, 0.0))
    src = lax.broadcasted_iota(jnp.int32, (ts, ts), 0)
    dst = lax.broadcasted_iota(jnp.int32, (ts, ts), 1)
    incl = _dot(sel.astype(BF16), jnp.where(src <= dst, 1.0, 0.0).astype(BF16))
    carry = carry_ref[:, 0:1]
    rank_ref[...] = jnp.where(sel > 0.0, carry + incl - sel, -1.0).astype(jnp.int32)
    carry = carry + jnp.sum(sel, axis=1, keepdims=True)
    carry_ref[...] = jnp.broadcast_to(carry, carry_ref.shape)
    cnt_ref[0] = jnp.broadcast_to(carry, carry_ref.shape).astype(jnp.int32)


def _router(x, wh, wl, ts):
    m, d = x.shape
    ns = m // ts
    full = pl.BlockSpec((EXPERT_ROWS, d), lambda i: (0, 0))
    col = pl.BlockSpec((EXPERT_ROWS, ts), lambda i: (0, i))
    return pl.pallas_call(
        _router_kernel,
        grid=(ns,),
        in_specs=[pl.BlockSpec((ts, d), lambda i: (i, 0)), full, full],
        out_specs=[col, col, pl.BlockSpec((1, EXPERT_ROWS, LANE), lambda i: (i, 0, 0))],
        out_shape=[jax.ShapeDtypeStruct((EXPERT_ROWS, m), F32),
                   jax.ShapeDtypeStruct((EXPERT_ROWS, m), jnp.int32),
                   jax.ShapeDtypeStruct((ns, EXPERT_ROWS, LANE), jnp.int32)],
        scratch_shapes=[pltpu.VMEM((EXPERT_ROWS, LANE), F32)],
        compiler_params=_params(1),
        name="router",
    )(x, wh, wl)


def _route_plan(cnt, m, ts, td):
    ne, ns = N_EXPERTS, m // ts
    i32 = jnp.int32
    ca = cnt[:, :ne, 0]
    cb = jnp.concatenate([jnp.zeros((1, ne), i32), ca[:-1]], axis=0)
    padded = (ca[-1] + td - 1) // td * td
    ends = jnp.cumsum(padded)
    off = ends - padded
    n_act = ends[-1] // td
    rt = -(-(2 * m + ne * (td - 1)) // td)
    tiles = jnp.minimum(jnp.arange(rt, dtype=i32), n_act - 1)
    tile_expert = jnp.searchsorted(ends // td, tiles, side="right").astype(i32)

    n = (ca - cb).T
    start = off[:, None] + cb.T
    j0 = start // td
    j1 = (start + jnp.maximum(n, 1) - 1) // td
    valid = jnp.stack([n > 0, (n > 0) & (j1 != j0)], axis=-1)
    j = jnp.stack([j0, j1], axis=-1)
    delta = off[:, None, None] - j * td

    vd = valid.reshape(-1)
    jd = jnp.maximum(lax.cummax(jnp.where(vd, j.reshape(-1), -1), axis=0), 0)
    first = jnp.concatenate([jnp.ones((1,), bool), jd[1:] != jd[:-1]])
    sd = jnp.broadcast_to(jnp.arange(ns, dtype=i32)[None, :, None], valid.shape).reshape(-1)
    ed = jnp.broadcast_to(jnp.arange(ne, dtype=i32)[:, None, None], valid.shape).reshape(-1)
    dispatch = (jd.astype(i32), sd, ed, vd.astype(i32) + 2 * first.astype(i32),
                delta.reshape(-1).astype(i32))

    vc = jnp.transpose(valid, (1, 0, 2)).reshape(-1)
    jc = jnp.transpose(j, (1, 0, 2)).reshape(-1)
    last = lax.cummax(jnp.where(vc, jnp.arange(vc.shape[0], dtype=i32), -1), axis=0)
    jc = jnp.where(last >= 0, jc[jnp.maximum(last, 0)], 0)
    combine = (jc.astype(i32), vc.astype(i32),
               jnp.transpose(delta, (1, 0, 2)).reshape(-1).astype(i32))
    return dispatch, combine, tile_expert, n_act.astype(i32).reshape(1), rt


def _one_hot_rows(rank_ref, e, delta, td):
    rank = rank_ref[pl.ds(e, 1), :]
    want = jnp.where(rank >= 0, rank + delta, -1)
    row = lax.broadcasted_iota(jnp.int32, (td, rank.shape[1]), 0)
    return row == want


def _dispatch_kernel(j_ref, s_ref, e_ref, fl_ref, dl_ref, x_ref, rank_ref, gate_ref,
                     xs_ref, gs_ref):
    del j_ref, s_ref
    k = pl.program_id(0)
    flags = fl_ref[k]

    @pl.when(flags >= 2)
    def _():
        xs_ref[...] = jnp.zeros_like(xs_ref)
        gs_ref[...] = jnp.zeros_like(gs_ref)

    @pl.when(flags % 2 == 1)
    def _():
        e = e_ref[k]
        hit = _one_hot_rows(rank_ref, e, dl_ref[k], xs_ref.shape[0])
        rows = _dot(jnp.where(hit, 1.0, 0.0).astype(BF16), x_ref[...])
        xs_ref[...] += rows.astype(xs_ref.dtype)
        gs_ref[...] += jnp.sum(jnp.where(hit, gate_ref[pl.ds(e, 1), :], 0.0), axis=1,
                               keepdims=True)


def _dispatch(plan, xb, rank_t, gate_t, rt, ts, td):
    d = xb.shape[1]
    col = pl.BlockSpec((EXPERT_ROWS, ts), lambda k, j, s, e, fl, dl: (0, s[k]))
    return pl.pallas_call(
        _dispatch_kernel,
        grid_spec=pltpu.PrefetchScalarGridSpec(
            num_scalar_prefetch=5,
            grid=(plan[0].shape[0],),
            in_specs=[pl.BlockSpec((ts, d), lambda k, j, s, e, fl, dl: (s[k], 0)), col, col],
            out_specs=[pl.BlockSpec((td, d), lambda k, j, s, e, fl, dl: (j[k], 0)),
                       pl.BlockSpec((td, 1), lambda k, j, s, e, fl, dl: (j[k], 0))]),
        out_shape=[jax.ShapeDtypeStruct((rt * td, d), BF16),
                   jax.ShapeDtypeStruct((rt * td, 1), F32)],
        compiler_params=_params(1),
        name="moe_dispatch",
    )(*plan, xb, rank_t, gate_t)


def _experts_kernel(te_ref, na_ref, xs_ref, wg_ref, wu_ref, wd_ref, gs_ref, o_ref, acc_ref):
    del te_ref
    j, f = pl.program_id(0), pl.program_id(1)

    @pl.when(j < na_ref[0])
    def _():
        @pl.when(f == 0)
        def _():
            acc_ref[...] = jnp.zeros_like(acc_ref)

        h = _swiglu(xs_ref[...], wg_ref[0], wu_ref[0])
        acc_ref[...] += _dot(h.astype(BF16), wd_ref[0])

        @pl.when(f == pl.num_programs(1) - 1)
        def _():
            o_ref[...] = (acc_ref[...] * gs_ref[...]).astype(o_ref.dtype)


def _experts(tile_expert, n_act, xs, gs, wg, wu, wd, td):
    rows, d = xs.shape
    ff = wg.shape[2]
    tf = _tile(ff, 512)
    nf = ff // tf
    row = lambda j, f, te, na: (jnp.minimum(j, na[0] - 1), 0)
    fcol = lambda j, f, na: jnp.where(j < na[0], f, nf - 1)
    return pl.pallas_call(
        _experts_kernel,
        grid_spec=pltpu.PrefetchScalarGridSpec(
            num_scalar_prefetch=2,
            grid=(rows // td, nf),
            in_specs=[pl.BlockSpec((td, d), row),
                      pl.BlockSpec((1, d, tf), lambda j, f, te, na: (te[j], 0, fcol(j, f, na))),
                      pl.BlockSpec((1, d, tf), lambda j, f, te, na: (te[j], 0, fcol(j, f, na))),
                      pl.BlockSpec((1, tf, d), lambda j, f, te, na: (te[j], fcol(j, f, na), 0)),
                      pl.BlockSpec((td, 1), row)],
            out_specs=pl.BlockSpec((td, d), row),
            scratch_shapes=[pltpu.VMEM((td, d), F32)]),
        out_shape=jax.ShapeDtypeStruct((rows, d), BF16),
        compiler_params=_params(2),
        name="moe_experts",
    )(tile_expert, n_act, xs, wg, wu, wd, gs)


def _combine_kernel(j_ref, fl_ref, dl_ref, o_ref, rank_ref, x_ref, g_ref, b_ref,
                    out_ref, outb_ref, acc_ref):
    del j_ref
    q, nq = pl.program_id(1), pl.num_programs(1)
    k = pl.program_id(0) * nq + q

    @pl.when(q == 0)
    def _():
        acc_ref[...] = jnp.zeros_like(acc_ref)

    @pl.when(fl_ref[k] != 0)
    def _():
        hit = _one_hot_rows(rank_ref, q // 2, dl_ref[k], o_ref.shape[0])
        acc_ref[...] += lax.dot_general(jnp.where(hit, 1.0, 0.0).astype(BF16), o_ref[...],
                                        (((0,), (0,)), ((), ())), preferred_element_type=F32)

    @pl.when(q == nq - 1)
    def _():
        out = _layernorm(ALPHA * x_ref[...] + acc_ref[...], g_ref[...], b_ref[...])
        out_ref[...] = out
        outb_ref[...] = out.astype(BF16)


def _combine(plan, o_sorted, rank_t, x, g, b, ts, td):
    m, d = x.shape
    nq = 2 * N_EXPERTS
    tok = lambda s, q, j, fl, dl: (s, 0)
    vec = pl.BlockSpec((1, d), lambda s, q, j, fl, dl: (0, 0))
    return pl.pallas_call(
        _combine_kernel,
        grid_spec=pltpu.PrefetchScalarGridSpec(
            num_scalar_prefetch=3,
            grid=(m // ts, nq),
            in_specs=[pl.BlockSpec((td, d), lambda s, q, j, fl, dl: (j[s * nq + q], 0)),
                      pl.BlockSpec((EXPERT_ROWS, ts), lambda s, q, j, fl, dl: (0, s)),
                      pl.BlockSpec((ts, d), tok), vec, vec],
            out_specs=[pl.BlockSpec((ts, d), tok), pl.BlockSpec((ts, d), tok)],
            scratch_shapes=[pltpu.VMEM((ts, d), F32)]),
        out_shape=[jax.ShapeDtypeStruct((m, d), F32), jax.ShapeDtypeStruct((m, d), BF16)],
        compiler_params=_params(2),
        name="moe_combine",
    )(*plan, o_sorted, rank_t, x, g, b)


def _moe(x, xb, router_w, wg, wu, wd, g, b):
    m = x.shape[0]
    ts = td = _tile(m, 512)
    rw = jnp.pad(jnp.transpose(router_w), ((0, EXPERT_ROWS - N_EXPERTS), (0, 0)))
    rwh = rw.astype(BF16)
    rwl = (rw - rwh.astype(F32)).astype(BF16)
    gate_t, rank_t, cnt = _router(x, rwh, rwl, ts)
    dplan, cplan, tile_expert, n_act, rt = _route_plan(cnt, m, ts, td)
    xs, gs = _dispatch(dplan, xb, rank_t, gate_t, rt, ts, td)
    o_sorted = _experts(tile_expert, n_act, xs, gs, wg, wu, wd, td)
    return _combine(cplan, o_sorted, rank_t, x, g, b, ts, td)


def _rot_cols(w):
    half = QK_ROPE // 2
    return jnp.concatenate([-w[..., half:], w[..., :half]], axis=-1)


def _layer_weights(l, w_in, conv_w, sgu_ln_g, sgu_ln_b, sgu_w, sgu_b, q_norm_g, w_uq, kv_norm_g,
                   w_uk, w_uv, mix_norm_g, w_o):
    d = w_in.shape[1]
    wi = w_in[l]
    o_cq, o_ckv, o_kr = 2560, 3328, 3840
    k_r = wi[:, o_kr:o_kr + QK_ROPE]
    w_in_pad = jnp.concatenate(
        [wi[:, :o_cq], wi[:, o_ckv:o_kr], wi[:, o_cq:o_ckv], k_r, _rot_cols(k_r),
         jnp.zeros((d, D_IN_PAD - COL_KR - 2 * QK_ROPE), F32)], axis=1).astype(BF16)
    uq = w_uq[l]
    zeros = jnp.zeros((Q_LORA, MLA_HEADS, QK_PAD - QK_NOPE - QK_ROPE), F32)
    wq_main = jnp.concatenate([uq, zeros], axis=-1).reshape(Q_LORA, MLA_HEADS * QK_PAD)
    wq_rot = jnp.concatenate([_rot_cols(uq[..., QK_NOPE:]), zeros], axis=-1
                             ).reshape(Q_LORA, MLA_HEADS * LANE)
    return {
        "w_in": w_in_pad,
        "conv_w": conv_w[l],
        "sgu_w": sgu_w[l],
        "sgu_bt": jnp.transpose(sgu_b[l]),
        "sgu_ln_g": sgu_ln_g[l][None], "sgu_ln_b": sgu_ln_b[l][None],
        "q_norm_g": q_norm_g[l][None], "kv_norm_g": kv_norm_g[l][None],
        "wq_main": wq_main.astype(BF16), "wq_rot": wq_rot.astype(BF16),
        "w_uk": w_uk[l].reshape(KV_LORA, MLA_HEADS * QK_NOPE).astype(BF16),
        "w_uv": w_uv[l].reshape(KV_LORA, MLA_HEADS * V_HEAD).astype(BF16),
        "w_ukt": jnp.transpose(w_uk[l], (1, 2, 0)).astype(BF16),
        "w_uv_h": jnp.transpose(w_uv[l], (1, 0, 2)).astype(BF16),
        "g_mix": mix_norm_g[l][None],
        "w_o": w_o[l].astype(BF16),
    }


def _rope_tables(pos):
    half = QK_ROPE // 2
    inv = ROPE_THETA ** (-jnp.arange(half, dtype=F32) / half)
    ang = pos.astype(F32)[:, None] * inv[None, :]
    zeros = jnp.zeros((pos.shape[0], LANE - QK_ROPE), F32)
    cos = jnp.concatenate([jnp.cos(ang), jnp.cos(ang), zeros], axis=-1)
    sin = jnp.concatenate([jnp.sin(ang), jnp.sin(ang), zeros], axis=-1)
    return cos, sin


def kernel(x_prompt, x_sample, state_conv, cache_ckv, cache_krope, w_in, conv_w, sgu_ln_g,
           sgu_ln_b, sgu_w, sgu_b, q_norm_g, w_uq, kv_norm_g, w_uk, w_uv, mix_norm_g, w_o,
           ln1_g, ln1_b, ln2_g, ln2_b, ffn_w_gate, ffn_w_up, ffn_w_down, router_w,
           moe_w_gate, moe_w_up, moe_w_down):
    nbp, t, d = x_prompt.shape
    nbs, s_len, _ = x_sample.shape
    past = cache_ckv.shape[2]
    depth = w_in.shape[0]
    mp, ms = nbp * t, nbs * s_len

    x = jnp.concatenate([x_prompt.reshape(mp, d), x_sample.reshape(ms, d)], axis=0)
    xb = x.astype(BF16)

    cos_p, sin_p = _rope_tables(jnp.arange(t))
    tms = _tile(ms, 256)
    cos_s, sin_s = _rope_tables(past + jnp.arange(s_len))
    cos_s, sin_s = jnp.tile(cos_s, (tms // s_len, 1)), jnp.tile(sin_s, (tms // s_len, 1))
    zero_state = jnp.zeros((nbp, CONV_WIDTH - 1, CONV_DIM), F32)

    outs = [[] for _ in range(7)]
    for l in range(depth):
        lw = _layer_weights(l, w_in, conv_w, sgu_ln_g, sgu_ln_b, sgu_w, sgu_b, q_norm_g, w_uq,
                            kv_norm_g, w_uk, w_uv, mix_norm_g, w_o)
        proj = _matmul(xb, lw["w_in"], BF16)

        mixed, conv_p = _conv_mixer(proj, zero_state, lw["conv_w"], lw["g_mix"], None, nbp, t, 0)
        mixed, conv_s = _conv_mixer(proj, state_conv[l], lw["conv_w"], lw["g_mix"], mixed,
                                    nbs, s_len, mp // s_len)
        sgu_args = (lw["sgu_w"], lw["sgu_bt"], lw["sgu_ln_g"], lw["sgu_ln_b"], lw["g_mix"])
        chunk_p = min(MLP_CHUNK, t)
        (mixed,) = _sgu_mixer(proj, *sgu_args, mixed, mp, 0, chunk_p, _tile(t, 512) if t >= 512
                              else chunk_p, False)
        chunk_s = min(MLP_CHUNK, s_len)
        mixed, v_rows = _sgu_mixer(proj, *sgu_args, mixed, ms, mp, chunk_s, chunk_s, True)
        q, k, v, ckv_p, kr_p = _mla_pre_prompt(proj, lw, cos_p, sin_p, mp, t)
        mixed = _attn_prompt(q, k, v, lw["g_mix"], mixed, nbp, t)
        q_lat, q_rope, ckv_s, kr_s = _mla_pre_sample(proj, lw, cos_s, sin_s, mp, ms, tms)
        mixed = _attn_sample(q_lat, q_rope, cache_ckv, cache_krope, ckv_s, kr_s, lw["w_uv_h"],
                             lw["g_mix"], mixed, l, mp, nbs, s_len)

        x, xb = _out_proj(mixed, lw["w_o"], x, ln1_g[l][None], ln1_b[l][None])

        i = l // 2
        if l % 2 == 0:
            x, xb = _ffn(xb, x, ffn_w_gate[i].astype(BF16), ffn_w_up[i].astype(BF16),
                         ffn_w_down[i].astype(BF16), ln2_g[l][None], ln2_b[l][None])
        else:
            x, xb = _moe(x, xb, router_w[i], moe_w_gate[i].astype(BF16), moe_w_up[i].astype(BF16),
                         moe_w_down[i].astype(BF16), ln2_g[l][None], ln2_b[l][None])

        for lst, val in zip(outs, (conv_p, ckv_p.reshape(nbp, t, KV_LORA),
                                   kr_p.reshape(nbp, t, QK_ROPE), conv_s,
                                   ckv_s.reshape(nbs, s_len, KV_LORA),
                                   kr_s.reshape(nbs, s_len, QK_ROPE),
                                   v_rows.reshape(nbs, s_len, SGU_DIM))):
            lst.append(val)

    return (x[:mp].reshape(nbp, t, d), x[mp:].reshape(nbs, s_len, d),
            *[jnp.stack(o) for o in outs])
```

```python
import functools

import numpy as np
import jax
import jax.numpy as jnp
from jax import lax
from jax.experimental import pallas as pl
from jax.experimental.pallas import tpu as pltpu
from jax.experimental.pallas import tpu_sc as plsc

F32 = jnp.float32
BF16 = jnp.bfloat16
I32 = jnp.int32

D_MODEL = 2048
CONV_DIM = 512
CONV_WIDTH = 3
SGU_DIM = 512
SGU_HEADS = 4
MLP_CHUNK = 128
MLA_HEADS = 8
QK_NOPE = 128
QK_ROPE = 64
V_HEAD = 128
Q_LORA = 768
KV_LORA = 512
ROPE_THETA = 10000.0
CHUNK = 64
HEAD = 128
N_EXPERTS = 8
DEPTH = 2
SM_SCALE = (QK_NOPE + QK_ROPE) ** -0.5
ALPHA = (2 * DEPTH) ** 0.25

COL_BG, COL_CG, COL_HC, COL_U, COL_V = 0, 512, 1024, 1536, 2048
COL_CKV, COL_CQ, COL_KR = 2560, 3072, 3840
D_IN_PAD = 4096
QK_PAD = 256

LANE = 128
VMEM_LIMIT = 56 * 1024 * 1024
SC_CORES, SC_SUBCORES = 2, 16
SC_ROWS = 32


def _params(n_axes):
    return pltpu.CompilerParams(dimension_semantics=("arbitrary",) * n_axes,
                                vmem_limit_bytes=VMEM_LIMIT)


def _tile(n, pref):
    t = pref
    while n % t:
        t //= 2
    assert t >= 8, (n, pref)
    return t


def _rms(x, g, eps=1e-6):
    return x * lax.rsqrt(jnp.mean(x * x, axis=-1, keepdims=True) + eps) * g


def _layernorm(y, g, b, eps=1e-5):
    yc = y - jnp.mean(y, axis=-1, keepdims=True)
    var = jnp.mean(yc * yc, axis=-1, keepdims=True)
    return yc * lax.rsqrt(var + eps) * g + b


def _gelu(x):
    c = np.sqrt(2.0 / np.pi).astype(np.float32)
    return 0.5 * x * (1.0 + jnp.tanh(c * (x + 0.044715 * (x * x * x))))


def _dot(a, b):
    return jnp.dot(a, b, preferred_element_type=F32)


def _dot_nt(a, b):
    return lax.dot_general(a, b, (((1,), (1,)), ((), ())), preferred_element_type=F32)


def _two_stream_specs(tm, width, n_p, single_buffer=False):
    def p_map(i, *_):
        return (jnp.minimum(i, n_p - 1), 0)

    def s_map(i, *_):
        return (jnp.maximum(i - n_p, 0), 0)

    mode = {"pipeline_mode": pl.Buffered(1)} if single_buffer else {}
    return (pl.BlockSpec((tm, width), p_map, **mode), pl.BlockSpec((tm, width), s_map, **mode))


def _load2(p_ref, s_ref, i, n_p):
    return jnp.where(i < n_p, p_ref[...], s_ref[...])


def _store2(p_ref, s_ref, i, n_p, val):
    @pl.when(i < n_p)
    def _():
        p_ref[...] = val.astype(p_ref.dtype)

    @pl.when(i >= n_p)
    def _():
        s_ref[...] = val.astype(s_ref.dtype)


def _pack_halves(y):
    w = y.shape[1] // 2
    lo = lax.bitcast_convert_type(y[:, :w].astype(BF16).astype(F32), jnp.uint32)
    hi = lax.bitcast_convert_type(y[:, w:].astype(BF16).astype(F32), jnp.uint32)
    return lax.bitcast_convert_type(hi | (lo >> 16), I32)


def _unpack_halves(p):
    u = lax.bitcast_convert_type(p, jnp.uint32)
    lo = lax.bitcast_convert_type(u << 16, F32)
    hi = lax.bitcast_convert_type(u & jnp.uint32(0xFFFF0000), F32)
    return lo, hi


def _mm_kernel(x_ref, w_ref, o_ref):
    o_ref[...] = _dot(x_ref[...], w_ref[...]).astype(o_ref.dtype)


def _matmul(x, w, out_dtype, tm_pref=1024, tn_pref=512):
    m, k = x.shape
    n = w.shape[1]
    tm, tn = _tile(m, tm_pref), _tile(n, tn_pref)
    return pl.pallas_call(
        _mm_kernel,
        grid=(m // tm, n // tn),
        in_specs=[pl.BlockSpec((tm, k), lambda i, j: (i, 0)),
                  pl.BlockSpec((k, tn), lambda i, j: (0, j))],
        out_specs=pl.BlockSpec((tm, tn), lambda i, j: (i, j)),
        out_shape=jax.ShapeDtypeStruct((m, n), out_dtype),
        compiler_params=_params(2),
        name="in_proj",
    )(x, w)


def _conv_kernel(bg_ref, cg_ref, hc_ref, st_ref, w_ref, g_ref, o_ref, ns_ref):
    t = bg_ref.shape[0]
    z = cg_ref[...].astype(F32) * hc_ref[...].astype(F32)
    st = st_ref[0].astype(F32)
    w = w_ref[...]
    row = lax.broadcasted_iota(I32, z.shape, 0)
    z1 = jnp.where(row == 0, st[1:2], pltpu.roll(z, 1, 0))
    z2 = jnp.where(row == 0, st[0:1],
                   jnp.where(row == 1, st[1:2], pltpu.roll(z, 2, 0)))
    y = w[0:1] * z2 + w[1:2] * z1 + w[2:3] * z
    a = bg_ref[...].astype(F32) * y
    o_ref[...] = _rms(a, g_ref[...]).astype(o_ref.dtype)
    ns_ref[0] = z[t - 2:t, :]


def _conv_mixer(proj, state, conv_w, g_mix, nb, t, row_blk0):
    ng = CONV_DIM // LANE

    def col(c0):
        return pl.BlockSpec((t, LANE), lambda b, g: (row_blk0 + b, c0 // LANE + g))

    return pl.pallas_call(
        _conv_kernel,
        grid=(nb, ng),
        in_specs=[col(COL_BG), col(COL_CG), col(COL_HC),
                  pl.BlockSpec((1, CONV_WIDTH - 1, LANE), lambda b, g: (b, 0, g)),
                  pl.BlockSpec((CONV_WIDTH, LANE), lambda b, g: (0, g)),
                  pl.BlockSpec((1, LANE), lambda b, g: (0, g))],
        out_specs=[pl.BlockSpec((t, LANE), lambda b, g: (b, g)),
                   pl.BlockSpec((1, CONV_WIDTH - 1, LANE), lambda b, g: (b, 0, g))],
        out_shape=[jax.ShapeDtypeStruct((nb * t, CONV_DIM), BF16),
                   jax.ShapeDtypeStruct((nb, CONV_WIDTH - 1, CONV_DIM), F32)],
        compiler_params=_params(2),
        name="conv_mixer",
    )(proj, proj, proj, state, conv_w, g_mix)


def _sgu_kernel(u_ref, v_ref, w_ref, bt_ref, lg_ref, lb_ref, g_ref, o_ref, *vr_refs, chunk):
    tr = u_ref.shape[0]
    r = lax.broadcasted_iota(I32, (chunk, chunk), 0)
    c = lax.broadcasted_iota(I32, (chunk, chunk), 1)
    for h in range(SGU_HEADS):
        cs = slice(h * HEAD, (h + 1) * HEAD)
        wm = jnp.where(c <= r, w_ref[h][:chunk, :chunk], 0.0).astype(BF16)
        bias = bt_ref[:chunk, h:h + 1]
        for k in range(tr // chunk):
            rs = slice(k * chunk, (k + 1) * chunk)
            vg = _gelu(v_ref[rs, cs].astype(F32))
            xc = vg - jnp.mean(vg, axis=-1, keepdims=True)
            var = jnp.mean(xc * xc, axis=-1, keepdims=True)
            vh = xc * lax.rsqrt(var + 1e-5) * lg_ref[:, cs] + lb_ref[:, cs]
            if vr_refs:
                vr_refs[0][rs, cs] = vh
            s = _dot(wm, vh.astype(BF16)) + bias
            out = _gelu(u_ref[rs, cs].astype(F32)) * s
            o_ref[rs, cs] = _rms(out, g_ref[:, cs]).astype(o_ref.dtype)


def _sgu_mixer(proj, sgu_w, sgu_bt, ln_g, ln_b, g_mix, nrows, row0, chunk, tr, want_v):
    rb0 = row0 // tr
    full = lambda shape: pl.BlockSpec(shape, lambda i: (0,) * len(shape))
    out_specs = [pl.BlockSpec((tr, SGU_DIM), lambda i: (i, 0))]
    out_shape = [jax.ShapeDtypeStruct((nrows, SGU_DIM), BF16)]
    if want_v:
        out_specs.append(pl.BlockSpec((tr, SGU_DIM), lambda i: (i, 0)))
        out_shape.append(jax.ShapeDtypeStruct((nrows, SGU_DIM), F32))
    return pl.pallas_call(
        functools.partial(_sgu_kernel, chunk=chunk),
        grid=(nrows // tr,),
        in_specs=[pl.BlockSpec((tr, SGU_DIM), lambda i: (rb0 + i, COL_U // SGU_DIM)),
                  pl.BlockSpec((tr, SGU_DIM), lambda i: (rb0 + i, COL_V // SGU_DIM)),
                  full((SGU_HEADS, MLP_CHUNK, MLP_CHUNK)),
                  full((MLP_CHUNK, SGU_HEADS)),
                  full((1, SGU_DIM)), full((1, SGU_DIM)),
                  pl.BlockSpec((1, SGU_DIM), lambda i: (0, 1))],
        out_specs=out_specs,
        out_shape=out_shape,
        compiler_params=_params(1),
        name="sgu_mixer",
    )(proj, proj, sgu_w, sgu_bt, ln_g, ln_b, g_mix)


def _rope_q(qm, qr, cos, sin, h):
    nope = qm[:, h * QK_PAD:h * QK_PAD + QK_NOPE]
    rope = (qm[:, h * QK_PAD + QK_NOPE:(h + 1) * QK_PAD] * cos
            + qr[:, h * LANE:(h + 1) * LANE] * sin)
    return nope * SM_SCALE, rope * SM_SCALE


def _rope_k(kr_ref, cos, sin):
    blk = kr_ref[...].astype(F32)
    return blk * cos + pltpu.roll(blk, QK_ROPE, 1) * sin


def _mla_pre_prompt_kernel(cq_ref, ckv_ref, kr_ref, qg_ref, kvg_ref, wqm_ref, wqr_ref,
                           wuk_ref, wuv_ref, cos_ref, sin_ref,
                           q_ref, k_ref, v_ref, ckv_out_ref, kr_out_ref):
    cos, sin = cos_ref[...], sin_ref[...]
    cqn = _rms(cq_ref[...].astype(F32), qg_ref[...]).astype(BF16)
    qm = _dot(cqn, wqm_ref[...])
    qr = _dot(cqn, wqr_ref[...])
    ckv = _rms(ckv_ref[...].astype(F32), kvg_ref[...])
    ckv_out_ref[...] = ckv
    ckb = ckv.astype(BF16)
    kn = _dot(ckb, wuk_ref[...])
    v_ref[...] = _dot(ckb, wuv_ref[...]).astype(v_ref.dtype)
    krp = _rope_k(kr_ref, cos, sin)
    kr_out_ref[...] = krp[:, :QK_ROPE]
    krb = krp.astype(BF16)
    for h in range(MLA_HEADS):
        nope, rope = _rope_q(qm, qr, cos, sin, h)
        q_ref[:, h * QK_PAD:h * QK_PAD + QK_NOPE] = nope.astype(BF16)
        q_ref[:, h * QK_PAD + QK_NOPE:(h + 1) * QK_PAD] = rope.astype(BF16)
        k_ref[:, h * QK_PAD:h * QK_PAD + QK_NOPE] = kn[:, h * LANE:(h + 1) * LANE].astype(BF16)
        k_ref[:, h * QK_PAD + QK_NOPE:(h + 1) * QK_PAD] = krb


def _mla_pre_prompt(proj, lw, cos, sin, mp, t):
    tm = _tile(t, 512)
    nt = t // tm
    full = lambda shape: pl.BlockSpec(shape, lambda i: (0,) * len(shape))
    hq, hv = MLA_HEADS * QK_PAD, MLA_HEADS * V_HEAD
    return pl.pallas_call(
        _mla_pre_prompt_kernel,
        grid=(mp // tm,),
        in_specs=[pl.BlockSpec((tm, Q_LORA), lambda i: (i, COL_CQ // Q_LORA)),
                  pl.BlockSpec((tm, KV_LORA), lambda i: (i, COL_CKV // KV_LORA)),
                  pl.BlockSpec((tm, LANE), lambda i: (i, COL_KR // LANE)),
                  full((1, Q_LORA)), full((1, KV_LORA)),
                  full((Q_LORA, hq)), full((Q_LORA, MLA_HEADS * LANE)),
                  full((KV_LORA, hv)), full((KV_LORA, hv)),
                  pl.BlockSpec((tm, LANE), lambda i: (i % nt, 0)),
                  pl.BlockSpec((tm, LANE), lambda i: (i % nt, 0))],
        out_specs=[pl.BlockSpec((tm, hq), lambda i: (i, 0)),
                   pl.BlockSpec((tm, hq), lambda i: (i, 0)),
                   pl.BlockSpec((tm, hv), lambda i: (i, 0)),
                   pl.BlockSpec((tm, KV_LORA), lambda i: (i, 0)),
                   pl.BlockSpec((tm, QK_ROPE), lambda i: (i, 0))],
        out_shape=[jax.ShapeDtypeStruct((mp, hq), BF16),
                   jax.ShapeDtypeStruct((mp, hq), BF16),
                   jax.ShapeDtypeStruct((mp, hv), BF16),
                   jax.ShapeDtypeStruct((mp, KV_LORA), F32),
                   jax.ShapeDtypeStruct((mp, QK_ROPE), F32)],
        compiler_params=_params(1),
        name="mla_pre_prompt",
    )(proj, proj, proj, lw["q_norm_g"], lw["kv_norm_g"], lw["wq_main"], lw["wq_rot"],
      lw["w_uk"], lw["w_uv"], cos, sin)


def _mla_pre_sample_kernel(cq_ref, ckv_ref, kr_ref, qg_ref, kvg_ref, wqm_ref, wqr_ref,
                           wukt_ref, cos_ref, sin_ref,
                           ql_ref, qr_ref, ckv_out_ref, kr_out_ref):
    cos, sin = cos_ref[...], sin_ref[...]
    cqn = _rms(cq_ref[...].astype(F32), qg_ref[...]).astype(BF16)
    qm = _dot(cqn, wqm_ref[...])
    qr = _dot(cqn, wqr_ref[...])
    ckv_out_ref[...] = _rms(ckv_ref[...].astype(F32), kvg_ref[...])
    kr_out_ref[...] = _rope_k(kr_ref, cos, sin)[:, :QK_ROPE]
    for h in range(MLA_HEADS):
        nope, rope = _rope_q(qm, qr, cos, sin, h)
        ql_ref[h] = _dot(nope.astype(BF16), wukt_ref[h]).astype(BF16)
        qr_ref[h] = rope[:, :QK_ROPE].astype(BF16)


def _mla_pre_sample(proj, lw, cos, sin, mp, ms, tm):
    rb0 = mp // tm
    full = lambda shape: pl.BlockSpec(shape, lambda i: (0,) * len(shape))
    return pl.pallas_call(
        _mla_pre_sample_kernel,
        grid=(ms // tm,),
        in_specs=[pl.BlockSpec((tm, Q_LORA), lambda i: (rb0 + i, COL_CQ // Q_LORA)),
                  pl.BlockSpec((tm, KV_LORA), lambda i: (rb0 + i, COL_CKV // KV_LORA)),
                  pl.BlockSpec((tm, LANE), lambda i: (rb0 + i, COL_KR // LANE)),
                  full((1, Q_LORA)), full((1, KV_LORA)),
                  full((Q_LORA, MLA_HEADS * QK_PAD)), full((Q_LORA, MLA_HEADS * LANE)),
                  full((MLA_HEADS, QK_NOPE, KV_LORA)),
                  full((tm, LANE)), full((tm, LANE))],
        out_specs=[pl.BlockSpec((MLA_HEADS, tm, KV_LORA), lambda i: (0, i, 0)),
                   pl.BlockSpec((MLA_HEADS, tm, QK_ROPE), lambda i: (0, i, 0)),
                   pl.BlockSpec((tm, KV_LORA), lambda i: (i, 0)),
                   pl.BlockSpec((tm, QK_ROPE), lambda i: (i, 0))],
        out_shape=[jax.ShapeDtypeStruct((MLA_HEADS, ms, KV_LORA), BF16),
                   jax.ShapeDtypeStruct((MLA_HEADS, ms, QK_ROPE), BF16),
                   jax.ShapeDtypeStruct((ms, KV_LORA), F32),
                   jax.ShapeDtypeStruct((ms, QK_ROPE), F32)],
        compiler_params=_params(1),
        name="mla_pre_sample",
    )(proj, proj, proj, lw["q_norm_g"], lw["kv_norm_g"], lw["wq_main"], lw["wq_rot"],
      lw["w_ukt"], cos, sin)


def _attn_prompt_kernel(q_ref, k_ref, v_ref, g_ref, o_ref, *, tq):
    t = q_ref.shape[0]
    rc = lax.broadcasted_iota(I32, (tq, tq), 0) // CHUNK
    cc = lax.broadcasted_iota(I32, (tq, tq), 1) // CHUNK
    visible = cc <= rc
    g = g_ref[...]
    for i in range(t // tq):
        lo, hi = i * tq, (i + 1) * tq
        q = q_ref[lo:hi, :]
        sd = jnp.where(visible, _dot_nt(q, k_ref[lo:hi, :]), -jnp.inf)
        m = jnp.max(sd, axis=-1, keepdims=True)
        if i:
            so = _dot_nt(q, k_ref[0:lo, :])
            m = jnp.maximum(m, jnp.max(so, axis=-1, keepdims=True))
        pd = jnp.exp(sd - m)
        l = jnp.sum(pd, axis=-1, keepdims=True)
        o = _dot(pd.astype(BF16), v_ref[lo:hi, :])
        if i:
            po = jnp.exp(so - m)
            l = l + jnp.sum(po, axis=-1, keepdims=True)
            o = o + _dot(po.astype(BF16), v_ref[0:lo, :])
        o_ref[lo:hi, :] = _rms(o / l, g).astype(o_ref.dtype)


def _attn_prompt(q, k, v, g_mix, nb, t):
    tq = _tile(t, 256)
    c0 = (CONV_DIM + SGU_DIM) // HEAD
    return pl.pallas_call(
        functools.partial(_attn_prompt_kernel, tq=tq),
        grid=(nb, MLA_HEADS),
        in_specs=[pl.BlockSpec((t, QK_PAD), lambda b, h: (b, h)),
                  pl.BlockSpec((t, QK_PAD), lambda b, h: (b, h)),
                  pl.BlockSpec((t, V_HEAD), lambda b, h: (b, h)),
                  pl.BlockSpec((1, HEAD), lambda b, h: (0, c0 + h))],
        out_specs=pl.BlockSpec((t, HEAD), lambda b, h: (b, h)),
        out_shape=jax.ShapeDtypeStruct((nb * t, MLA_HEADS * HEAD), BF16),
        compiler_params=_params(2),
        name="attn_prompt",
    )(q, k, v, g_mix)


def _attn_sample_kernel(ql_ref, qr_ref, cc_ref, ck_ref, nc_ref, nk_ref, wuv_ref, g_ref, o_ref):
    s_len = ql_ref.shape[1]
    ql = ql_ref[...].reshape(MLA_HEADS * s_len, KV_LORA)
    qr = qr_ref[...].reshape(MLA_HEADS * s_len, QK_ROPE)
    cc = cc_ref[0, 0].astype(BF16)
    ck = ck_ref[0, 0].astype(BF16)
    nc = nc_ref[...].astype(BF16)
    nk = nk_ref[...].astype(BF16)
    sp = _dot_nt(ql, cc) + _dot_nt(qr, ck)
    sn = _dot_nt(ql, nc) + _dot_nt(qr, nk)
    m = jnp.maximum(jnp.max(sp, axis=-1, keepdims=True), jnp.max(sn, axis=-1, keepdims=True))
    pp, pn = jnp.exp(sp - m), jnp.exp(sn - m)
    l = jnp.sum(pp, axis=-1, keepdims=True) + jnp.sum(pn, axis=-1, keepdims=True)
    o_lat = ((_dot(pp.astype(BF16), cc) + _dot(pn.astype(BF16), nc)) / l).astype(BF16)
    for h in range(MLA_HEADS):
        oh = _dot(o_lat[h * s_len:(h + 1) * s_len], wuv_ref[h])
        o_ref[:, h * HEAD:(h + 1) * HEAD] = _rms(oh, g_ref[:, h * HEAD:(h + 1) * HEAD]
                                                 ).astype(o_ref.dtype)


def _attn_sample(q_lat, q_rope, cache_ckv, cache_krope, ckv_new, kr_new, wuv_h, g_mix,
                 layer, nb, s_len):
    past = cache_ckv.shape[2]
    full = lambda shape: pl.BlockSpec(shape, lambda b: (0,) * len(shape))
    return pl.pallas_call(
        _attn_sample_kernel,
        grid=(nb,),
        in_specs=[pl.BlockSpec((MLA_HEADS, s_len, KV_LORA), lambda b: (0, b, 0)),
                  pl.BlockSpec((MLA_HEADS, s_len, QK_ROPE), lambda b: (0, b, 0)),
                  pl.BlockSpec((1, 1, past, KV_LORA), lambda b: (layer, b, 0, 0)),
                  pl.BlockSpec((1, 1, past, QK_ROPE), lambda b: (layer, b, 0, 0)),
                  pl.BlockSpec((s_len, KV_LORA), lambda b: (b, 0)),
                  pl.BlockSpec((s_len, QK_ROPE), lambda b: (b, 0)),
                  full((MLA_HEADS, KV_LORA, V_HEAD)),
                  pl.BlockSpec((1, MLA_HEADS * HEAD), lambda b: (0, 1))],
        out_specs=pl.BlockSpec((s_len, MLA_HEADS * HEAD), lambda b: (b, 0)),
        out_shape=jax.ShapeDtypeStruct((nb * s_len, MLA_HEADS * HEAD), BF16),
        compiler_params=_params(1),
        name="attn_sample",
    )(q_lat, q_rope, cache_ckv, cache_krope, ckv_new, kr_new, wuv_h, g_mix)


def _out_proj_kernel(ap_ref, bp_ref, cp_ref, as_ref, bs_ref, cs_ref, w_ref, xp_ref, xs_ref,
                     g_ref, b_ref, op_ref, os_ref, ob_ref, mx_ref, *, n_p, packed):
    i = pl.program_id(0)
    c1, c2 = CONV_DIM, CONV_DIM + SGU_DIM

    @pl.when(i < n_p)
    def _():
        mx_ref[:, :c1] = ap_ref[...]
        mx_ref[:, c1:c2] = bp_ref[...]
        mx_ref[:, c2:] = cp_ref[...]

    @pl.when(i >= n_p)
    def _():
        mx_ref[:, :c1] = as_ref[...]
        mx_ref[:, c1:c2] = bs_ref[...]
        mx_ref[:, c2:] = cs_ref[...]

    y = ALPHA * _load2(xp_ref, xs_ref, i, n_p) + _dot(mx_ref[...], w_ref[...])
    out = _layernorm(y, g_ref[...], b_ref[...])
    _store2(op_ref, os_ref, i, n_p, out)
    ob_ref[...] = _pack_halves(out) if packed else out.astype(BF16)


def _out_proj(mix_p, mix_s, w_o, x_p, x_s, g, b, packed):
    (mp, d), ms = x_p.shape, x_s.shape[0]
    tm = _tile(ms, 256)
    n_p = mp // tm
    m = mp + ms
    full = lambda shape: pl.BlockSpec(shape, lambda i: (0,) * len(shape))
    in_specs = []
    for arrs, clamp in ((mix_p, lambda i: (jnp.minimum(i, n_p - 1), 0)),
                        (mix_s, lambda i: (jnp.maximum(i - n_p, 0), 0))):
        in_specs += [pl.BlockSpec((tm, a.shape[1]), clamp) for a in arrs]
    xp_spec, xs_spec = _two_stream_specs(tm, d, n_p)
    ob_shape = (m, d // 2) if packed else (m, d)
    return pl.pallas_call(
        functools.partial(_out_proj_kernel, n_p=n_p, packed=packed),
        grid=(m // tm,),
        in_specs=in_specs + [full((d, d)), xp_spec, xs_spec, full((1, d)), full((1, d))],
        out_specs=[xp_spec, xs_spec, pl.BlockSpec((tm, ob_shape[1]), lambda i: (i, 0))],
        out_shape=[jax.ShapeDtypeStruct((mp, d), F32), jax.ShapeDtypeStruct((ms, d), F32),
                   jax.ShapeDtypeStruct(ob_shape, I32 if packed else BF16)],
        scratch_shapes=[pltpu.VMEM((tm, d), BF16)],
        compiler_params=_params(1),
        name="out_proj_ln",
    )(*mix_p, *mix_s, w_o, x_p, x_s, g, b)


def _swiglu(xb, wg, wu):
    a = _dot(xb, wg)
    return a * (1.0 / (1.0 + jnp.exp(-a))) * _dot(xb, wu)


def _ffn_kernel(xb_ref, wg_ref, wu_ref, wd_ref, xp_ref, xs_ref, g_ref, b_ref,
                op_ref, os_ref, ob_ref, acc_ref, *, n_p):
    i, f = pl.program_id(0), pl.program_id(1)

    @pl.when(f == 0)
    def _():
        acc_ref[...] = jnp.zeros_like(acc_ref)

    h = _swiglu(xb_ref[...], wg_ref[...], wu_ref[...])
    acc_ref[...] += _dot(h.astype(BF16), wd_ref[...])

    @pl.when(f == pl.num_programs(1) - 1)
    def _():
        y = ALPHA * _load2(xp_ref, xs_ref, i, n_p) + acc_ref[...]
        out = _layernorm(y, g_ref[...], b_ref[...])
        _store2(op_ref, os_ref, i, n_p, out)
        ob_ref[...] = out.astype(BF16)


def _ffn(xb, x_p, x_s, wg, wu, wd, g, b):
    (mp, d), ms = x_p.shape, x_s.shape[0]
    m = mp + ms
    ff = wg.shape[1]
    tm, tf = _tile(ms, 512), _tile(ff, 512)
    n_p = mp // tm
    row = lambda i, f: (i, 0)
    vec = pl.BlockSpec((1, d), lambda i, f: (0, 0))
    xp_spec, xs_spec = _two_stream_specs(tm, d, n_p)
    xp_in, xs_in = _two_stream_specs(tm, d, n_p, single_buffer=True)
    return pl.pallas_call(
        functools.partial(_ffn_kernel, n_p=n_p),
        grid=(m // tm, ff // tf),
        in_specs=[pl.BlockSpec((tm, d), row),
                  pl.BlockSpec((d, tf), lambda i, f: (0, f)),
                  pl.BlockSpec((d, tf), lambda i, f: (0, f)),
                  pl.BlockSpec((tf, d), lambda i, f: (f, 0)),
                  xp_in, xs_in, vec, vec],
        out_specs=[xp_spec, xs_spec, pl.BlockSpec((tm, d), row)],
        out_shape=[jax.ShapeDtypeStruct((mp, d), F32), jax.ShapeDtypeStruct((ms, d), F32),
                   jax.ShapeDtypeStruct((m, d), BF16)],
        scratch_shapes=[pltpu.VMEM((tm, d), F32)],
        compiler_params=_params(2),
        name="ffn_dense",
    )(xb, wg, wu, wd, x_p, x_s, g, b)


EXPERT_ROWS = 16
ROW_TILE = 512


def _router_kernel(xp_ref, xs_ref, wh_ref, wl_ref, gate_ref, rank_ref, cnt_ref, carry_ref, *, n_p):
    i = pl.program_id(0)

    @pl.when(i == 0)
    def _():
        carry_ref[...] = jnp.zeros_like(carry_ref)

    x = _load2(xp_ref, xs_ref, i, n_p)
    ts = x.shape[0]
    xh = x.astype(BF16)
    xl = (x - xh.astype(F32)).astype(BF16)
    wh, wl = wh_ref[...], wl_ref[...]
    logits = _dot_nt(wh, xh) + (_dot_nt(wh, xl) + _dot_nt(wl, xh))
    row = lax.broadcasted_iota(I32, logits.shape, 0)
    logits = jnp.where(row < N_EXPERTS, logits, -jnp.inf)
    m1 = jnp.max(logits, axis=0, keepdims=True)
    i1 = jnp.min(jnp.where(logits == m1, row, EXPERT_ROWS), axis=0, keepdims=True)
    rest = jnp.where(row == i1, -jnp.inf, logits)
    m2 = jnp.max(rest, axis=0, keepdims=True)
    i2 = jnp.min(jnp.where(rest == m2, row, EXPERT_ROWS), axis=0, keepdims=True)
    e2 = jnp.exp(m2 - m1)
    g1 = 1.0 / (1.0 + e2)
    g2 = e2 / (1.0 + e2)
    gate_ref[...] = jnp.where(row == i1, g1, jnp.where(row == i2, g2, 0.0))
    sel = jnp.where(row == i1, 1.0, jnp.where(row == i2, 1.0, 0.0))
    src = lax.broadcasted_iota(I32, (ts, ts), 0)
    dst = lax.broadcasted_iota(I32, (ts, ts), 1)
    incl = _dot(sel.astype(BF16), jnp.where(src <= dst, 1.0, 0.0).astype(BF16))
    carry = carry_ref[:, 0:1]
    rank_ref[...] = jnp.where(sel > 0.0, carry + incl - sel, -1.0).astype(I32)
    carry = carry + jnp.sum(sel, axis=1, keepdims=True)
    carry_ref[...] = jnp.broadcast_to(carry, carry_ref.shape)
    cnt_ref[...] = jnp.broadcast_to(carry, carry_ref.shape).astype(I32)


def _router(x_p, x_s, wh, wl):
    (mp, d), ms = x_p.shape, x_s.shape[0]
    m = mp + ms
    ts = _tile(ms, 512)
    n_p = mp // ts
    full = pl.BlockSpec((EXPERT_ROWS, d), lambda i: (0, 0))
    col = pl.BlockSpec((EXPERT_ROWS, ts), lambda i: (0, i))
    xp_spec, xs_spec = _two_stream_specs(ts, d, n_p)
    return pl.pallas_call(
        functools.partial(_router_kernel, n_p=n_p),
        grid=(m // ts,),
        in_specs=[xp_spec, xs_spec, full, full],
        out_specs=[col, col, pl.BlockSpec((EXPERT_ROWS, LANE), lambda i: (0, 0))],
        out_shape=[jax.ShapeDtypeStruct((EXPERT_ROWS, m), F32),
                   jax.ShapeDtypeStruct((EXPERT_ROWS, m), I32),
                   jax.ShapeDtypeStruct((EXPERT_ROWS, LANE), I32)],
        scratch_shapes=[pltpu.VMEM((EXPERT_ROWS, LANE), F32)],
        compiler_params=_params(1),
        name="router",
    )(x_p, x_s, wh, wl)


def _route_plan(gate_t, rank_t, cnt, m, td):
    ne = N_EXPERTS
    total = cnt[:ne, 0]
    padded = (total + td - 1) // td * td
    ends = jnp.cumsum(padded)
    off = ends - padded
    n_act = (ends[-1] // td).astype(I32)
    rt = -(-(2 * m + ne * (td - 1)) // td)
    tiles = jnp.minimum(jnp.arange(rt, dtype=I32), n_act - 1)
    tile_expert = jnp.sum(tiles[:, None] >= (ends // td)[None, :], axis=1).astype(I32)

    rank, gate = rank_t[:ne], gate_t[:ne]
    chosen = rank >= 0
    pos = off[:, None] + rank
    pos_a = jnp.min(jnp.where(chosen, pos, rt * td), axis=0).astype(I32)
    pos_b = jnp.max(jnp.where(chosen, pos, -1), axis=0).astype(I32)
    gate_a = jnp.sum(jnp.where(chosen & (pos == pos_a[None]), gate, 0.0), axis=0)
    gate_b = jnp.sum(jnp.where(chosen & (pos == pos_b[None]), gate, 0.0), axis=0)
    return pos_a, pos_b, gate_a[:, None], gate_b[:, None], tile_expert, n_act.reshape(1), rt


def _sc_mesh():
    return plsc.VectorSubcoreMesh(core_axis_name="c", subcore_axis_name="s")


def _sc_worker_base(per_worker):
    wid = lax.axis_index("s") * SC_CORES + lax.axis_index("c")
    return wid * per_worker


def _sc_scatter_rows(x, idx_a, idx_b, n_rows):
    m, w = x.shape
    per_worker = m // (SC_CORES * SC_SUBCORES)
    assert per_worker * SC_CORES * SC_SUBCORES == m and per_worker % SC_ROWS == 0, m

    @functools.partial(
        pl.kernel, mesh=_sc_mesh(),
        out_type=jax.ShapeDtypeStruct((n_rows, w), x.dtype),
        scratch_types=[pltpu.VMEM((SC_ROWS,), I32), pltpu.VMEM((SC_ROWS,), I32),
                       pltpu.VMEM((SC_ROWS, w), x.dtype),
                       pltpu.SemaphoreType.DMA, pltpu.SemaphoreType.DMA])
    def scatter(x_hbm, ia_hbm, ib_hbm, out_hbm, ia_v, ib_v, rows_v, sem_a, sem_b):
        base = _sc_worker_base(per_worker)

        @pl.loop(0, per_worker // SC_ROWS)
        def _(c):
            lo = pl.multiple_of(base + c * SC_ROWS, 8)
            pltpu.sync_copy(ia_hbm.at[pl.ds(lo, SC_ROWS)], ia_v)
            pltpu.sync_copy(ib_hbm.at[pl.ds(lo, SC_ROWS)], ib_v)
            pltpu.sync_copy(x_hbm.at[pl.ds(lo, SC_ROWS)], rows_v)
            put_a = pltpu.async_copy(rows_v, out_hbm.at[ia_v], sem_a)
            put_b = pltpu.async_copy(rows_v, out_hbm.at[ib_v], sem_b)
            put_a.wait()
            put_b.wait()

    return scatter(x, idx_a, idx_b)


def _sc_gather_rows(table, idx):
    m, w = idx.shape[0], table.shape[1]
    per_worker = m // (SC_CORES * SC_SUBCORES)
    assert per_worker * SC_CORES * SC_SUBCORES == m and per_worker % SC_ROWS == 0, m

    @functools.partial(
        pl.kernel, mesh=_sc_mesh(),
        out_type=jax.ShapeDtypeStruct((m, w), table.dtype),
        scratch_types=[pltpu.VMEM((SC_ROWS,), I32), pltpu.VMEM((SC_ROWS, w), table.dtype),
                       pltpu.SemaphoreType.DMA])
    def gather(table_hbm, idx_hbm, out_hbm, idx_v, rows_v, sem):
        base = _sc_worker_base(per_worker)

        @pl.loop(0, per_worker // SC_ROWS)
        def _(c):
            lo = pl.multiple_of(base + c * SC_ROWS, 8)
            pltpu.sync_copy(idx_hbm.at[pl.ds(lo, SC_ROWS)], idx_v)
            pltpu.async_copy(table_hbm.at[idx_v], rows_v, sem).wait()
            pltpu.sync_copy(rows_v, out_hbm.at[pl.ds(lo, SC_ROWS)])

    return gather(table, idx)


def _experts_kernel(te_ref, na_ref, xs_ref, wg_ref, wu_ref, wd_ref, o_ref, xb_ref, acc_ref):
    del te_ref
    j, f = pl.program_id(0), pl.program_id(1)
    half = xs_ref.shape[1]

    @pl.when(j < na_ref[0])
    def _():
        @pl.when(f == 0)
        def _():
            lo, hi = _unpack_halves(xs_ref[...])
            xb_ref[:, :half] = lo.astype(BF16)
            xb_ref[:, half:] = hi.astype(BF16)
            acc_ref[...] = jnp.zeros_like(acc_ref)

        h = _swiglu(xb_ref[...], wg_ref[0], wu_ref[0])
        acc_ref[...] += _dot(h.astype(BF16), wd_ref[0])

        @pl.when(f == pl.num_programs(1) - 1)
        def _():
            o_ref[...] = _pack_halves(acc_ref[...])


def _experts(tile_expert, n_act, xs, wg, wu, wd, td):
    rows, half = xs.shape
    d = 2 * half
    ff = wg.shape[2]
    tf = _tile(ff, 512)
    nf = ff // tf
    row = lambda j, f, te, na: (jnp.minimum(j, na[0] - 1), 0)
    fcol = lambda j, f, na: jnp.where(j < na[0], f, nf - 1)
    return pl.pallas_call(
        _experts_kernel,
        grid_spec=pltpu.PrefetchScalarGridSpec(
            num_scalar_prefetch=2,
            grid=(rows // td, nf),
            in_specs=[pl.BlockSpec((td, half), row),
                      pl.BlockSpec((1, d, tf), lambda j, f, te, na: (te[j], 0, fcol(j, f, na))),
                      pl.BlockSpec((1, d, tf), lambda j, f, te, na: (te[j], 0, fcol(j, f, na))),
                      pl.BlockSpec((1, tf, d), lambda j, f, te, na: (te[j], fcol(j, f, na), 0))],
            out_specs=pl.BlockSpec((td, half), row),
            scratch_shapes=[pltpu.VMEM((td, d), BF16), pltpu.VMEM((td, d), F32)]),
        out_shape=jax.ShapeDtypeStruct((rows, half), I32),
        compiler_params=_params(2),
        name="moe_experts",
    )(tile_expert, n_act, xs, wg, wu, wd)


def _moe_finish_kernel(oa_ref, ob_ref, ga_ref, gb_ref, xp_ref, xs_ref, g_ref, b_ref,
                       op_ref, os_ref, *, n_p):
    i = pl.program_id(0)
    a_lo, a_hi = _unpack_halves(oa_ref[...])
    b_lo, b_hi = _unpack_halves(ob_ref[...])
    ga, gb = ga_ref[...], gb_ref[...]
    y = jnp.concatenate([ga * a_lo + gb * b_lo, ga * a_hi + gb * b_hi], axis=1)
    out = _layernorm(ALPHA * _load2(xp_ref, xs_ref, i, n_p) + y, g_ref[...], b_ref[...])
    _store2(op_ref, os_ref, i, n_p, out)


def _moe_finish(o_a, o_b, gate_a, gate_b, x_p, x_s, g, b):
    (mp, d), ms = x_p.shape, x_s.shape[0]
    m = mp + ms
    tm = _tile(ms, 512)
    n_p = mp // tm
    row = lambda i: (i, 0)
    vec = pl.BlockSpec((1, d), lambda i: (0, 0))
    xp_spec, xs_spec = _two_stream_specs(tm, d, n_p)
    return pl.pallas_call(
        functools.partial(_moe_finish_kernel, n_p=n_p),
        grid=(m // tm,),
        in_specs=[pl.BlockSpec((tm, d // 2), row), pl.BlockSpec((tm, d // 2), row),
                  pl.BlockSpec((tm, 1), row), pl.BlockSpec((tm, 1), row),
                  xp_spec, xs_spec, vec, vec],
        out_specs=[xp_spec, xs_spec],
        out_shape=[jax.ShapeDtypeStruct((mp, d), F32), jax.ShapeDtypeStruct((ms, d), F32)],
        compiler_params=_params(1),
        name="moe_finish",
    )(o_a, o_b, gate_a, gate_b, x_p, x_s, g, b)


def _moe(x_p, x_s, x_packed, router_w, wg, wu, wd, g, b):
    m = x_p.shape[0] + x_s.shape[0]
    rw = jnp.pad(jnp.transpose(router_w), ((0, EXPERT_ROWS - N_EXPERTS), (0, 0)))
    rwh = rw.astype(BF16)
    rwl = (rw - rwh.astype(F32)).astype(BF16)
    gate_t, rank_t, cnt = _router(x_p, x_s, rwh, rwl)
    td = min(ROW_TILE, _tile(m, ROW_TILE))
    pos_a, pos_b, gate_a, gate_b, tile_expert, n_act, rt = _route_plan(gate_t, rank_t, cnt, m, td)
    xs = _sc_scatter_rows(x_packed, pos_a, pos_b, rt * td)
    o_sorted = _experts(tile_expert, n_act, xs, wg, wu, wd, td)
    o_a = _sc_gather_rows(o_sorted, pos_a)
    o_b = _sc_gather_rows(o_sorted, pos_b)
    return _moe_finish(o_a, o_b, gate_a, gate_b, x_p, x_s, g, b)


def _rot_cols(w):
    half = QK_ROPE // 2
    return jnp.concatenate([-w[..., half:], w[..., :half]], axis=-1)


def _layer_weights(l, w_in, conv_w, sgu_ln_g, sgu_ln_b, sgu_w, sgu_b, q_norm_g, w_uq, kv_norm_g,
                   w_uk, w_uv, mix_norm_g, w_o):
    d = w_in.shape[1]
    wi = w_in[l]
    o_cq, o_ckv, o_kr = 2560, 3328, 3840
    k_r = wi[:, o_kr:o_kr + QK_ROPE]
    w_in_pad = jnp.concatenate(
        [wi[:, :o_cq], wi[:, o_ckv:o_kr], wi[:, o_cq:o_ckv], k_r, _rot_cols(k_r),
         jnp.zeros((d, D_IN_PAD - COL_KR - 2 * QK_ROPE), F32)], axis=1).astype(BF16)
    uq = w_uq[l]
    zeros = jnp.zeros((Q_LORA, MLA_HEADS, QK_PAD - QK_NOPE - QK_ROPE), F32)
    wq_main = jnp.concatenate([uq, zeros], axis=-1).reshape(Q_LORA, MLA_HEADS * QK_PAD)
    wq_rot = jnp.concatenate([_rot_cols(uq[..., QK_NOPE:]), zeros], axis=-1
                             ).reshape(Q_LORA, MLA_HEADS * LANE)
    return {
        "w_in": w_in_pad,
        "conv_w": conv_w[l],
        "sgu_w": sgu_w[l],
        "sgu_bt": jnp.transpose(sgu_b[l]),
        "sgu_ln_g": sgu_ln_g[l][None], "sgu_ln_b": sgu_ln_b[l][None],
        "q_norm_g": q_norm_g[l][None], "kv_norm_g": kv_norm_g[l][None],
        "wq_main": wq_main.astype(BF16), "wq_rot": wq_rot.astype(BF16),
        "w_uk": w_uk[l].reshape(KV_LORA, MLA_HEADS * QK_NOPE).astype(BF16),
        "w_uv": w_uv[l].reshape(KV_LORA, MLA_HEADS * V_HEAD).astype(BF16),
        "w_ukt": jnp.transpose(w_uk[l], (1, 2, 0)).astype(BF16),
        "w_uv_h": jnp.transpose(w_uv[l], (1, 0, 2)).astype(BF16),
        "g_mix": mix_norm_g[l][None],
        "w_o": w_o[l].astype(BF16),
    }


def _rope_tables(pos):
    half = QK_ROPE // 2
    inv = ROPE_THETA ** (-jnp.arange(half, dtype=F32) / half)
    ang = pos.astype(F32)[:, None] * inv[None, :]
    zeros = jnp.zeros((pos.shape[0], LANE - QK_ROPE), F32)
    cos = jnp.concatenate([jnp.cos(ang), jnp.cos(ang), zeros], axis=-1)
    sin = jnp.concatenate([jnp.sin(ang), jnp.sin(ang), zeros], axis=-1)
    return cos, sin


def kernel(x_prompt, x_sample, state_conv, cache_ckv, cache_krope, w_in, conv_w, sgu_ln_g,
           sgu_ln_b, sgu_w, sgu_b, q_norm_g, w_uq, kv_norm_g, w_uk, w_uv, mix_norm_g, w_o,
           ln1_g, ln1_b, ln2_g, ln2_b, ffn_w_gate, ffn_w_up, ffn_w_down, router_w,
           moe_w_gate, moe_w_up, moe_w_down):
    nbp, t, d = x_prompt.shape
    nbs, s_len, _ = x_sample.shape
    past = cache_ckv.shape[2]
    depth = w_in.shape[0]
    mp, ms = nbp * t, nbs * s_len

    x_p, x_s = x_prompt.reshape(mp, d), x_sample.reshape(ms, d)
    xb = jnp.concatenate([x_p.astype(BF16), x_s.astype(BF16)], axis=0)

    cos_p, sin_p = _rope_tables(jnp.arange(t))
    tms = _tile(ms, 256)
    cos_s, sin_s = _rope_tables(past + jnp.arange(s_len))
    cos_s, sin_s = jnp.tile(cos_s, (tms // s_len, 1)), jnp.tile(sin_s, (tms // s_len, 1))
    zero_state = jnp.zeros((nbp, CONV_WIDTH - 1, CONV_DIM), F32)

    outs = [[] for _ in range(7)]
    for l in range(depth):
        lw = _layer_weights(l, w_in, conv_w, sgu_ln_g, sgu_ln_b, sgu_w, sgu_b, q_norm_g, w_uq,
                            kv_norm_g, w_uk, w_uv, mix_norm_g, w_o)
        routed = l % 2 == 1
        proj = _matmul(xb, lw["w_in"], BF16)

        a_p, conv_p = _conv_mixer(proj, zero_state, lw["conv_w"], lw["g_mix"], nbp, t, 0)
        a_s, conv_s = _conv_mixer(proj, state_conv[l], lw["conv_w"], lw["g_mix"], nbs, s_len,
                                  mp // s_len)
        sgu_args = (lw["sgu_w"], lw["sgu_bt"], lw["sgu_ln_g"], lw["sgu_ln_b"], lw["g_mix"])
        chunk_p = min(MLP_CHUNK, t)
        (b_p,) = _sgu_mixer(proj, *sgu_args, mp, 0, chunk_p,
                            _tile(t, 512) if t >= 512 else chunk_p, False)
        chunk_s = min(MLP_CHUNK, s_len)
        b_s, v_rows = _sgu_mixer(proj, *sgu_args, ms, mp, chunk_s, chunk_s, True)
        q, k, v, ckv_p, kr_p = _mla_pre_prompt(proj, lw, cos_p, sin_p, mp, t)
        c_p = _attn_prompt(q, k, v, lw["g_mix"], nbp, t)
        q_lat, q_rope, ckv_s, kr_s = _mla_pre_sample(proj, lw, cos_s, sin_s, mp, ms, tms)
        c_s = _attn_sample(q_lat, q_rope, cache_ckv, cache_krope, ckv_s, kr_s, lw["w_uv_h"],
                           lw["g_mix"], l, nbs, s_len)

        x_p, x_s, xb = _out_proj((a_p, b_p, c_p), (a_s, b_s, c_s), lw["w_o"], x_p, x_s,
                                 ln1_g[l][None], ln1_b[l][None], routed)

        i = l // 2
        if routed:
            x_p, x_s = _moe(x_p, x_s, xb, router_w[i], moe_w_gate[i].astype(BF16),
                            moe_w_up[i].astype(BF16), moe_w_down[i].astype(BF16),
                            ln2_g[l][None], ln2_b[l][None])
            if l + 1 < depth:
                xb = jnp.concatenate([x_p.astype(BF16), x_s.astype(BF16)], axis=0)
        else:
            x_p, x_s, xb = _ffn(xb, x_p, x_s, ffn_w_gate[i].astype(BF16),
                                ffn_w_up[i].astype(BF16), ffn_w_down[i].astype(BF16),
                                ln2_g[l][None], ln2_b[l][None])

        for lst, val in zip(outs, (conv_p, ckv_p.reshape(nbp, t, KV_LORA),
                                   kr_p.reshape(nbp, t, QK_ROPE), conv_s,
                                   ckv_s.reshape(nbs, s_len, KV_LORA),
                                   kr_s.reshape(nbs, s_len, QK_ROPE),
                                   v_rows.reshape(nbs, s_len, SGU_DIM))):
            lst.append(val)

    return (x_p.reshape(nbp, t, d), x_s.reshape(nbs, s_len, d), *[jnp.stack(o) for o in outs])
```

```python
import functools

import numpy as np
import jax
import jax.numpy as jnp
from jax import lax
from jax.experimental import pallas as pl
from jax.experimental.pallas import tpu as pltpu
from jax.experimental.pallas import tpu_sc as plsc

F32 = jnp.float32
BF16 = jnp.bfloat16
I32 = jnp.int32

D_MODEL = 2048
CONV_DIM = 512
CONV_WIDTH = 3
SGU_DIM = 512
SGU_HEADS = 4
MLP_CHUNK = 128
MLA_HEADS = 8
QK_NOPE = 128
QK_ROPE = 64
V_HEAD = 128
Q_LORA = 768
KV_LORA = 512
ROPE_THETA = 10000.0
CHUNK = 64
HEAD = 128
N_EXPERTS = 8
DEPTH = 2
SM_SCALE = (QK_NOPE + QK_ROPE) ** -0.5
ALPHA = (2 * DEPTH) ** 0.25

COL_BG, COL_CG, COL_HC, COL_U, COL_V = 0, 512, 1024, 1536, 2048
COL_CQ, COL_CKV, COL_KR = 2560, 3328, 3840
D_IN_PAD = 4096
LORA_BLK = 256
N_CQ_BLK, N_CKV_BLK = Q_LORA // LORA_BLK, KV_LORA // LORA_BLK
QK_PAD = 256

LANE = 128
VMEM_LIMIT = 56 * 1024 * 1024
SC_CORES, SC_SUBCORES = 2, 16
SC_ROWS = 32


def _params(n_axes):
    return pltpu.CompilerParams(dimension_semantics=("arbitrary",) * n_axes,
                                vmem_limit_bytes=VMEM_LIMIT)


def _tile(n, pref):
    t = pref
    while n % t:
        t //= 2
    assert t >= 8, (n, pref)
    return t


def _rms(x, g, eps=1e-6):
    return x * lax.rsqrt(jnp.mean(x * x, axis=-1, keepdims=True) + eps) * g


def _layernorm(y, g, b, eps=1e-5):
    yc = y - jnp.mean(y, axis=-1, keepdims=True)
    var = jnp.mean(yc * yc, axis=-1, keepdims=True)
    return yc * lax.rsqrt(var + eps) * g + b


def _gelu(x):
    c = np.sqrt(2.0 / np.pi).astype(np.float32)
    return 0.5 * x * (1.0 + jnp.tanh(c * (x + 0.044715 * (x * x * x))))


def _dot(a, b):
    return jnp.dot(a, b, preferred_element_type=F32)


def _dot_nt(a, b):
    return lax.dot_general(a, b, (((1,), (1,)), ((), ())), preferred_element_type=F32)


def _two_stream_specs(tm, width, n_p, single_buffer=False):
    def p_map(i, *_):
        return (jnp.minimum(i, n_p - 1), 0)

    def s_map(i, *_):
        return (jnp.maximum(i - n_p, 0), 0)

    mode = {"pipeline_mode": pl.Buffered(1)} if single_buffer else {}
    return (pl.BlockSpec((tm, width), p_map, **mode), pl.BlockSpec((tm, width), s_map, **mode))


def _load2(p_ref, s_ref, i, n_p):
    return jnp.where(i < n_p, p_ref[...], s_ref[...])


def _store2(p_ref, s_ref, i, n_p, val):
    @pl.when(i < n_p)
    def _():
        p_ref[...] = val.astype(p_ref.dtype)

    @pl.when(i >= n_p)
    def _():
        s_ref[...] = val.astype(s_ref.dtype)


def _pack_halves(y):
    w = y.shape[1] // 2
    lo = lax.bitcast_convert_type(y[:, :w].astype(BF16).astype(F32), jnp.uint32)
    hi = lax.bitcast_convert_type(y[:, w:].astype(BF16).astype(F32), jnp.uint32)
    return lax.bitcast_convert_type(hi | (lo >> 16), I32)


def _unpack_halves(p):
    u = lax.bitcast_convert_type(p, jnp.uint32)
    lo = lax.bitcast_convert_type(u << 16, F32)
    hi = lax.bitcast_convert_type(u & jnp.uint32(0xFFFF0000), F32)
    return lo, hi


def _in_proj_kernel(xp_ref, xs_ref, w_ref, o_ref, xb_ref, *, n_p):
    @pl.when(pl.program_id(1) == 0)
    def _():
        xb_ref[...] = _load2(xp_ref, xs_ref, pl.program_id(0), n_p).astype(BF16)

    o_ref[...] = _dot(xb_ref[...], w_ref[...]).astype(o_ref.dtype)


def _in_proj(x_p, x_s, w):
    (mp, k), ms = x_p.shape, x_s.shape[0]
    m, n = mp + ms, w.shape[1]
    tm, tn = _tile(ms, 512), _tile(n, 512)
    n_p = mp // tm
    xp_spec, xs_spec = _two_stream_specs(tm, k, n_p)
    return pl.pallas_call(
        functools.partial(_in_proj_kernel, n_p=n_p),
        grid=(m // tm, n // tn),
        in_specs=[xp_spec, xs_spec, pl.BlockSpec((k, tn), lambda i, j: (0, j))],
        out_specs=pl.BlockSpec((tm, tn), lambda i, j: (i, j)),
        out_shape=jax.ShapeDtypeStruct((m, n), BF16),
        scratch_shapes=[pltpu.VMEM((tm, k), BF16)],
        compiler_params=_params(2),
        name="in_proj",
    )(x_p, x_s, w)


def _conv_kernel(bg_ref, cg_ref, hc_ref, st_ref, w_ref, g_ref, o_ref, ns_ref, *, t):
    rows = bg_ref.shape[0]
    nseq = rows // t
    row = lax.broadcasted_iota(I32, (rows, LANE), 0)
    for c in range(CONV_DIM // LANE):
        cs = slice(c * LANE, (c + 1) * LANE)
        z = cg_ref[:, cs].astype(F32) * hc_ref[:, cs].astype(F32)
        z1, z2 = pltpu.roll(z, 1, 0), pltpu.roll(z, 2, 0)
        for s in range(nseq):
            prev2, prev1 = st_ref[s, 0:1, cs], st_ref[s, 1:2, cs]
            z1 = jnp.where(row == s * t, prev1, z1)
            z2 = jnp.where(row == s * t, prev2, jnp.where(row == s * t + 1, prev1, z2))
            ns_ref[s, :, cs] = z[(s + 1) * t - 2:(s + 1) * t, :]
        w = w_ref[:, cs]
        a = bg_ref[:, cs].astype(F32) * (w[0:1] * z2 + w[1:2] * z1 + w[2:3] * z)
        o_ref[:, cs] = _rms(a, g_ref[:, cs]).astype(o_ref.dtype)


def _conv_mixer(proj, state, conv_w, g_mix, nb, t, row0):
    nseq = max(1, min(nb, 256 // t))
    rows = nseq * t
    rb0 = row0 // rows

    def col(c0):
        return pl.BlockSpec((rows, CONV_DIM), lambda b: (rb0 + b, c0 // CONV_DIM))

    state_spec = pl.BlockSpec((nseq, CONV_WIDTH - 1, CONV_DIM), lambda b: (b, 0, 0))
    return pl.pallas_call(
        functools.partial(_conv_kernel, t=t),
        grid=(nb // nseq,),
        in_specs=[col(COL_BG), col(COL_CG), col(COL_HC), state_spec,
                  pl.BlockSpec((CONV_WIDTH, CONV_DIM), lambda b: (0, 0)),
                  pl.BlockSpec((1, CONV_DIM), lambda b: (0, 0))],
        out_specs=[pl.BlockSpec((rows, CONV_DIM), lambda b: (b, 0)), state_spec],
        out_shape=[jax.ShapeDtypeStruct((nb * t, CONV_DIM), BF16),
                   jax.ShapeDtypeStruct((nb, CONV_WIDTH - 1, CONV_DIM), F32)],
        compiler_params=_params(1),
        name="conv_mixer",
    )(proj, proj, proj, state, conv_w, g_mix)


def _sgu_kernel(u_ref, v_ref, w_ref, bt_ref, lg_ref, lb_ref, g_ref, o_ref, *vr_refs, chunk):
    tr = u_ref.shape[0]
    r = lax.broadcasted_iota(I32, (chunk, chunk), 0)
    c = lax.broadcasted_iota(I32, (chunk, chunk), 1)
    for h in range(SGU_HEADS):
        cs = slice(h * HEAD, (h + 1) * HEAD)
        wm = jnp.where(c <= r, w_ref[h][:chunk, :chunk], 0.0).astype(BF16)
        bias = bt_ref[:chunk, h:h + 1]
        for k in range(tr // chunk):
            rs = slice(k * chunk, (k + 1) * chunk)
            vg = _gelu(v_ref[rs, cs].astype(F32))
            xc = vg - jnp.mean(vg, axis=-1, keepdims=True)
            var = jnp.mean(xc * xc, axis=-1, keepdims=True)
            vh = xc * lax.rsqrt(var + 1e-5) * lg_ref[:, cs] + lb_ref[:, cs]
            if vr_refs:
                vr_refs[0][rs, cs] = vh
            s = _dot(wm, vh.astype(BF16)) + bias
            out = _gelu(u_ref[rs, cs].astype(F32)) * s
            o_ref[rs, cs] = _rms(out, g_ref[:, cs]).astype(o_ref.dtype)


def _sgu_mixer(proj, sgu_w, sgu_bt, ln_g, ln_b, g_mix, nrows, row0, chunk, tr, want_v):
    rb0 = row0 // tr
    full = lambda shape: pl.BlockSpec(shape, lambda i: (0,) * len(shape))
    out_specs = [pl.BlockSpec((tr, SGU_DIM), lambda i: (i, 0))]
    out_shape = [jax.ShapeDtypeStruct((nrows, SGU_DIM), BF16)]
    if want_v:
        out_specs.append(pl.BlockSpec((tr, SGU_DIM), lambda i: (i, 0)))
        out_shape.append(jax.ShapeDtypeStruct((nrows, SGU_DIM), F32))
    return pl.pallas_call(
        functools.partial(_sgu_kernel, chunk=chunk),
        grid=(nrows // tr,),
        in_specs=[pl.BlockSpec((tr, SGU_DIM), lambda i: (rb0 + i, COL_U // SGU_DIM)),
                  pl.BlockSpec((tr, SGU_DIM), lambda i: (rb0 + i, COL_V // SGU_DIM)),
                  full((SGU_HEADS, MLP_CHUNK, MLP_CHUNK)),
                  full((MLP_CHUNK, SGU_HEADS)),
                  full((1, SGU_DIM)), full((1, SGU_DIM)),
                  pl.BlockSpec((1, SGU_DIM), lambda i: (0, 1))],
        out_specs=out_specs,
        out_shape=out_shape,
        compiler_params=_params(1),
        name="sgu_mixer",
    )(proj, proj, sgu_w, sgu_bt, ln_g, ln_b, g_mix)


def _rope_q(qm, qr, cos, sin, h):
    nope = qm[:, h * QK_PAD:h * QK_PAD + QK_NOPE]
    rope = (qm[:, h * QK_PAD + QK_NOPE:(h + 1) * QK_PAD] * cos
            + qr[:, h * LANE:(h + 1) * LANE] * sin)
    return nope * SM_SCALE, rope * SM_SCALE


def _rope_k(kr_ref, cos, sin):
    blk = kr_ref[...].astype(F32)
    return blk * cos + pltpu.roll(blk, QK_ROPE, 1) * sin


def _lora_inputs(refs):
    cq = jnp.concatenate([r[...] for r in refs[:N_CQ_BLK]], axis=1).astype(F32)
    ckv = jnp.concatenate([r[...] for r in refs[N_CQ_BLK:N_CQ_BLK + N_CKV_BLK]], axis=1
                          ).astype(F32)
    return cq, ckv, refs[N_CQ_BLK + N_CKV_BLK:]


def _lora_specs(tm, rb0):
    def blk(c0, k):
        return pl.BlockSpec((tm, LORA_BLK), lambda i: (rb0 + i, c0 // LORA_BLK + k))

    return ([blk(COL_CQ, k) for k in range(N_CQ_BLK)]
            + [blk(COL_CKV, k) for k in range(N_CKV_BLK)]
            + [pl.BlockSpec((tm, LANE), lambda i: (rb0 + i, COL_KR // LANE))])


def _mla_pre_prompt_kernel(*refs):
    cq, ckv_in, refs = _lora_inputs(refs)
    (kr_ref, qg_ref, kvg_ref, wqm_ref, wqr_ref, wuk_ref, wuv_ref, cos_ref, sin_ref,
     q_ref, k_ref, v_ref, ckv_out_ref, kr_out_ref) = refs
    cos, sin = cos_ref[...], sin_ref[...]
    cqn = _rms(cq, qg_ref[...]).astype(BF16)
    qm = _dot(cqn, wqm_ref[...])
    qr = _dot(cqn, wqr_ref[...])
    ckv = _rms(ckv_in, kvg_ref[...])
    ckv_out_ref[...] = ckv
    ckb = ckv.astype(BF16)
    kn = _dot(ckb, wuk_ref[...])
    v_ref[...] = _dot(ckb, wuv_ref[...]).astype(v_ref.dtype)
    krp = _rope_k(kr_ref, cos, sin)
    kr_out_ref[...] = krp[:, :QK_ROPE]
    krb = krp.astype(BF16)
    for h in range(MLA_HEADS):
        nope, rope = _rope_q(qm, qr, cos, sin, h)
        q_ref[:, h * QK_PAD:h * QK_PAD + QK_NOPE] = nope.astype(BF16)
        q_ref[:, h * QK_PAD + QK_NOPE:(h + 1) * QK_PAD] = rope.astype(BF16)
        k_ref[:, h * QK_PAD:h * QK_PAD + QK_NOPE] = kn[:, h * LANE:(h + 1) * LANE].astype(BF16)
        k_ref[:, h * QK_PAD + QK_NOPE:(h + 1) * QK_PAD] = krb


def _mla_pre_prompt(proj, lw, cos, sin, mp, t):
    tm = _tile(t, 512)
    nt = t // tm
    full = lambda shape: pl.BlockSpec(shape, lambda i: (0,) * len(shape))
    hq, hv = MLA_HEADS * QK_PAD, MLA_HEADS * V_HEAD
    return pl.pallas_call(
        _mla_pre_prompt_kernel,
        grid=(mp // tm,),
        in_specs=_lora_specs(tm, 0) + [
                  full((1, Q_LORA)), full((1, KV_LORA)),
                  full((Q_LORA, hq)), full((Q_LORA, MLA_HEADS * LANE)),
                  full((KV_LORA, hv)), full((KV_LORA, hv)),
                  pl.BlockSpec((tm, LANE), lambda i: (i % nt, 0)),
                  pl.BlockSpec((tm, LANE), lambda i: (i % nt, 0))],
        out_specs=[pl.BlockSpec((tm, hq), lambda i: (i, 0)),
                   pl.BlockSpec((tm, hq), lambda i: (i, 0)),
                   pl.BlockSpec((tm, hv), lambda i: (i, 0)),
                   pl.BlockSpec((tm, KV_LORA), lambda i: (i, 0)),
                   pl.BlockSpec((tm, QK_ROPE), lambda i: (i, 0))],
        out_shape=[jax.ShapeDtypeStruct((mp, hq), BF16),
                   jax.ShapeDtypeStruct((mp, hq), BF16),
                   jax.ShapeDtypeStruct((mp, hv), BF16),
                   jax.ShapeDtypeStruct((mp, KV_LORA), F32),
                   jax.ShapeDtypeStruct((mp, QK_ROPE), F32)],
        compiler_params=_params(1),
        name="mla_pre_prompt",
    )(*[proj] * (N_CQ_BLK + N_CKV_BLK + 1), lw["q_norm_g"], lw["kv_norm_g"], lw["wq_main"], lw["wq_rot"],
      lw["w_uk"], lw["w_uv"], cos, sin)


def _mla_pre_sample_kernel(*refs):
    cq, ckv_in, refs = _lora_inputs(refs)
    (kr_ref, qg_ref, kvg_ref, wqm_ref, wqr_ref, wukt_ref, cos_ref, sin_ref,
     ql_ref, qr_ref, ckv_out_ref, kr_out_ref) = refs
    cos, sin = cos_ref[...], sin_ref[...]
    cqn = _rms(cq, qg_ref[...]).astype(BF16)
    qm = _dot(cqn, wqm_ref[...])
    qr = _dot(cqn, wqr_ref[...])
    ckv_out_ref[...] = _rms(ckv_in, kvg_ref[...])
    kr_out_ref[...] = _rope_k(kr_ref, cos, sin)[:, :QK_ROPE]
    for h in range(MLA_HEADS):
        nope, rope = _rope_q(qm, qr, cos, sin, h)
        ql_ref[h] = _dot(nope.astype(BF16), wukt_ref[h]).astype(BF16)
        qr_ref[h] = rope[:, :QK_ROPE].astype(BF16)


def _mla_pre_sample(proj, lw, cos, sin, mp, ms, tm):
    rb0 = mp // tm
    full = lambda shape: pl.BlockSpec(shape, lambda i: (0,) * len(shape))
    return pl.pallas_call(
        _mla_pre_sample_kernel,
        grid=(ms // tm,),
        in_specs=_lora_specs(tm, rb0) + [
                  full((1, Q_LORA)), full((1, KV_LORA)),
                  full((Q_LORA, MLA_HEADS * QK_PAD)), full((Q_LORA, MLA_HEADS * LANE)),
                  full((MLA_HEADS, QK_NOPE, KV_LORA)),
                  full((tm, LANE)), full((tm, LANE))],
        out_specs=[pl.BlockSpec((MLA_HEADS, tm, KV_LORA), lambda i: (0, i, 0)),
                   pl.BlockSpec((MLA_HEADS, tm, QK_ROPE), lambda i: (0, i, 0)),
                   pl.BlockSpec((tm, KV_LORA), lambda i: (i, 0)),
                   pl.BlockSpec((tm, QK_ROPE), lambda i: (i, 0))],
        out_shape=[jax.ShapeDtypeStruct((MLA_HEADS, ms, KV_LORA), BF16),
                   jax.ShapeDtypeStruct((MLA_HEADS, ms, QK_ROPE), BF16),
                   jax.ShapeDtypeStruct((ms, KV_LORA), F32),
                   jax.ShapeDtypeStruct((ms, QK_ROPE), F32)],
        compiler_params=_params(1),
        name="mla_pre_sample",
    )(*[proj] * (N_CQ_BLK + N_CKV_BLK + 1), lw["q_norm_g"], lw["kv_norm_g"], lw["wq_main"], lw["wq_rot"],
      lw["w_ukt"], cos, sin)


def _attn_prompt_kernel(q_ref, k_ref, v_ref, g_ref, o_ref, *, tq):
    t = q_ref.shape[0]
    rc = lax.broadcasted_iota(I32, (tq, tq), 0) // CHUNK
    cc = lax.broadcasted_iota(I32, (tq, tq), 1) // CHUNK
    visible = cc <= rc
    g = g_ref[...]
    for i in range(t // tq):
        lo, hi = i * tq, (i + 1) * tq
        q = q_ref[lo:hi, :]
        sd = jnp.where(visible, _dot_nt(q, k_ref[lo:hi, :]), -jnp.inf)
        m = jnp.max(sd, axis=-1, keepdims=True)
        if i:
            so = _dot_nt(q, k_ref[0:lo, :])
            m = jnp.maximum(m, jnp.max(so, axis=-1, keepdims=True))
        pd = jnp.exp(sd - m)
        l = jnp.sum(pd, axis=-1, keepdims=True)
        o = _dot(pd.astype(BF16), v_ref[lo:hi, :])
        if i:
            po = jnp.exp(so - m)
            l = l + jnp.sum(po, axis=-1, keepdims=True)
            o = o + _dot(po.astype(BF16), v_ref[0:lo, :])
        o_ref[lo:hi, :] = _rms(o / l, g).astype(o_ref.dtype)


def _attn_prompt(q, k, v, g_mix, nb, t):
    tq = _tile(t, 256)
    c0 = (CONV_DIM + SGU_DIM) // HEAD
    return pl.pallas_call(
        functools.partial(_attn_prompt_kernel, tq=tq),
        grid=(nb, MLA_HEADS),
        in_specs=[pl.BlockSpec((t, QK_PAD), lambda b, h: (b, h)),
                  pl.BlockSpec((t, QK_PAD), lambda b, h: (b, h)),
                  pl.BlockSpec((t, V_HEAD), lambda b, h: (b, h)),
                  pl.BlockSpec((1, HEAD), lambda b, h: (0, c0 + h))],
        out_specs=pl.BlockSpec((t, HEAD), lambda b, h: (b, h)),
        out_shape=jax.ShapeDtypeStruct((nb * t, MLA_HEADS * HEAD), BF16),
        compiler_params=_params(2),
        name="attn_prompt",
    )(q, k, v, g_mix)


def _attn_sample_kernel(ql_ref, qr_ref, cc_ref, ck_ref, nc_ref, nk_ref, wuv_ref, g_ref, o_ref):
    s_len = ql_ref.shape[1]
    ql = ql_ref[...].reshape(MLA_HEADS * s_len, KV_LORA)
    qr = qr_ref[...].reshape(MLA_HEADS * s_len, QK_ROPE)
    cc = cc_ref[0, 0].astype(BF16)
    ck = ck_ref[0, 0].astype(BF16)
    nc = nc_ref[...].astype(BF16)
    nk = nk_ref[...].astype(BF16)
    sp = _dot_nt(ql, cc) + _dot_nt(qr, ck)
    sn = _dot_nt(ql, nc) + _dot_nt(qr, nk)
    m = jnp.maximum(jnp.max(sp, axis=-1, keepdims=True), jnp.max(sn, axis=-1, keepdims=True))
    pp, pn = jnp.exp(sp - m), jnp.exp(sn - m)
    l = jnp.sum(pp, axis=-1, keepdims=True) + jnp.sum(pn, axis=-1, keepdims=True)
    o_lat = ((_dot(pp.astype(BF16), cc) + _dot(pn.astype(BF16), nc)) / l).astype(BF16)
    for h in range(MLA_HEADS):
        oh = _dot(o_lat[h * s_len:(h + 1) * s_len], wuv_ref[h])
        o_ref[:, h * HEAD:(h + 1) * HEAD] = _rms(oh, g_ref[:, h * HEAD:(h + 1) * HEAD]
                                                 ).astype(o_ref.dtype)


def _attn_sample(q_lat, q_rope, cache_ckv, cache_krope, ckv_new, kr_new, wuv_h, g_mix,
                 layer, nb, s_len):
    past = cache_ckv.shape[2]
    full = lambda shape: pl.BlockSpec(shape, lambda b: (0,) * len(shape))
    return pl.pallas_call(
        _attn_sample_kernel,
        grid=(nb,),
        in_specs=[pl.BlockSpec((MLA_HEADS, s_len, KV_LORA), lambda b: (0, b, 0)),
                  pl.BlockSpec((MLA_HEADS, s_len, QK_ROPE), lambda b: (0, b, 0)),
                  pl.BlockSpec((1, 1, past, KV_LORA), lambda b: (layer, b, 0, 0)),
                  pl.BlockSpec((1, 1, past, QK_ROPE), lambda b: (layer, b, 0, 0)),
                  pl.BlockSpec((s_len, KV_LORA), lambda b: (b, 0)),
                  pl.BlockSpec((s_len, QK_ROPE), lambda b: (b, 0)),
                  full((MLA_HEADS, KV_LORA, V_HEAD)),
                  pl.BlockSpec((1, MLA_HEADS * HEAD), lambda b: (0, 1))],
        out_specs=pl.BlockSpec((s_len, MLA_HEADS * HEAD), lambda b: (b, 0)),
        out_shape=jax.ShapeDtypeStruct((nb * s_len, MLA_HEADS * HEAD), BF16),
        compiler_params=_params(1),
        name="attn_sample",
    )(q_lat, q_rope, cache_ckv, cache_krope, ckv_new, kr_new, wuv_h, g_mix)


def _out_proj_kernel(ap_ref, bp_ref, cp_ref, as_ref, bs_ref, cs_ref, w_ref, xp_ref, xs_ref,
                     g_ref, b_ref, op_ref, os_ref, *rest, n_p, packed):
    mx_ref = rest[-1]
    i = pl.program_id(0)
    c1, c2 = CONV_DIM, CONV_DIM + SGU_DIM

    @pl.when(i < n_p)
    def _():
        mx_ref[:, :c1] = ap_ref[...]
        mx_ref[:, c1:c2] = bp_ref[...]
        mx_ref[:, c2:] = cp_ref[...]

    @pl.when(i >= n_p)
    def _():
        mx_ref[:, :c1] = as_ref[...]
        mx_ref[:, c1:c2] = bs_ref[...]
        mx_ref[:, c2:] = cs_ref[...]

    y = ALPHA * _load2(xp_ref, xs_ref, i, n_p) + _dot(mx_ref[...], w_ref[...])
    out = _layernorm(y, g_ref[...], b_ref[...])
    _store2(op_ref, os_ref, i, n_p, out)
    if packed:
        rest[0][...] = _pack_halves(out)
    else:
        _store2(rest[0], rest[1], i, n_p, out)


def _out_proj(mix_p, mix_s, w_o, x_p, x_s, g, b, packed):
    (mp, d), ms = x_p.shape, x_s.shape[0]
    tm = _tile(ms, 256)
    n_p = mp // tm
    m = mp + ms
    full = lambda shape: pl.BlockSpec(shape, lambda i: (0,) * len(shape))
    in_specs = []
    for arrs, clamp in ((mix_p, lambda i: (jnp.minimum(i, n_p - 1), 0)),
                        (mix_s, lambda i: (jnp.maximum(i - n_p, 0), 0))):
        in_specs += [pl.BlockSpec((tm, a.shape[1]), clamp) for a in arrs]
    xp_spec, xs_spec = _two_stream_specs(tm, d, n_p)
    if packed:
        low_specs = [pl.BlockSpec((tm, d // 2), lambda i: (i, 0))]
        low_shapes = [jax.ShapeDtypeStruct((m, d // 2), I32)]
    else:
        low_specs = [xp_spec, xs_spec]
        low_shapes = [jax.ShapeDtypeStruct((mp, d), BF16), jax.ShapeDtypeStruct((ms, d), BF16)]
    return pl.pallas_call(
        functools.partial(_out_proj_kernel, n_p=n_p, packed=packed),
        grid=(m // tm,),
        in_specs=in_specs + [full((d, d)), xp_spec, xs_spec, full((1, d)), full((1, d))],
        out_specs=[xp_spec, xs_spec] + low_specs,
        out_shape=[jax.ShapeDtypeStruct((mp, d), F32), jax.ShapeDtypeStruct((ms, d), F32)]
                  + low_shapes,
        scratch_shapes=[pltpu.VMEM((tm, d), BF16)],
        compiler_params=_params(1),
        name="out_proj_ln",
    )(*mix_p, *mix_s, w_o, x_p, x_s, g, b)


def _swiglu(xb, wg, wu):
    a = _dot(xb, wg)
    return a * (1.0 / (1.0 + jnp.exp(-a))) * _dot(xb, wu)


def _ffn_kernel(xb_ref, wg_ref, wu_ref, wd_ref, x_ref, g_ref, b_ref, o_ref, ob_ref, acc_ref):
    f = pl.program_id(1)

    @pl.when(f == 0)
    def _():
        acc_ref[...] = jnp.zeros_like(acc_ref)

    h = _swiglu(xb_ref[...], wg_ref[...], wu_ref[...])
    acc_ref[...] += _dot(h.astype(BF16), wd_ref[...])

    @pl.when(f == pl.num_programs(1) - 1)
    def _():
        out = _layernorm(ALPHA * x_ref[...] + acc_ref[...], g_ref[...], b_ref[...])
        o_ref[...] = out
        ob_ref[...] = out.astype(BF16)


def _ffn(xb, x, wg, wu, wd, g, b):
    m, d = x.shape
    ff = wg.shape[1]
    tm, tf = _tile(m, 512), _tile(ff, 512)
    row = lambda i, f: (i, 0)
    vec = pl.BlockSpec((1, d), lambda i, f: (0, 0))
    return pl.pallas_call(
        _ffn_kernel,
        grid=(m // tm, ff // tf),
        in_specs=[pl.BlockSpec((tm, d), row),
                  pl.BlockSpec((d, tf), lambda i, f: (0, f)),
                  pl.BlockSpec((d, tf), lambda i, f: (0, f)),
                  pl.BlockSpec((tf, d), lambda i, f: (f, 0)),
                  pl.BlockSpec((tm, d), row), vec, vec],
        out_specs=[pl.BlockSpec((tm, d), row), pl.BlockSpec((tm, d), row)],
        out_shape=[jax.ShapeDtypeStruct((m, d), F32), jax.ShapeDtypeStruct((m, d), BF16)],
        scratch_shapes=[pltpu.VMEM((tm, d), F32)],
        compiler_params=_params(2),
        name="ffn_dense",
    )(xb, wg, wu, wd, x, g, b)


EXPERT_ROWS = 16
ROW_TILE = 512


def _router_kernel(xp_ref, xs_ref, wh_ref, wl_ref, gate_ref, rank_ref, cnt_ref, carry_ref, *, n_p):
    i = pl.program_id(0)

    @pl.when(i == 0)
    def _():
        carry_ref[...] = jnp.zeros_like(carry_ref)

    x = _load2(xp_ref, xs_ref, i, n_p)
    ts = x.shape[0]
    xh = x.astype(BF16)
    xl = (x - xh.astype(F32)).astype(BF16)
    wh, wl = wh_ref[...], wl_ref[...]
    logits = _dot_nt(wh, xh) + (_dot_nt(wh, xl) + _dot_nt(wl, xh))
    row = lax.broadcasted_iota(I32, logits.shape, 0)
    logits = jnp.where(row < N_EXPERTS, logits, -jnp.inf)
    m1 = jnp.max(logits, axis=0, keepdims=True)
    i1 = jnp.min(jnp.where(logits == m1, row, EXPERT_ROWS), axis=0, keepdims=True)
    rest = jnp.where(row == i1, -jnp.inf, logits)
    m2 = jnp.max(rest, axis=0, keepdims=True)
    i2 = jnp.min(jnp.where(rest == m2, row, EXPERT_ROWS), axis=0, keepdims=True)
    e2 = jnp.exp(m2 - m1)
    g1 = 1.0 / (1.0 + e2)
    g2 = e2 / (1.0 + e2)
    gate_ref[...] = jnp.where(row == i1, g1, jnp.where(row == i2, g2, 0.0))
    sel = jnp.where(row == i1, 1.0, jnp.where(row == i2, 1.0, 0.0))
    src = lax.broadcasted_iota(I32, (ts, ts), 0)
    dst = lax.broadcasted_iota(I32, (ts, ts), 1)
    incl = _dot(sel.astype(BF16), jnp.where(src <= dst, 1.0, 0.0).astype(BF16))
    carry = carry_ref[:, 0:1]
    rank_ref[...] = jnp.where(sel > 0.0, carry + incl - sel, -1.0).astype(I32)
    carry = carry + jnp.sum(sel, axis=1, keepdims=True)
    carry_ref[...] = jnp.broadcast_to(carry, carry_ref.shape)
    cnt_ref[...] = jnp.broadcast_to(carry, carry_ref.shape).astype(I32)


def _router(x_p, x_s, wh, wl):
    (mp, d), ms = x_p.shape, x_s.shape[0]
    m = mp + ms
    ts = _tile(ms, 512)
    n_p = mp // ts
    full = pl.BlockSpec((EXPERT_ROWS, d), lambda i: (0, 0))
    col = pl.BlockSpec((EXPERT_ROWS, ts), lambda i: (0, i))
    xp_spec, xs_spec = _two_stream_specs(ts, d, n_p)
    return pl.pallas_call(
        functools.partial(_router_kernel, n_p=n_p),
        grid=(m // ts,),
        in_specs=[xp_spec, xs_spec, full, full],
        out_specs=[col, col, pl.BlockSpec((EXPERT_ROWS, LANE), lambda i: (0, 0))],
        out_shape=[jax.ShapeDtypeStruct((EXPERT_ROWS, m), F32),
                   jax.ShapeDtypeStruct((EXPERT_ROWS, m), I32),
                   jax.ShapeDtypeStruct((EXPERT_ROWS, LANE), I32)],
        scratch_shapes=[pltpu.VMEM((EXPERT_ROWS, LANE), F32)],
        compiler_params=_params(1),
        name="router",
    )(x_p, x_s, wh, wl)


def _route_plan(gate_t, rank_t, cnt, m, td):
    ne = N_EXPERTS
    total = cnt[:ne, 0]
    padded = (total + td - 1) // td * td
    ends = jnp.cumsum(padded)
    off = ends - padded
    n_act = (ends[-1] // td).astype(I32)
    rt = -(-(2 * m + ne * (td - 1)) // td)
    tiles = jnp.minimum(jnp.arange(rt, dtype=I32), n_act - 1)
    tile_expert = jnp.sum(tiles[:, None] >= (ends // td)[None, :], axis=1).astype(I32)

    rank, gate = rank_t[:ne], gate_t[:ne]
    chosen = rank >= 0
    pos = off[:, None] + rank
    pos_a = jnp.min(jnp.where(chosen, pos, rt * td), axis=0).astype(I32)
    pos_b = jnp.max(jnp.where(chosen, pos, -1), axis=0).astype(I32)
    gate_a = jnp.sum(jnp.where(chosen & (pos == pos_a[None]), gate, 0.0), axis=0)
    gate_b = jnp.sum(jnp.where(chosen & (pos == pos_b[None]), gate, 0.0), axis=0)
    return pos_a, pos_b, gate_a[:, None], gate_b[:, None], tile_expert, n_act.reshape(1), rt


def _sc_mesh():
    return plsc.VectorSubcoreMesh(core_axis_name="c", subcore_axis_name="s")


def _sc_worker_base(per_worker):
    wid = lax.axis_index("s") * SC_CORES + lax.axis_index("c")
    return wid * per_worker


def _sc_scatter_rows(x, idx_a, idx_b, n_rows):
    m, w = x.shape
    per_worker = m // (SC_CORES * SC_SUBCORES)
    assert per_worker * SC_CORES * SC_SUBCORES == m and per_worker % SC_ROWS == 0, m

    @functools.partial(
        pl.kernel, mesh=_sc_mesh(),
        out_type=jax.ShapeDtypeStruct((n_rows, w), x.dtype),
        scratch_types=[pltpu.VMEM((SC_ROWS,), I32), pltpu.VMEM((SC_ROWS,), I32),
                       pltpu.VMEM((SC_ROWS, w), x.dtype),
                       pltpu.SemaphoreType.DMA, pltpu.SemaphoreType.DMA])
    def scatter(x_hbm, ia_hbm, ib_hbm, out_hbm, ia_v, ib_v, rows_v, sem_a, sem_b):
        base = _sc_worker_base(per_worker)

        @pl.loop(0, per_worker // SC_ROWS)
        def _(c):
            lo = pl.multiple_of(base + c * SC_ROWS, 8)
            pltpu.sync_copy(ia_hbm.at[pl.ds(lo, SC_ROWS)], ia_v)
            pltpu.sync_copy(ib_hbm.at[pl.ds(lo, SC_ROWS)], ib_v)
            pltpu.sync_copy(x_hbm.at[pl.ds(lo, SC_ROWS)], rows_v)
            put_a = pltpu.async_copy(rows_v, out_hbm.at[ia_v], sem_a)
            put_b = pltpu.async_copy(rows_v, out_hbm.at[ib_v], sem_b)
            put_a.wait()
            put_b.wait()

    return scatter(x, idx_a, idx_b)


def _sc_gather_rows(table, idx):
    m, w = idx.shape[0], table.shape[1]
    per_worker = m // (SC_CORES * SC_SUBCORES)
    assert per_worker * SC_CORES * SC_SUBCORES == m and per_worker % SC_ROWS == 0, m

    @functools.partial(
        pl.kernel, mesh=_sc_mesh(),
        out_type=jax.ShapeDtypeStruct((m, w), table.dtype),
        scratch_types=[pltpu.VMEM((SC_ROWS,), I32), pltpu.VMEM((SC_ROWS, w), table.dtype),
                       pltpu.SemaphoreType.DMA])
    def gather(table_hbm, idx_hbm, out_hbm, idx_v, rows_v, sem):
        base = _sc_worker_base(per_worker)

        @pl.loop(0, per_worker // SC_ROWS)
        def _(c):
            lo = pl.multiple_of(base + c * SC_ROWS, 8)
            pltpu.sync_copy(idx_hbm.at[pl.ds(lo, SC_ROWS)], idx_v)
            pltpu.async_copy(table_hbm.at[idx_v], rows_v, sem).wait()
            pltpu.sync_copy(rows_v, out_hbm.at[pl.ds(lo, SC_ROWS)])

    return gather(table, idx)


def _experts_kernel(te_ref, na_ref, xs_ref, wg_ref, wu_ref, wd_ref, o_ref, xb_ref, acc_ref):
    del te_ref
    j, f = pl.program_id(0), pl.program_id(1)
    half = xs_ref.shape[1]

    @pl.when(j < na_ref[0])
    def _():
        @pl.when(f == 0)
        def _():
            lo, hi = _unpack_halves(xs_ref[...])
            xb_ref[:, :half] = lo.astype(BF16)
            xb_ref[:, half:] = hi.astype(BF16)
            acc_ref[...] = jnp.zeros_like(acc_ref)

        h = _swiglu(xb_ref[...], wg_ref[0], wu_ref[0])
        acc_ref[...] += _dot(h.astype(BF16), wd_ref[0])

        @pl.when(f == pl.num_programs(1) - 1)
        def _():
            o_ref[...] = _pack_halves(acc_ref[...])


def _experts(tile_expert, n_act, xs, wg, wu, wd, td):
    rows, half = xs.shape
    d = 2 * half
    ff = wg.shape[2]
    tf = _tile(ff, 512)
    nf = ff // tf
    row = lambda j, f, te, na: (jnp.minimum(j, na[0] - 1), 0)
    fcol = lambda j, f, na: jnp.where(j < na[0], f, nf - 1)
    return pl.pallas_call(
        _experts_kernel,
        grid_spec=pltpu.PrefetchScalarGridSpec(
            num_scalar_prefetch=2,
            grid=(rows // td, nf),
            in_specs=[pl.BlockSpec((td, half), row),
                      pl.BlockSpec((1, d, tf), lambda j, f, te, na: (te[j], 0, fcol(j, f, na))),
                      pl.BlockSpec((1, d, tf), lambda j, f, te, na: (te[j], 0, fcol(j, f, na))),
                      pl.BlockSpec((1, tf, d), lambda j, f, te, na: (te[j], fcol(j, f, na), 0))],
            out_specs=pl.BlockSpec((td, half), row),
            scratch_shapes=[pltpu.VMEM((td, d), BF16), pltpu.VMEM((td, d), F32)]),
        out_shape=jax.ShapeDtypeStruct((rows, half), I32),
        compiler_params=_params(2),
        name="moe_experts",
    )(tile_expert, n_act, xs, wg, wu, wd)


def _moe_finish_kernel(oa_ref, ob_ref, ga_ref, gb_ref, xp_ref, xs_ref, g_ref, b_ref,
                       op_ref, os_ref, *, n_p):
    i = pl.program_id(0)
    a_lo, a_hi = _unpack_halves(oa_ref[...])
    b_lo, b_hi = _unpack_halves(ob_ref[...])
    ga, gb = ga_ref[...], gb_ref[...]
    y = jnp.concatenate([ga * a_lo + gb * b_lo, ga * a_hi + gb * b_hi], axis=1)
    out = _layernorm(ALPHA * _load2(xp_ref, xs_ref, i, n_p) + y, g_ref[...], b_ref[...])
    _store2(op_ref, os_ref, i, n_p, out)


def _moe_finish(o_a, o_b, gate_a, gate_b, x_p, x_s, g, b):
    (mp, d), ms = x_p.shape, x_s.shape[0]
    m = mp + ms
    tm = _tile(ms, 512)
    n_p = mp // tm
    row = lambda i: (i, 0)
    vec = pl.BlockSpec((1, d), lambda i: (0, 0))
    xp_spec, xs_spec = _two_stream_specs(tm, d, n_p)
    return pl.pallas_call(
        functools.partial(_moe_finish_kernel, n_p=n_p),
        grid=(m // tm,),
        in_specs=[pl.BlockSpec((tm, d // 2), row), pl.BlockSpec((tm, d // 2), row),
                  pl.BlockSpec((tm, 1), row), pl.BlockSpec((tm, 1), row),
                  xp_spec, xs_spec, vec, vec],
        out_specs=[xp_spec, xs_spec],
        out_shape=[jax.ShapeDtypeStruct((mp, d), F32), jax.ShapeDtypeStruct((ms, d), F32)],
        compiler_params=_params(1),
        name="moe_finish",
    )(o_a, o_b, gate_a, gate_b, x_p, x_s, g, b)


def _moe(x_p, x_s, x_packed, router_w, wg, wu, wd, g, b):
    m = x_p.shape[0] + x_s.shape[0]
    rw = jnp.pad(jnp.transpose(router_w), ((0, EXPERT_ROWS - N_EXPERTS), (0, 0)))
    rwh = rw.astype(BF16)
    rwl = (rw - rwh.astype(F32)).astype(BF16)
    gate_t, rank_t, cnt = _router(x_p, x_s, rwh, rwl)
    td = min(ROW_TILE, _tile(m, ROW_TILE))
    pos_a, pos_b, gate_a, gate_b, tile_expert, n_act, rt = _route_plan(gate_t, rank_t, cnt, m, td)
    xs = _sc_scatter_rows(x_packed, pos_a, pos_b, rt * td)
    o_sorted = _experts(tile_expert, n_act, xs, wg, wu, wd, td)
    o_a = _sc_gather_rows(o_sorted, pos_a)
    o_b = _sc_gather_rows(o_sorted, pos_b)
    return _moe_finish(o_a, o_b, gate_a, gate_b, x_p, x_s, g, b)


def _rot_cols(w):
    half = QK_ROPE // 2
    return jnp.concatenate([-w[..., half:], w[..., :half]], axis=-1)


def _layer_weights(l, w_in, conv_w, sgu_ln_g, sgu_ln_b, sgu_w, sgu_b, q_norm_g, w_uq, kv_norm_g,
                   w_uk, w_uv, mix_norm_g, w_o):
    d = w_in.shape[1]
    wi = w_in[l]
    k_r = wi[:, COL_KR:COL_KR + QK_ROPE]
    w_in_pad = jnp.concatenate(
        [wi.astype(BF16), _rot_cols(k_r).astype(BF16),
         jnp.zeros((d, D_IN_PAD - COL_KR - 2 * QK_ROPE), BF16)], axis=1)
    uq = w_uq[l]
    zeros = jnp.zeros((Q_LORA, MLA_HEADS, QK_PAD - QK_NOPE - QK_ROPE), F32)
    wq_main = jnp.concatenate([uq, zeros], axis=-1).reshape(Q_LORA, MLA_HEADS * QK_PAD)
    wq_rot = jnp.concatenate([_rot_cols(uq[..., QK_NOPE:]), zeros], axis=-1
                             ).reshape(Q_LORA, MLA_HEADS * LANE)
    return {
        "w_in": w_in_pad,
        "conv_w": conv_w[l],
        "sgu_w": sgu_w[l],
        "sgu_bt": jnp.transpose(sgu_b[l]),
        "sgu_ln_g": sgu_ln_g[l][None], "sgu_ln_b": sgu_ln_b[l][None],
        "q_norm_g": q_norm_g[l][None], "kv_norm_g": kv_norm_g[l][None],
        "wq_main": wq_main.astype(BF16), "wq_rot": wq_rot.astype(BF16),
        "w_uk": w_uk[l].reshape(KV_LORA, MLA_HEADS * QK_NOPE).astype(BF16),
        "w_uv": w_uv[l].reshape(KV_LORA, MLA_HEADS * V_HEAD).astype(BF16),
        "w_ukt": jnp.transpose(w_uk[l], (1, 2, 0)).astype(BF16),
        "w_uv_h": jnp.transpose(w_uv[l], (1, 0, 2)).astype(BF16),
        "g_mix": mix_norm_g[l][None],
        "w_o": w_o[l].astype(BF16),
    }


def _rope_tables(pos):
    half = QK_ROPE // 2
    inv = ROPE_THETA ** (-jnp.arange(half, dtype=F32) / half)
    ang = pos.astype(F32)[:, None] * inv[None, :]
    zeros = jnp.zeros((pos.shape[0], LANE - QK_ROPE), F32)
    cos = jnp.concatenate([jnp.cos(ang), jnp.cos(ang), zeros], axis=-1)
    sin = jnp.concatenate([jnp.sin(ang), jnp.sin(ang), zeros], axis=-1)
    return cos, sin


def kernel(x_prompt, x_sample, state_conv, cache_ckv, cache_krope, w_in, conv_w, sgu_ln_g,
           sgu_ln_b, sgu_w, sgu_b, q_norm_g, w_uq, kv_norm_g, w_uk, w_uv, mix_norm_g, w_o,
           ln1_g, ln1_b, ln2_g, ln2_b, ffn_w_gate, ffn_w_up, ffn_w_down, router_w,
           moe_w_gate, moe_w_up, moe_w_down):
    nbp, t, d = x_prompt.shape
    nbs, s_len, _ = x_sample.shape
    past = cache_ckv.shape[2]
    depth = w_in.shape[0]
    mp, ms = nbp * t, nbs * s_len

    x_p, x_s = x_prompt.reshape(mp, d), x_sample.reshape(ms, d)
    xb_p, xb_s = x_p, x_s

    cos_p, sin_p = _rope_tables(jnp.arange(t))
    tms = _tile(ms, 256)
    cos_s, sin_s = _rope_tables(past + jnp.arange(s_len))
    cos_s, sin_s = jnp.tile(cos_s, (tms // s_len, 1)), jnp.tile(sin_s, (tms // s_len, 1))
    zero_state = jnp.zeros((nbp, CONV_WIDTH - 1, CONV_DIM), F32)

    outs = [[] for _ in range(7)]
    for l in range(depth):
        lw = _layer_weights(l, w_in, conv_w, sgu_ln_g, sgu_ln_b, sgu_w, sgu_b, q_norm_g, w_uq,
                            kv_norm_g, w_uk, w_uv, mix_norm_g, w_o)
        routed = l % 2 == 1
        proj = _in_proj(xb_p, xb_s, lw["w_in"])

        a_p, conv_p = _conv_mixer(proj, zero_state, lw["conv_w"], lw["g_mix"], nbp, t, 0)
        a_s, conv_s = _conv_mixer(proj, state_conv[l], lw["conv_w"], lw["g_mix"], nbs, s_len, mp)
        sgu_args = (lw["sgu_w"], lw["sgu_bt"], lw["sgu_ln_g"], lw["sgu_ln_b"], lw["g_mix"])
        chunk_p = min(MLP_CHUNK, t)
        (b_p,) = _sgu_mixer(proj, *sgu_args, mp, 0, chunk_p,
                            _tile(t, 512) if t >= 512 else chunk_p, False)
        chunk_s = min(MLP_CHUNK, s_len)
        b_s, v_rows = _sgu_mixer(proj, *sgu_args, ms, mp, chunk_s, chunk_s, True)
        q, k, v, ckv_p, kr_p = _mla_pre_prompt(proj, lw, cos_p, sin_p, mp, t)
        c_p = _attn_prompt(q, k, v, lw["g_mix"], nbp, t)
        q_lat, q_rope, ckv_s, kr_s = _mla_pre_sample(proj, lw, cos_s, sin_s, mp, ms, tms)
        c_s = _attn_sample(q_lat, q_rope, cache_ckv, cache_krope, ckv_s, kr_s, lw["w_uv_h"],
                           lw["g_mix"], l, nbs, s_len)

        x_p, x_s, *low = _out_proj((a_p, b_p, c_p), (a_s, b_s, c_s), lw["w_o"], x_p, x_s,
                                   ln1_g[l][None], ln1_b[l][None], routed)

        i = l // 2
        if routed:
            x_p, x_s = _moe(x_p, x_s, low[0], router_w[i], moe_w_gate[i].astype(BF16),
                            moe_w_up[i].astype(BF16), moe_w_down[i].astype(BF16),
                            ln2_g[l][None], ln2_b[l][None])
            xb_p, xb_s = x_p, x_s
        else:
            ffn_w = (ffn_w_gate[i].astype(BF16), ffn_w_up[i].astype(BF16),
                     ffn_w_down[i].astype(BF16), ln2_g[l][None], ln2_b[l][None])
            x_p, xb_p = _ffn(low[0], x_p, *ffn_w)
            x_s, xb_s = _ffn(low[1], x_s, *ffn_w)

        for lst, val in zip(outs, (conv_p, ckv_p.reshape(nbp, t, KV_LORA),
                                   kr_p.reshape(nbp, t, QK_ROPE), conv_s,
                                   ckv_s.reshape(nbs, s_len, KV_LORA),
                                   kr_s.reshape(nbs, s_len, QK_ROPE),
                                   v_rows.reshape(nbs, s_len, SGU_DIM))):
            lst.append(val)

    return (x_p.reshape(nbp, t, d), x_s.reshape(nbs, s_len, d), *[jnp.stack(o) for o in outs])
```

```python
import functools

import numpy as np
import jax
import jax.numpy as jnp
from jax import lax
from jax.experimental import pallas as pl
from jax.experimental.pallas import tpu as pltpu
from jax.experimental.pallas import tpu_sc as plsc

F32 = jnp.float32
BF16 = jnp.bfloat16
I32 = jnp.int32

D_MODEL = 2048
CONV_DIM = 512
CONV_WIDTH = 3
SGU_DIM = 512
SGU_HEADS = 4
MLP_CHUNK = 128
MLA_HEADS = 8
QK_NOPE = 128
QK_ROPE = 64
V_HEAD = 128
Q_LORA = 768
KV_LORA = 512
ROPE_THETA = 10000.0
CHUNK = 64
HEAD = 128
N_EXPERTS = 8
DEPTH = 2
SM_SCALE = (QK_NOPE + QK_ROPE) ** -0.5
ALPHA = (2 * DEPTH) ** 0.25

COL_BG, COL_CG, COL_HC, COL_U, COL_V = 0, 512, 1024, 1536, 2048
COL_CQ, COL_CKV, COL_KR = 2560, 3328, 3840
D_IN_PAD = 4096
LORA_BLK = 256
N_CQ_BLK, N_CKV_BLK = Q_LORA // LORA_BLK, KV_LORA // LORA_BLK
QK_PAD = 256

LANE = 128
VMEM_LIMIT = 56 * 1024 * 1024
SC_CORES, SC_SUBCORES = 2, 16
SC_ROWS = 32


def _params(n_axes):
    return pltpu.CompilerParams(dimension_semantics=("arbitrary",) * n_axes,
                                vmem_limit_bytes=VMEM_LIMIT)


def _tile(n, pref):
    t = pref
    while n % t:
        t //= 2
    assert t >= 8, (n, pref)
    return t


def _rms(x, g, eps=1e-6):
    return x * lax.rsqrt(jnp.mean(x * x, axis=-1, keepdims=True) + eps) * g


def _layernorm(y, g, b, eps=1e-5):
    yc = y - jnp.mean(y, axis=-1, keepdims=True)
    var = jnp.mean(yc * yc, axis=-1, keepdims=True)
    return yc * lax.rsqrt(var + eps) * g + b


def _gelu(x):
    c = np.sqrt(2.0 / np.pi).astype(np.float32)
    return 0.5 * x * (1.0 + jnp.tanh(c * (x + 0.044715 * (x * x * x))))


def _dot(a, b):
    return jnp.dot(a, b, preferred_element_type=F32)


def _dot_nt(a, b):
    return lax.dot_general(a, b, (((1,), (1,)), ((), ())), preferred_element_type=F32)


def _two_stream_specs(tm, width, n_p, single_buffer=False):
    def p_map(i, *_):
        return (jnp.minimum(i, n_p - 1), 0)

    def s_map(i, *_):
        return (jnp.maximum(i - n_p, 0), 0)

    mode = {"pipeline_mode": pl.Buffered(1)} if single_buffer else {}
    return (pl.BlockSpec((tm, width), p_map, **mode), pl.BlockSpec((tm, width), s_map, **mode))


def _load2(p_ref, s_ref, i, n_p):
    return jnp.where(i < n_p, p_ref[...], s_ref[...])


def _store2(p_ref, s_ref, i, n_p, val):
    @pl.when(i < n_p)
    def _():
        p_ref[...] = val.astype(p_ref.dtype)

    @pl.when(i >= n_p)
    def _():
        s_ref[...] = val.astype(s_ref.dtype)


def _pack_halves(y):
    w = y.shape[1] // 2
    lo = lax.bitcast_convert_type(y[:, :w].astype(BF16).astype(F32), jnp.uint32)
    hi = lax.bitcast_convert_type(y[:, w:].astype(BF16).astype(F32), jnp.uint32)
    return lax.bitcast_convert_type(hi | (lo >> 16), I32)


def _unpack_halves(p):
    u = lax.bitcast_convert_type(p, jnp.uint32)
    lo = lax.bitcast_convert_type(u << 16, F32)
    hi = lax.bitcast_convert_type(u & jnp.uint32(0xFFFF0000), F32)
    return lo, hi


def _in_proj_kernel(xp_ref, xs_ref, w_ref, o_ref, xb_ref, *, n_p):
    @pl.when(pl.program_id(1) == 0)
    def _():
        xb_ref[...] = _load2(xp_ref, xs_ref, pl.program_id(0), n_p).astype(BF16)

    o_ref[...] = _dot(xb_ref[...], w_ref[...]).astype(o_ref.dtype)


def _in_proj(x_p, x_s, w):
    (mp, k), ms = x_p.shape, x_s.shape[0]
    m, n = mp + ms, w.shape[1]
    tm, tn = _tile(ms, 1024), _tile(n, 1024)
    n_p = mp // tm
    xp_spec, xs_spec = _two_stream_specs(tm, k, n_p)
    return pl.pallas_call(
        functools.partial(_in_proj_kernel, n_p=n_p),
        grid=(m // tm, n // tn),
        in_specs=[xp_spec, xs_spec, pl.BlockSpec((k, tn), lambda i, j: (0, j))],
        out_specs=pl.BlockSpec((tm, tn), lambda i, j: (i, j)),
        out_shape=jax.ShapeDtypeStruct((m, n), BF16),
        scratch_shapes=[pltpu.VMEM((tm, k), BF16)],
        compiler_params=_params(2),
        name="in_proj",
    )(x_p, x_s, w)


def _conv_kernel(bg_ref, cg_ref, hc_ref, st_ref, w_ref, g_ref, o_ref, ns_ref, *, t):
    rows = bg_ref.shape[0]
    nseq = rows // t
    row = lax.broadcasted_iota(I32, (rows, LANE), 0)
    for c in range(CONV_DIM // LANE):
        cs = slice(c * LANE, (c + 1) * LANE)
        z = cg_ref[:, cs].astype(F32) * hc_ref[:, cs].astype(F32)
        z1, z2 = pltpu.roll(z, 1, 0), pltpu.roll(z, 2, 0)
        for s in range(nseq):
            prev2, prev1 = st_ref[s, 0:1, cs], st_ref[s, 1:2, cs]
            z1 = jnp.where(row == s * t, prev1, z1)
            z2 = jnp.where(row == s * t, prev2, jnp.where(row == s * t + 1, prev1, z2))
            ns_ref[s, :, cs] = z[(s + 1) * t - 2:(s + 1) * t, :]
        w = w_ref[:, cs]
        a = bg_ref[:, cs].astype(F32) * (w[0:1] * z2 + w[1:2] * z1 + w[2:3] * z)
        o_ref[:, cs] = _rms(a, g_ref[:, cs]).astype(o_ref.dtype)


def _conv_mixer(proj, state, conv_w, g_mix, nb, t, row0):
    nseq = max(1, min(nb, 256 // t))
    rows = nseq * t
    rb0 = row0 // rows

    def col(c0):
        return pl.BlockSpec((rows, CONV_DIM), lambda b: (rb0 + b, c0 // CONV_DIM))

    state_spec = pl.BlockSpec((nseq, CONV_WIDTH - 1, CONV_DIM), lambda b: (b, 0, 0))
    return pl.pallas_call(
        functools.partial(_conv_kernel, t=t),
        grid=(nb // nseq,),
        in_specs=[col(COL_BG), col(COL_CG), col(COL_HC), state_spec,
                  pl.BlockSpec((CONV_WIDTH, CONV_DIM), lambda b: (0, 0)),
                  pl.BlockSpec((1, CONV_DIM), lambda b: (0, 0))],
        out_specs=[pl.BlockSpec((rows, CONV_DIM), lambda b: (b, 0)), state_spec],
        out_shape=[jax.ShapeDtypeStruct((nb * t, CONV_DIM), BF16),
                   jax.ShapeDtypeStruct((nb, CONV_WIDTH - 1, CONV_DIM), F32)],
        compiler_params=_params(1),
        name="conv_mixer",
    )(proj, proj, proj, state, conv_w, g_mix)


def _sgu_kernel(u_ref, v_ref, w_ref, bt_ref, lg_ref, lb_ref, g_ref, o_ref, *vr_refs, chunk):
    tr = u_ref.shape[0]
    r = lax.broadcasted_iota(I32, (chunk, chunk), 0)
    c = lax.broadcasted_iota(I32, (chunk, chunk), 1)
    for h in range(SGU_HEADS):
        cs = slice(h * HEAD, (h + 1) * HEAD)
        wm = jnp.where(c <= r, w_ref[h][:chunk, :chunk], 0.0).astype(BF16)
        bias = bt_ref[:chunk, h:h + 1]
        for k in range(tr // chunk):
            rs = slice(k * chunk, (k + 1) * chunk)
            vg = _gelu(v_ref[rs, cs].astype(F32))
            xc = vg - jnp.mean(vg, axis=-1, keepdims=True)
            var = jnp.mean(xc * xc, axis=-1, keepdims=True)
            vh = xc * lax.rsqrt(var + 1e-5) * lg_ref[:, cs] + lb_ref[:, cs]
            if vr_refs:
                vr_refs[0][rs, cs] = vh
            s = _dot(wm, vh.astype(BF16)) + bias
            out = _gelu(u_ref[rs, cs].astype(F32)) * s
            o_ref[rs, cs] = _rms(out, g_ref[:, cs]).astype(o_ref.dtype)


def _sgu_mixer(proj, sgu_w, sgu_bt, ln_g, ln_b, g_mix, nrows, row0, chunk, tr, want_v):
    rb0 = row0 // tr
    full = lambda shape: pl.BlockSpec(shape, lambda i: (0,) * len(shape))
    out_specs = [pl.BlockSpec((tr, SGU_DIM), lambda i: (i, 0))]
    out_shape = [jax.ShapeDtypeStruct((nrows, SGU_DIM), BF16)]
    if want_v:
        out_specs.append(pl.BlockSpec((tr, SGU_DIM), lambda i: (i, 0)))
        out_shape.append(jax.ShapeDtypeStruct((nrows, SGU_DIM), F32))
    return pl.pallas_call(
        functools.partial(_sgu_kernel, chunk=chunk),
        grid=(nrows // tr,),
        in_specs=[pl.BlockSpec((tr, SGU_DIM), lambda i: (rb0 + i, COL_U // SGU_DIM)),
                  pl.BlockSpec((tr, SGU_DIM), lambda i: (rb0 + i, COL_V // SGU_DIM)),
                  full((SGU_HEADS, MLP_CHUNK, MLP_CHUNK)),
                  full((MLP_CHUNK, SGU_HEADS)),
                  full((1, SGU_DIM)), full((1, SGU_DIM)),
                  pl.BlockSpec((1, SGU_DIM), lambda i: (0, 1))],
        out_specs=out_specs,
        out_shape=out_shape,
        compiler_params=_params(1),
        name="sgu_mixer",
    )(proj, proj, sgu_w, sgu_bt, ln_g, ln_b, g_mix)


def _rope_q(qm, qr, cos, sin, h):
    nope = qm[:, h * QK_PAD:h * QK_PAD + QK_NOPE]
    rope = (qm[:, h * QK_PAD + QK_NOPE:(h + 1) * QK_PAD] * cos
            + qr[:, h * LANE:(h + 1) * LANE] * sin)
    return nope * SM_SCALE, rope * SM_SCALE


def _rope_k(kr_ref, cos, sin):
    blk = kr_ref[...].astype(F32)
    return blk * cos + pltpu.roll(blk, QK_ROPE, 1) * sin


def _lora_inputs(refs):
    cq = jnp.concatenate([r[...] for r in refs[:N_CQ_BLK]], axis=1).astype(F32)
    ckv = jnp.concatenate([r[...] for r in refs[N_CQ_BLK:N_CQ_BLK + N_CKV_BLK]], axis=1
                          ).astype(F32)
    return cq, ckv, refs[N_CQ_BLK + N_CKV_BLK:]


def _lora_specs(tm, rb0):
    def blk(c0, k):
        return pl.BlockSpec((tm, LORA_BLK), lambda i: (rb0 + i, c0 // LORA_BLK + k))

    return ([blk(COL_CQ, k) for k in range(N_CQ_BLK)]
            + [blk(COL_CKV, k) for k in range(N_CKV_BLK)]
            + [pl.BlockSpec((tm, LANE), lambda i: (rb0 + i, COL_KR // LANE))])


def _mla_pre_prompt_kernel(*refs):
    cq, ckv_in, refs = _lora_inputs(refs)
    (kr_ref, qg_ref, kvg_ref, wqm_ref, wqr_ref, wuk_ref, wuv_ref, cos_ref, sin_ref,
     q_ref, k_ref, v_ref, ckv_out_ref, kr_out_ref) = refs
    cos, sin = cos_ref[...], sin_ref[...]
    cqn = _rms(cq, qg_ref[...]).astype(BF16)
    qm = _dot(cqn, wqm_ref[...])
    qr = _dot(cqn, wqr_ref[...])
    ckv = _rms(ckv_in, kvg_ref[...])
    ckv_out_ref[...] = ckv
    ckb = ckv.astype(BF16)
    kn = _dot(ckb, wuk_ref[...])
    v_ref[...] = _dot(ckb, wuv_ref[...]).astype(v_ref.dtype)
    krp = _rope_k(kr_ref, cos, sin)
    kr_out_ref[...] = krp[:, :QK_ROPE]
    krb = krp.astype(BF16)
    for h in range(MLA_HEADS):
        nope, rope = _rope_q(qm, qr, cos, sin, h)
        q_ref[:, h * QK_PAD:h * QK_PAD + QK_NOPE] = nope.astype(BF16)
        q_ref[:, h * QK_PAD + QK_NOPE:(h + 1) * QK_PAD] = rope.astype(BF16)
        k_ref[:, h * QK_PAD:h * QK_PAD + QK_NOPE] = kn[:, h * LANE:(h + 1) * LANE].astype(BF16)
        k_ref[:, h * QK_PAD + QK_NOPE:(h + 1) * QK_PAD] = krb


def _mla_pre_prompt(proj, lw, cos, sin, mp, t):
    tm = _tile(t, 512)
    nt = t // tm
    full = lambda shape: pl.BlockSpec(shape, lambda i: (0,) * len(shape))
    hq, hv = MLA_HEADS * QK_PAD, MLA_HEADS * V_HEAD
    return pl.pallas_call(
        _mla_pre_prompt_kernel,
        grid=(mp // tm,),
        in_specs=_lora_specs(tm, 0) + [
                  full((1, Q_LORA)), full((1, KV_LORA)),
                  full((Q_LORA, hq)), full((Q_LORA, MLA_HEADS * LANE)),
                  full((KV_LORA, hv)), full((KV_LORA, hv)),
                  pl.BlockSpec((tm, LANE), lambda i: (i % nt, 0)),
                  pl.BlockSpec((tm, LANE), lambda i: (i % nt, 0))],
        out_specs=[pl.BlockSpec((tm, hq), lambda i: (i, 0)),
                   pl.BlockSpec((tm, hq), lambda i: (i, 0)),
                   pl.BlockSpec((tm, hv), lambda i: (i, 0)),
                   pl.BlockSpec((tm, KV_LORA), lambda i: (i, 0)),
                   pl.BlockSpec((tm, QK_ROPE), lambda i: (i, 0))],
        out_shape=[jax.ShapeDtypeStruct((mp, hq), BF16),
                   jax.ShapeDtypeStruct((mp, hq), BF16),
                   jax.ShapeDtypeStruct((mp, hv), BF16),
                   jax.ShapeDtypeStruct((mp, KV_LORA), F32),
                   jax.ShapeDtypeStruct((mp, QK_ROPE), F32)],
        compiler_params=_params(1),
        name="mla_pre_prompt",
    )(*[proj] * (N_CQ_BLK + N_CKV_BLK + 1), lw["q_norm_g"], lw["kv_norm_g"], lw["wq_main"], lw["wq_rot"],
      lw["w_uk"], lw["w_uv"], cos, sin)


def _mla_pre_sample_kernel(*refs):
    cq, ckv_in, refs = _lora_inputs(refs)
    (kr_ref, qg_ref, kvg_ref, wqm_ref, wqr_ref, wukt_ref, cos_ref, sin_ref,
     ql_ref, qr_ref, ckv_out_ref, kr_out_ref) = refs
    cos, sin = cos_ref[...], sin_ref[...]
    cqn = _rms(cq, qg_ref[...]).astype(BF16)
    qm = _dot(cqn, wqm_ref[...])
    qr = _dot(cqn, wqr_ref[...])
    ckv_out_ref[...] = _rms(ckv_in, kvg_ref[...])
    kr_out_ref[...] = _rope_k(kr_ref, cos, sin)[:, :QK_ROPE]
    for h in range(MLA_HEADS):
        nope, rope = _rope_q(qm, qr, cos, sin, h)
        ql_ref[h] = _dot(nope.astype(BF16), wukt_ref[h]).astype(BF16)
        qr_ref[h] = rope[:, :QK_ROPE].astype(BF16)


def _mla_pre_sample(proj, lw, cos, sin, mp, ms, tm):
    rb0 = mp // tm
    full = lambda shape: pl.BlockSpec(shape, lambda i: (0,) * len(shape))
    return pl.pallas_call(
        _mla_pre_sample_kernel,
        grid=(ms // tm,),
        in_specs=_lora_specs(tm, rb0) + [
                  full((1, Q_LORA)), full((1, KV_LORA)),
                  full((Q_LORA, MLA_HEADS * QK_PAD)), full((Q_LORA, MLA_HEADS * LANE)),
                  full((MLA_HEADS, QK_NOPE, KV_LORA)),
                  full((tm, LANE)), full((tm, LANE))],
        out_specs=[pl.BlockSpec((MLA_HEADS, tm, KV_LORA), lambda i: (0, i, 0)),
                   pl.BlockSpec((MLA_HEADS, tm, QK_ROPE), lambda i: (0, i, 0)),
                   pl.BlockSpec((tm, KV_LORA), lambda i: (i, 0)),
                   pl.BlockSpec((tm, QK_ROPE), lambda i: (i, 0))],
        out_shape=[jax.ShapeDtypeStruct((MLA_HEADS, ms, KV_LORA), BF16),
                   jax.ShapeDtypeStruct((MLA_HEADS, ms, QK_ROPE), BF16),
                   jax.ShapeDtypeStruct((ms, KV_LORA), F32),
                   jax.ShapeDtypeStruct((ms, QK_ROPE), F32)],
        compiler_params=_params(1),
        name="mla_pre_sample",
    )(*[proj] * (N_CQ_BLK + N_CKV_BLK + 1), lw["q_norm_g"], lw["kv_norm_g"], lw["wq_main"], lw["wq_rot"],
      lw["w_ukt"], cos, sin)


ATTN_Q_BLOCK = 512
ATTN_HEADS_PER_STEP = 4


def _attn_prompt_kernel(q_ref, k_ref, v_ref, g_ref, o_ref, *, tq):
    t = q_ref.shape[0]
    rc = lax.broadcasted_iota(I32, (tq, tq), 0) // CHUNK
    cc = lax.broadcasted_iota(I32, (tq, tq), 1) // CHUNK
    visible = cc <= rc
    for h in range(ATTN_HEADS_PER_STEP):
        qk = slice(h * QK_PAD, (h + 1) * QK_PAD)
        hv = slice(h * V_HEAD, (h + 1) * V_HEAD)
        g = g_ref[:, hv]
        for i in range(t // tq):
            lo, hi = i * tq, (i + 1) * tq
            q = q_ref[lo:hi, qk]
            sd = jnp.where(visible, _dot_nt(q, k_ref[lo:hi, qk]), -jnp.inf)
            m = jnp.max(sd, axis=-1, keepdims=True)
            if i:
                so = _dot_nt(q, k_ref[0:lo, qk])
                m = jnp.maximum(m, jnp.max(so, axis=-1, keepdims=True))
            pd = jnp.exp(sd - m)
            l = jnp.sum(pd, axis=-1, keepdims=True)
            o = _dot(pd.astype(BF16), v_ref[lo:hi, hv])
            if i:
                po = jnp.exp(so - m)
                l = l + jnp.sum(po, axis=-1, keepdims=True)
                o = o + _dot(po.astype(BF16), v_ref[0:lo, hv])
            o_ref[lo:hi, hv] = _rms(o / l, g).astype(o_ref.dtype)


def _attn_prompt(q, k, v, g_mix, nb, t):
    tq = _tile(t, ATTN_Q_BLOCK)
    nh = ATTN_HEADS_PER_STEP
    c0 = (CONV_DIM + SGU_DIM) // (nh * HEAD)
    return pl.pallas_call(
        functools.partial(_attn_prompt_kernel, tq=tq),
        grid=(nb, MLA_HEADS // nh),
        in_specs=[pl.BlockSpec((t, nh * QK_PAD), lambda b, h: (b, h)),
                  pl.BlockSpec((t, nh * QK_PAD), lambda b, h: (b, h)),
                  pl.BlockSpec((t, nh * V_HEAD), lambda b, h: (b, h)),
                  pl.BlockSpec((1, nh * HEAD), lambda b, h: (0, c0 + h))],
        out_specs=pl.BlockSpec((t, nh * HEAD), lambda b, h: (b, h)),
        out_shape=jax.ShapeDtypeStruct((nb * t, MLA_HEADS * HEAD), BF16),
        compiler_params=_params(2),
        name="attn_prompt",
    )(q, k, v, g_mix)


def _attn_sample_kernel(ql_ref, qr_ref, cc_ref, ck_ref, nc_ref, nk_ref, wuv_ref, g_ref, o_ref):
    s_len = ql_ref.shape[1]
    ql = ql_ref[...].reshape(MLA_HEADS * s_len, KV_LORA)
    qr = qr_ref[...].reshape(MLA_HEADS * s_len, QK_ROPE)
    cc = cc_ref[0, 0].astype(BF16)
    ck = ck_ref[0, 0].astype(BF16)
    nc = nc_ref[...].astype(BF16)
    nk = nk_ref[...].astype(BF16)
    sp = _dot_nt(ql, cc) + _dot_nt(qr, ck)
    sn = _dot_nt(ql, nc) + _dot_nt(qr, nk)
    m = jnp.maximum(jnp.max(sp, axis=-1, keepdims=True), jnp.max(sn, axis=-1, keepdims=True))
    pp, pn = jnp.exp(sp - m), jnp.exp(sn - m)
    l = jnp.sum(pp, axis=-1, keepdims=True) + jnp.sum(pn, axis=-1, keepdims=True)
    o_lat = ((_dot(pp.astype(BF16), cc) + _dot(pn.astype(BF16), nc)) / l).astype(BF16)
    for h in range(MLA_HEADS):
        oh = _dot(o_lat[h * s_len:(h + 1) * s_len], wuv_ref[h])
        o_ref[:, h * HEAD:(h + 1) * HEAD] = _rms(oh, g_ref[:, h * HEAD:(h + 1) * HEAD]
                                                 ).astype(o_ref.dtype)


def _attn_sample(q_lat, q_rope, cache_ckv, cache_krope, ckv_new, kr_new, wuv_h, g_mix,
                 layer, nb, s_len):
    past = cache_ckv.shape[2]
    full = lambda shape: pl.BlockSpec(shape, lambda b: (0,) * len(shape))
    return pl.pallas_call(
        _attn_sample_kernel,
        grid=(nb,),
        in_specs=[pl.BlockSpec((MLA_HEADS, s_len, KV_LORA), lambda b: (0, b, 0)),
                  pl.BlockSpec((MLA_HEADS, s_len, QK_ROPE), lambda b: (0, b, 0)),
                  pl.BlockSpec((1, 1, past, KV_LORA), lambda b: (layer, b, 0, 0)),
                  pl.BlockSpec((1, 1, past, QK_ROPE), lambda b: (layer, b, 0, 0)),
                  pl.BlockSpec((s_len, KV_LORA), lambda b: (b, 0)),
                  pl.BlockSpec((s_len, QK_ROPE), lambda b: (b, 0)),
                  full((MLA_HEADS, KV_LORA, V_HEAD)),
                  pl.BlockSpec((1, MLA_HEADS * HEAD), lambda b: (0, 1))],
        out_specs=pl.BlockSpec((s_len, MLA_HEADS * HEAD), lambda b: (b, 0)),
        out_shape=jax.ShapeDtypeStruct((nb * s_len, MLA_HEADS * HEAD), BF16),
        compiler_params=_params(1),
        name="attn_sample",
    )(q_lat, q_rope, cache_ckv, cache_krope, ckv_new, kr_new, wuv_h, g_mix)


def _out_proj_kernel(ap_ref, bp_ref, cp_ref, as_ref, bs_ref, cs_ref, w_ref, xp_ref, xs_ref,
                     g_ref, b_ref, op_ref, os_ref, *rest, n_p, packed):
    mx_ref = rest[-1]
    i = pl.program_id(0)
    c1, c2 = CONV_DIM, CONV_DIM + SGU_DIM

    @pl.when(i < n_p)
    def _():
        mx_ref[:, :c1] = ap_ref[...]
        mx_ref[:, c1:c2] = bp_ref[...]
        mx_ref[:, c2:] = cp_ref[...]

    @pl.when(i >= n_p)
    def _():
        mx_ref[:, :c1] = as_ref[...]
        mx_ref[:, c1:c2] = bs_ref[...]
        mx_ref[:, c2:] = cs_ref[...]

    y = ALPHA * _load2(xp_ref, xs_ref, i, n_p) + _dot(mx_ref[...], w_ref[...])
    out = _layernorm(y, g_ref[...], b_ref[...])
    _store2(op_ref, os_ref, i, n_p, out)
    if packed:
        rest[0][...] = _pack_halves(out)
    else:
        _store2(rest[0], rest[1], i, n_p, out)


def _out_proj(mix_p, mix_s, w_o, x_p, x_s, g, b, packed):
    (mp, d), ms = x_p.shape, x_s.shape[0]
    tm = _tile(ms, 256)
    n_p = mp // tm
    m = mp + ms
    full = lambda shape: pl.BlockSpec(shape, lambda i: (0,) * len(shape))
    in_specs = []
    for arrs, clamp in ((mix_p, lambda i: (jnp.minimum(i, n_p - 1), 0)),
                        (mix_s, lambda i: (jnp.maximum(i - n_p, 0), 0))):
        in_specs += [pl.BlockSpec((tm, a.shape[1]), clamp) for a in arrs]
    xp_spec, xs_spec = _two_stream_specs(tm, d, n_p)
    if packed:
        low_specs = [pl.BlockSpec((tm, d // 2), lambda i: (i, 0))]
        low_shapes = [jax.ShapeDtypeStruct((m, d // 2), I32)]
    else:
        low_specs = [xp_spec, xs_spec]
        low_shapes = [jax.ShapeDtypeStruct((mp, d), BF16), jax.ShapeDtypeStruct((ms, d), BF16)]
    return pl.pallas_call(
        functools.partial(_out_proj_kernel, n_p=n_p, packed=packed),
        grid=(m // tm,),
        in_specs=in_specs + [full((d, d)), xp_spec, xs_spec, full((1, d)), full((1, d))],
        out_specs=[xp_spec, xs_spec] + low_specs,
        out_shape=[jax.ShapeDtypeStruct((mp, d), F32), jax.ShapeDtypeStruct((ms, d), F32)]
                  + low_shapes,
        scratch_shapes=[pltpu.VMEM((tm, d), BF16)],
        compiler_params=_params(1),
        name="out_proj_ln",
    )(*mix_p, *mix_s, w_o, x_p, x_s, g, b)


def _swiglu_accumulate(xb_ref, wg_ref, wu_ref, wd_ref, acc_ref):
    xb = xb_ref[...]
    a = _dot(xb, wg_ref[...])
    h = a * (1.0 / (1.0 + jnp.exp(-a))) * _dot(xb, wu_ref[...])
    acc_ref[...] += _dot(h.astype(BF16), wd_ref[...])


def _ffn_kernel(xb_ref, wg_ref, wu_ref, wd_ref, x_ref, g_ref, b_ref, o_ref, ob_ref, acc_ref):
    f = pl.program_id(1)

    @pl.when(f == 0)
    def _():
        acc_ref[...] = jnp.zeros_like(acc_ref)

    _swiglu_accumulate(xb_ref, wg_ref, wu_ref, wd_ref, acc_ref)

    @pl.when(f == pl.num_programs(1) - 1)
    def _():
        out = _layernorm(ALPHA * x_ref[...] + acc_ref[...], g_ref[...], b_ref[...])
        o_ref[...] = out
        ob_ref[...] = out.astype(BF16)


def _ffn(xb, x, wg, wu, wd, g, b):
    m, d = x.shape
    ff = wg.shape[1]
    tm, tf = _tile(m, 512), _tile(ff, 512)
    row = lambda i, f: (i, 0)
    vec = pl.BlockSpec((1, d), lambda i, f: (0, 0))
    return pl.pallas_call(
        _ffn_kernel,
        grid=(m // tm, ff // tf),
        in_specs=[pl.BlockSpec((tm, d), row),
                  pl.BlockSpec((d, tf), lambda i, f: (0, f)),
                  pl.BlockSpec((d, tf), lambda i, f: (0, f)),
                  pl.BlockSpec((tf, d), lambda i, f: (f, 0)),
                  pl.BlockSpec((tm, d), row), vec, vec],
        out_specs=[pl.BlockSpec((tm, d), row), pl.BlockSpec((tm, d), row)],
        out_shape=[jax.ShapeDtypeStruct((m, d), F32), jax.ShapeDtypeStruct((m, d), BF16)],
        scratch_shapes=[pltpu.VMEM((tm, d), F32)],
        compiler_params=_params(2),
        name="ffn_dense",
    )(xb, wg, wu, wd, x, g, b)


EXPERT_ROWS = 16
ROW_TILE = 512


def _router_kernel(xp_ref, xs_ref, wh_ref, wl_ref, gate_ref, rank_ref, cnt_ref, carry_ref, *, n_p):
    i = pl.program_id(0)

    @pl.when(i == 0)
    def _():
        carry_ref[...] = jnp.zeros_like(carry_ref)

    x = _load2(xp_ref, xs_ref, i, n_p)
    ts = x.shape[0]
    xh = x.astype(BF16)
    xl = (x - xh.astype(F32)).astype(BF16)
    wh, wl = wh_ref[...], wl_ref[...]
    logits = _dot_nt(wh, xh) + (_dot_nt(wh, xl) + _dot_nt(wl, xh))
    row = lax.broadcasted_iota(I32, logits.shape, 0)
    logits = jnp.where(row < N_EXPERTS, logits, -jnp.inf)
    m1 = jnp.max(logits, axis=0, keepdims=True)
    i1 = jnp.min(jnp.where(logits == m1, row, EXPERT_ROWS), axis=0, keepdims=True)
    rest = jnp.where(row == i1, -jnp.inf, logits)
    m2 = jnp.max(rest, axis=0, keepdims=True)
    i2 = jnp.min(jnp.where(rest == m2, row, EXPERT_ROWS), axis=0, keepdims=True)
    e2 = jnp.exp(m2 - m1)
    g1 = 1.0 / (1.0 + e2)
    g2 = e2 / (1.0 + e2)
    gate_ref[...] = jnp.where(row == i1, g1, jnp.where(row == i2, g2, 0.0))
    sel = jnp.where(row == i1, 1.0, jnp.where(row == i2, 1.0, 0.0))
    src = lax.broadcasted_iota(I32, (ts, ts), 0)
    dst = lax.broadcasted_iota(I32, (ts, ts), 1)
    incl = _dot(sel.astype(BF16), jnp.where(src <= dst, 1.0, 0.0).astype(BF16))
    carry = carry_ref[:, 0:1]
    rank_ref[...] = jnp.where(sel > 0.0, carry + incl - sel, -1.0).astype(I32)
    carry = carry + jnp.sum(sel, axis=1, keepdims=True)
    carry_ref[...] = jnp.broadcast_to(carry, carry_ref.shape)
    cnt_ref[...] = jnp.broadcast_to(carry, carry_ref.shape).astype(I32)


def _router(x_p, x_s, wh, wl):
    (mp, d), ms = x_p.shape, x_s.shape[0]
    m = mp + ms
    ts = _tile(ms, 512)
    n_p = mp // ts
    full = pl.BlockSpec((EXPERT_ROWS, d), lambda i: (0, 0))
    col = pl.BlockSpec((EXPERT_ROWS, ts), lambda i: (0, i))
    xp_spec, xs_spec = _two_stream_specs(ts, d, n_p)
    return pl.pallas_call(
        functools.partial(_router_kernel, n_p=n_p),
        grid=(m // ts,),
        in_specs=[xp_spec, xs_spec, full, full],
        out_specs=[col, col, pl.BlockSpec((EXPERT_ROWS, LANE), lambda i: (0, 0))],
        out_shape=[jax.ShapeDtypeStruct((EXPERT_ROWS, m), F32),
                   jax.ShapeDtypeStruct((EXPERT_ROWS, m), I32),
                   jax.ShapeDtypeStruct((EXPERT_ROWS, LANE), I32)],
        scratch_shapes=[pltpu.VMEM((EXPERT_ROWS, LANE), F32)],
        compiler_params=_params(1),
        name="router",
    )(x_p, x_s, wh, wl)


def _route_plan(gate_t, rank_t, cnt, m, td):
    ne = N_EXPERTS
    total = cnt[:ne, 0]
    padded = (total + td - 1) // td * td
    ends = jnp.cumsum(padded)
    off = ends - padded
    n_act = (ends[-1] // td).astype(I32)
    rt = -(-(2 * m + ne * (td - 1)) // td)
    tiles = jnp.minimum(jnp.arange(rt, dtype=I32), n_act - 1)
    tile_expert = jnp.sum(tiles[:, None] >= (ends // td)[None, :], axis=1).astype(I32)

    rank, gate = rank_t[:ne], gate_t[:ne]
    chosen = rank >= 0
    pos = off[:, None] + rank
    pos_a = jnp.min(jnp.where(chosen, pos, rt * td), axis=0).astype(I32)
    pos_b = jnp.max(jnp.where(chosen, pos, -1), axis=0).astype(I32)
    gate_a = jnp.sum(jnp.where(chosen & (pos == pos_a[None]), gate, 0.0), axis=0)
    gate_b = jnp.sum(jnp.where(chosen & (pos == pos_b[None]), gate, 0.0), axis=0)
    return pos_a, pos_b, gate_a[:, None], gate_b[:, None], tile_expert, n_act.reshape(1), rt


def _sc_mesh():
    return plsc.VectorSubcoreMesh(core_axis_name="c", subcore_axis_name="s")


def _sc_worker_base(per_worker):
    wid = lax.axis_index("s") * SC_CORES + lax.axis_index("c")
    return wid * per_worker


def _sc_scatter_rows(x, idx_a, idx_b, n_rows):
    m, w = x.shape
    per_worker = m // (SC_CORES * SC_SUBCORES)
    assert per_worker * SC_CORES * SC_SUBCORES == m and per_worker % SC_ROWS == 0, m

    @functools.partial(
        pl.kernel, mesh=_sc_mesh(),
        out_type=jax.ShapeDtypeStruct((n_rows, w), x.dtype),
        scratch_types=[pltpu.VMEM((SC_ROWS,), I32), pltpu.VMEM((SC_ROWS,), I32),
                       pltpu.VMEM((SC_ROWS, w), x.dtype),
                       pltpu.SemaphoreType.DMA, pltpu.SemaphoreType.DMA])
    def scatter(x_hbm, ia_hbm, ib_hbm, out_hbm, ia_v, ib_v, rows_v, sem_a, sem_b):
        base = _sc_worker_base(per_worker)

        @pl.loop(0, per_worker // SC_ROWS)
        def _(c):
            lo = pl.multiple_of(base + c * SC_ROWS, 8)
            pltpu.sync_copy(ia_hbm.at[pl.ds(lo, SC_ROWS)], ia_v)
            pltpu.sync_copy(ib_hbm.at[pl.ds(lo, SC_ROWS)], ib_v)
            pltpu.sync_copy(x_hbm.at[pl.ds(lo, SC_ROWS)], rows_v)
            put_a = pltpu.async_copy(rows_v, out_hbm.at[ia_v], sem_a)
            put_b = pltpu.async_copy(rows_v, out_hbm.at[ib_v], sem_b)
            put_a.wait()
            put_b.wait()

    return scatter(x, idx_a, idx_b)


def _sc_gather_rows(table, idx):
    m, w = idx.shape[0], table.shape[1]
    per_worker = m // (SC_CORES * SC_SUBCORES)
    assert per_worker * SC_CORES * SC_SUBCORES == m and per_worker % SC_ROWS == 0, m

    @functools.partial(
        pl.kernel, mesh=_sc_mesh(),
        out_type=jax.ShapeDtypeStruct((m, w), table.dtype),
        scratch_types=[pltpu.VMEM((SC_ROWS,), I32), pltpu.VMEM((SC_ROWS, w), table.dtype),
                       pltpu.SemaphoreType.DMA])
    def gather(table_hbm, idx_hbm, out_hbm, idx_v, rows_v, sem):
        base = _sc_worker_base(per_worker)

        @pl.loop(0, per_worker // SC_ROWS)
        def _(c):
            lo = pl.multiple_of(base + c * SC_ROWS, 8)
            pltpu.sync_copy(idx_hbm.at[pl.ds(lo, SC_ROWS)], idx_v)
            pltpu.async_copy(table_hbm.at[idx_v], rows_v, sem).wait()
            pltpu.sync_copy(rows_v, out_hbm.at[pl.ds(lo, SC_ROWS)])

    return gather(table, idx)


def _experts_kernel(te_ref, na_ref, xs_ref, wg_ref, wu_ref, wd_ref, o_ref, xb_ref, acc_ref):
    del te_ref
    j, f = pl.program_id(0), pl.program_id(1)
    half = xs_ref.shape[1]

    @pl.when(j < na_ref[0])
    def _():
        @pl.when(f == 0)
        def _():
            lo, hi = _unpack_halves(xs_ref[...])
            xb_ref[:, :half] = lo.astype(BF16)
            xb_ref[:, half:] = hi.astype(BF16)
            acc_ref[...] = jnp.zeros_like(acc_ref)

        _swiglu_accumulate(xb_ref, wg_ref.at[0], wu_ref.at[0], wd_ref.at[0], acc_ref)

        @pl.when(f == pl.num_programs(1) - 1)
        def _():
            o_ref[...] = _pack_halves(acc_ref[...])


def _experts(tile_expert, n_act, xs, wg, wu, wd, td):
    rows, half = xs.shape
    d = 2 * half
    ff = wg.shape[2]
    tf = _tile(ff, 512)
    nf = ff // tf
    row = lambda j, f, te, na: (jnp.minimum(j, na[0] - 1), 0)
    fcol = lambda j, f, na: jnp.where(j < na[0], f, nf - 1)
    return pl.pallas_call(
        _experts_kernel,
        grid_spec=pltpu.PrefetchScalarGridSpec(
            num_scalar_prefetch=2,
            grid=(rows // td, nf),
            in_specs=[pl.BlockSpec((td, half), row),
                      pl.BlockSpec((1, d, tf), lambda j, f, te, na: (te[j], 0, fcol(j, f, na))),
                      pl.BlockSpec((1, d, tf), lambda j, f, te, na: (te[j], 0, fcol(j, f, na))),
                      pl.BlockSpec((1, tf, d), lambda j, f, te, na: (te[j], fcol(j, f, na), 0))],
            out_specs=pl.BlockSpec((td, half), row),
            scratch_shapes=[pltpu.VMEM((td, d), BF16), pltpu.VMEM((td, d), F32)]),
        out_shape=jax.ShapeDtypeStruct((rows, half), I32),
        compiler_params=_params(2),
        name="moe_experts",
    )(tile_expert, n_act, xs, wg, wu, wd)


def _moe_finish_kernel(oa_ref, ob_ref, ga_ref, gb_ref, xp_ref, xs_ref, g_ref, b_ref,
                       op_ref, os_ref, *, n_p):
    i = pl.program_id(0)
    a_lo, a_hi = _unpack_halves(oa_ref[...])
    b_lo, b_hi = _unpack_halves(ob_ref[...])
    ga, gb = ga_ref[...], gb_ref[...]
    y = jnp.concatenate([ga * a_lo + gb * b_lo, ga * a_hi + gb * b_hi], axis=1)
    out = _layernorm(ALPHA * _load2(xp_ref, xs_ref, i, n_p) + y, g_ref[...], b_ref[...])
    _store2(op_ref, os_ref, i, n_p, out)


def _moe_finish(o_a, o_b, gate_a, gate_b, x_p, x_s, g, b):
    (mp, d), ms = x_p.shape, x_s.shape[0]
    m = mp + ms
    tm = _tile(ms, 512)
    n_p = mp // tm
    row = lambda i: (i, 0)
    vec = pl.BlockSpec((1, d), lambda i: (0, 0))
    xp_spec, xs_spec = _two_stream_specs(tm, d, n_p)
    return pl.pallas_call(
        functools.partial(_moe_finish_kernel, n_p=n_p),
        grid=(m // tm,),
        in_specs=[pl.BlockSpec((tm, d // 2), row), pl.BlockSpec((tm, d // 2), row),
                  pl.BlockSpec((tm, 1), row), pl.BlockSpec((tm, 1), row),
                  xp_spec, xs_spec, vec, vec],
        out_specs=[xp_spec, xs_spec],
        out_shape=[jax.ShapeDtypeStruct((mp, d), F32), jax.ShapeDtypeStruct((ms, d), F32)],
        compiler_params=_params(1),
        name="moe_finish",
    )(o_a, o_b, gate_a, gate_b, x_p, x_s, g, b)


def _moe(x_p, x_s, x_packed, router_w, wg, wu, wd, g, b):
    m = x_p.shape[0] + x_s.shape[0]
    rw = jnp.pad(jnp.transpose(router_w), ((0, EXPERT_ROWS - N_EXPERTS), (0, 0)))
    rwh = rw.astype(BF16)
    rwl = (rw - rwh.astype(F32)).astype(BF16)
    gate_t, rank_t, cnt = _router(x_p, x_s, rwh, rwl)
    td = min(ROW_TILE, _tile(m, ROW_TILE))
    pos_a, pos_b, gate_a, gate_b, tile_expert, n_act, rt = _route_plan(gate_t, rank_t, cnt, m, td)
    xs = _sc_scatter_rows(x_packed, pos_a, pos_b, rt * td)
    o_sorted = _experts(tile_expert, n_act, xs, wg, wu, wd, td)
    o_a = _sc_gather_rows(o_sorted, pos_a)
    o_b = _sc_gather_rows(o_sorted, pos_b)
    return _moe_finish(o_a, o_b, gate_a, gate_b, x_p, x_s, g, b)


def _rot_cols(w):
    half = QK_ROPE // 2
    return jnp.concatenate([-w[..., half:], w[..., :half]], axis=-1)


def _layer_weights(l, w_in, conv_w, sgu_ln_g, sgu_ln_b, sgu_w, sgu_b, q_norm_g, w_uq, kv_norm_g,
                   w_uk, w_uv, mix_norm_g, w_o):
    d = w_in.shape[1]
    wi = w_in[l]
    k_r = wi[:, COL_KR:COL_KR + QK_ROPE]
    w_in_pad = jnp.concatenate(
        [wi.astype(BF16), _rot_cols(k_r).astype(BF16),
         jnp.zeros((d, D_IN_PAD - COL_KR - 2 * QK_ROPE), BF16)], axis=1)
    uq = w_uq[l]
    zeros = jnp.zeros((Q_LORA, MLA_HEADS, QK_PAD - QK_NOPE - QK_ROPE), F32)
    wq_main = jnp.concatenate([uq, zeros], axis=-1).reshape(Q_LORA, MLA_HEADS * QK_PAD)
    wq_rot = jnp.concatenate([_rot_cols(uq[..., QK_NOPE:]), zeros], axis=-1
                             ).reshape(Q_LORA, MLA_HEADS * LANE)
    return {
        "w_in": w_in_pad,
        "conv_w": conv_w[l],
        "sgu_w": sgu_w[l],
        "sgu_bt": jnp.transpose(sgu_b[l]),
        "sgu_ln_g": sgu_ln_g[l][None], "sgu_ln_b": sgu_ln_b[l][None],
        "q_norm_g": q_norm_g[l][None], "kv_norm_g": kv_norm_g[l][None],
        "wq_main": wq_main.astype(BF16), "wq_rot": wq_rot.astype(BF16),
        "w_uk": w_uk[l].reshape(KV_LORA, MLA_HEADS * QK_NOPE).astype(BF16),
        "w_uv": w_uv[l].reshape(KV_LORA, MLA_HEADS * V_HEAD).astype(BF16),
        "w_ukt": jnp.transpose(w_uk[l], (1, 2, 0)).astype(BF16),
        "w_uv_h": jnp.transpose(w_uv[l], (1, 0, 2)).astype(BF16),
        "g_mix": mix_norm_g[l][None],
        "w_o": w_o[l].astype(BF16),
    }


def _rope_tables(pos):
    half = QK_ROPE // 2
    inv = ROPE_THETA ** (-jnp.arange(half, dtype=F32) / half)
    ang = pos.astype(F32)[:, None] * inv[None, :]
    zeros = jnp.zeros((pos.shape[0], LANE - QK_ROPE), F32)
    cos = jnp.concatenate([jnp.cos(ang), jnp.cos(ang), zeros], axis=-1)
    sin = jnp.concatenate([jnp.sin(ang), jnp.sin(ang), zeros], axis=-1)
    return cos, sin


def kernel(x_prompt, x_sample, state_conv, cache_ckv, cache_krope, w_in, conv_w, sgu_ln_g,
           sgu_ln_b, sgu_w, sgu_b, q_norm_g, w_uq, kv_norm_g, w_uk, w_uv, mix_norm_g, w_o,
           ln1_g, ln1_b, ln2_g, ln2_b, ffn_w_gate, ffn_w_up, ffn_w_down, router_w,
           moe_w_gate, moe_w_up, moe_w_down):
    nbp, t, d = x_prompt.shape
    nbs, s_len, _ = x_sample.shape
    past = cache_ckv.shape[2]
    depth = w_in.shape[0]
    mp, ms = nbp * t, nbs * s_len

    x_p, x_s = x_prompt.reshape(mp, d), x_sample.reshape(ms, d)
    xb_p, xb_s = x_p, x_s

    cos_p, sin_p = _rope_tables(jnp.arange(t))
    tms = _tile(ms, 256)
    cos_s, sin_s = _rope_tables(past + jnp.arange(s_len))
    cos_s, sin_s = jnp.tile(cos_s, (tms // s_len, 1)), jnp.tile(sin_s, (tms // s_len, 1))
    zero_state = jnp.zeros((nbp, CONV_WIDTH - 1, CONV_DIM), F32)

    outs = [[] for _ in range(7)]
    for l in range(depth):
        lw = _layer_weights(l, w_in, conv_w, sgu_ln_g, sgu_ln_b, sgu_w, sgu_b, q_norm_g, w_uq,
                            kv_norm_g, w_uk, w_uv, mix_norm_g, w_o)
        routed = l % 2 == 1
        proj = _in_proj(xb_p, xb_s, lw["w_in"])

        a_p, conv_p = _conv_mixer(proj, zero_state, lw["conv_w"], lw["g_mix"], nbp, t, 0)
        a_s, conv_s = _conv_mixer(proj, state_conv[l], lw["conv_w"], lw["g_mix"], nbs, s_len, mp)
        sgu_args = (lw["sgu_w"], lw["sgu_bt"], lw["sgu_ln_g"], lw["sgu_ln_b"], lw["g_mix"])
        chunk_p = min(MLP_CHUNK, t)
        (b_p,) = _sgu_mixer(proj, *sgu_args, mp, 0, chunk_p,
                            _tile(t, 512) if t >= 512 else chunk_p, False)
        chunk_s = min(MLP_CHUNK, s_len)
        b_s, v_rows = _sgu_mixer(proj, *sgu_args, ms, mp, chunk_s, chunk_s, True)
        q, k, v, ckv_p, kr_p = _mla_pre_prompt(proj, lw, cos_p, sin_p, mp, t)
        c_p = _attn_prompt(q, k, v, lw["g_mix"], nbp, t)
        q_lat, q_rope, ckv_s, kr_s = _mla_pre_sample(proj, lw, cos_s, sin_s, mp, ms, tms)
        c_s = _attn_sample(q_lat, q_rope, cache_ckv, cache_krope, ckv_s, kr_s, lw["w_uv_h"],
                           lw["g_mix"], l, nbs, s_len)

        x_p, x_s, *low = _out_proj((a_p, b_p, c_p), (a_s, b_s, c_s), lw["w_o"], x_p, x_s,
                                   ln1_g[l][None], ln1_b[l][None], routed)

        i = l // 2
        if routed:
            x_p, x_s = _moe(x_p, x_s, low[0], router_w[i], moe_w_gate[i].astype(BF16),
                            moe_w_up[i].astype(BF16), moe_w_down[i].astype(BF16),
                            ln2_g[l][None], ln2_b[l][None])
            xb_p, xb_s = x_p, x_s
        else:
            ffn_w = (ffn_w_gate[i].astype(BF16), ffn_w_up[i].astype(BF16),
                     ffn_w_down[i].astype(BF16), ln2_g[l][None], ln2_b[l][None])
            x_p, xb_p = _ffn(low[0], x_p, *ffn_w)
            x_s, xb_s = _ffn(low[1], x_s, *ffn_w)

        for lst, val in zip(outs, (conv_p, ckv_p.reshape(nbp, t, KV_LORA),
                                   kr_p.reshape(nbp, t, QK_ROPE), conv_s,
                                   ckv_s.reshape(nbs, s_len, KV_LORA),
                                   kr_s.reshape(nbs, s_len, QK_ROPE),
                                   v_rows.reshape(nbs, s_len, SGU_DIM))):
            lst.append(val)

    return (x_p.reshape(nbp, t, d), x_s.reshape(nbs, s_len, d), *[jnp.stack(o) for o in outs])
```

```python
import functools

import numpy as np
import jax
import jax.numpy as jnp
from jax import lax
from jax.experimental import pallas as pl
from jax.experimental.pallas import tpu as pltpu
from jax.experimental.pallas import tpu_sc as plsc

F32 = jnp.float32
BF16 = jnp.bfloat16
I32 = jnp.int32

D_MODEL = 2048
CONV_DIM = 512
CONV_WIDTH = 3
SGU_DIM = 512
SGU_HEADS = 4
MLP_CHUNK = 128
MLA_HEADS = 8
QK_NOPE = 128
QK_ROPE = 64
V_HEAD = 128
Q_LORA = 768
KV_LORA = 512
ROPE_THETA = 10000.0
CHUNK = 64
HEAD = 128
N_EXPERTS = 8
DEPTH = 2
SM_SCALE = (QK_NOPE + QK_ROPE) ** -0.5
ALPHA = (2 * DEPTH) ** 0.25

COL_BG, COL_CG, COL_HC, COL_U, COL_V = 0, 512, 1024, 1536, 2048
COL_CQ, COL_CKV, COL_KR = 2560, 3328, 3840
D_IN_PAD = 4096
LORA_BLK = 256
N_CQ_BLK, N_CKV_BLK = Q_LORA // LORA_BLK, KV_LORA // LORA_BLK
QK_PAD = 256

LANE = 128
VMEM_LIMIT = 56 * 1024 * 1024
SC_CORES, SC_SUBCORES = 2, 16
SC_ROWS = 32
SC_LANES = 16


def _params(n_axes):
    return pltpu.CompilerParams(dimension_semantics=("arbitrary",) * n_axes,
                                vmem_limit_bytes=VMEM_LIMIT)


def _tile(n, pref):
    t = pref
    while n % t:
        t //= 2
    assert t >= 8, (n, pref)
    return t


def _rms(x, g, eps=1e-6):
    return x * lax.rsqrt(jnp.mean(x * x, axis=-1, keepdims=True) + eps) * g


def _layernorm(y, g, b, eps=1e-5):
    yc = y - jnp.mean(y, axis=-1, keepdims=True)
    var = jnp.mean(yc * yc, axis=-1, keepdims=True)
    return yc * lax.rsqrt(var + eps) * g + b


def _gelu(x):
    c = np.sqrt(2.0 / np.pi).astype(np.float32)
    return 0.5 * x * (1.0 + jnp.tanh(c * (x + 0.044715 * (x * x * x))))


def _dot(a, b):
    return jnp.dot(a, b, preferred_element_type=F32)


def _dot_nt(a, b):
    return lax.dot_general(a, b, (((1,), (1,)), ((), ())), preferred_element_type=F32)


def _two_stream_specs(tm, width, n_p, single_buffer=False):
    def p_map(i, *_):
        return (jnp.minimum(i, n_p - 1), 0)

    def s_map(i, *_):
        return (jnp.maximum(i - n_p, 0), 0)

    mode = {"pipeline_mode": pl.Buffered(1)} if single_buffer else {}
    return (pl.BlockSpec((tm, width), p_map, **mode), pl.BlockSpec((tm, width), s_map, **mode))


def _load2(p_ref, s_ref, i, n_p):
    return jnp.where(i < n_p, p_ref[...], s_ref[...])


def _store2(p_ref, s_ref, i, n_p, val):
    @pl.when(i < n_p)
    def _():
        p_ref[...] = val.astype(p_ref.dtype)

    @pl.when(i >= n_p)
    def _():
        s_ref[...] = val.astype(s_ref.dtype)


def _pack_halves(y):
    w = y.shape[1] // 2
    lo = lax.bitcast_convert_type(y[:, :w].astype(BF16).astype(F32), jnp.uint32)
    hi = lax.bitcast_convert_type(y[:, w:].astype(BF16).astype(F32), jnp.uint32)
    return lax.bitcast_convert_type(hi | (lo >> 16), I32)


def _unpack_halves(p):
    u = lax.bitcast_convert_type(p, jnp.uint32)
    lo = lax.bitcast_convert_type(u << 16, F32)
    hi = lax.bitcast_convert_type(u & jnp.uint32(0xFFFF0000), F32)
    return lo, hi


def _in_proj_kernel(xp_ref, xs_ref, w_ref, o_ref, xb_ref, *, n_p):
    @pl.when(pl.program_id(1) == 0)
    def _():
        xb_ref[...] = _load2(xp_ref, xs_ref, pl.program_id(0), n_p).astype(BF16)

    o_ref[...] = _dot(xb_ref[...], w_ref[...]).astype(o_ref.dtype)


def _in_proj(x_p, x_s, w):
    (mp, k), ms = x_p.shape, x_s.shape[0]
    m, n = mp + ms, w.shape[1]
    tm, tn = _tile(ms, 1024), _tile(n, 1024)
    n_p = mp // tm
    xp_spec, xs_spec = _two_stream_specs(tm, k, n_p)
    return pl.pallas_call(
        functools.partial(_in_proj_kernel, n_p=n_p),
        grid=(m // tm, n // tn),
        in_specs=[xp_spec, xs_spec, pl.BlockSpec((k, tn), lambda i, j: (0, j))],
        out_specs=pl.BlockSpec((tm, tn), lambda i, j: (i, j)),
        out_shape=jax.ShapeDtypeStruct((m, n), BF16),
        scratch_shapes=[pltpu.VMEM((tm, k), BF16)],
        compiler_params=_params(2),
        name="in_proj",
    )(x_p, x_s, w)


def _conv_kernel(bg_ref, cg_ref, hc_ref, st_ref, w_ref, g_ref, o_ref, ns_ref, *, t):
    rows = bg_ref.shape[0]
    nseq = rows // t
    row = lax.broadcasted_iota(I32, (rows, LANE), 0)
    for c in range(CONV_DIM // LANE):
        cs = slice(c * LANE, (c + 1) * LANE)
        z = cg_ref[:, cs].astype(F32) * hc_ref[:, cs].astype(F32)
        z1, z2 = pltpu.roll(z, 1, 0), pltpu.roll(z, 2, 0)
        for s in range(nseq):
            prev2, prev1 = st_ref[s, 0:1, cs], st_ref[s, 1:2, cs]
            z1 = jnp.where(row == s * t, prev1, z1)
            z2 = jnp.where(row == s * t, prev2, jnp.where(row == s * t + 1, prev1, z2))
            ns_ref[s, :, cs] = z[(s + 1) * t - 2:(s + 1) * t, :]
        w = w_ref[:, cs]
        a = bg_ref[:, cs].astype(F32) * (w[0:1] * z2 + w[1:2] * z1 + w[2:3] * z)
        o_ref[:, cs] = _rms(a, g_ref[:, cs]).astype(o_ref.dtype)


def _conv_mixer(proj, state, conv_w, g_mix, nb, t, row0):
    nseq = max(1, min(nb, 256 // t))
    rows = nseq * t
    rb0 = row0 // rows

    def col(c0):
        return pl.BlockSpec((rows, CONV_DIM), lambda b: (rb0 + b, c0 // CONV_DIM))

    state_spec = pl.BlockSpec((nseq, CONV_WIDTH - 1, CONV_DIM), lambda b: (b, 0, 0))
    return pl.pallas_call(
        functools.partial(_conv_kernel, t=t),
        grid=(nb // nseq,),
        in_specs=[col(COL_BG), col(COL_CG), col(COL_HC), state_spec,
                  pl.BlockSpec((CONV_WIDTH, CONV_DIM), lambda b: (0, 0)),
                  pl.BlockSpec((1, CONV_DIM), lambda b: (0, 0))],
        out_specs=[pl.BlockSpec((rows, CONV_DIM), lambda b: (b, 0)), state_spec],
        out_shape=[jax.ShapeDtypeStruct((nb * t, CONV_DIM), BF16),
                   jax.ShapeDtypeStruct((nb, CONV_WIDTH - 1, CONV_DIM), F32)],
        compiler_params=_params(1),
        name="conv_mixer",
    )(proj, proj, proj, state, conv_w, g_mix)


def _sgu_kernel(u_ref, v_ref, w_ref, bt_ref, lg_ref, lb_ref, g_ref, o_ref, *vr_refs, chunk):
    tr = u_ref.shape[0]
    r = lax.broadcasted_iota(I32, (chunk, chunk), 0)
    c = lax.broadcasted_iota(I32, (chunk, chunk), 1)
    for h in range(SGU_HEADS):
        cs = slice(h * HEAD, (h + 1) * HEAD)
        wm = jnp.where(c <= r, w_ref[h][:chunk, :chunk], 0.0).astype(BF16)
        bias = bt_ref[:chunk, h:h + 1]
        for k in range(tr // chunk):
            rs = slice(k * chunk, (k + 1) * chunk)
            vg = _gelu(v_ref[rs, cs].astype(F32))
            xc = vg - jnp.mean(vg, axis=-1, keepdims=True)
            var = jnp.mean(xc * xc, axis=-1, keepdims=True)
            vh = xc * lax.rsqrt(var + 1e-5) * lg_ref[:, cs] + lb_ref[:, cs]
            if vr_refs:
                vr_refs[0][rs, cs] = vh
            s = _dot(wm, vh.astype(BF16)) + bias
            out = _gelu(u_ref[rs, cs].astype(F32)) * s
            o_ref[rs, cs] = _rms(out, g_ref[:, cs]).astype(o_ref.dtype)


def _sgu_mixer(proj, sgu_w, sgu_bt, ln_g, ln_b, g_mix, nrows, row0, chunk, tr, want_v):
    rb0 = row0 // tr
    full = lambda shape: pl.BlockSpec(shape, lambda i: (0,) * len(shape))
    out_specs = [pl.BlockSpec((tr, SGU_DIM), lambda i: (i, 0))]
    out_shape = [jax.ShapeDtypeStruct((nrows, SGU_DIM), BF16)]
    if want_v:
        out_specs.append(pl.BlockSpec((tr, SGU_DIM), lambda i: (i, 0)))
        out_shape.append(jax.ShapeDtypeStruct((nrows, SGU_DIM), F32))
    return pl.pallas_call(
        functools.partial(_sgu_kernel, chunk=chunk),
        grid=(nrows // tr,),
        in_specs=[pl.BlockSpec((tr, SGU_DIM), lambda i: (rb0 + i, COL_U // SGU_DIM)),
                  pl.BlockSpec((tr, SGU_DIM), lambda i: (rb0 + i, COL_V // SGU_DIM)),
                  full((SGU_HEADS, MLP_CHUNK, MLP_CHUNK)),
                  full((MLP_CHUNK, SGU_HEADS)),
                  full((1, SGU_DIM)), full((1, SGU_DIM)),
                  pl.BlockSpec((1, SGU_DIM), lambda i: (0, 1))],
        out_specs=out_specs,
        out_shape=out_shape,
        compiler_params=_params(1),
        name="sgu_mixer",
    )(proj, proj, sgu_w, sgu_bt, ln_g, ln_b, g_mix)


def _rope_q(qm, qr, cos, sin, h):
    nope = qm[:, h * QK_PAD:h * QK_PAD + QK_NOPE]
    rope = (qm[:, h * QK_PAD + QK_NOPE:(h + 1) * QK_PAD] * cos
            + qr[:, h * LANE:(h + 1) * LANE] * sin)
    return nope * SM_SCALE, rope * SM_SCALE


def _rope_k(kr_ref, cos, sin):
    blk = kr_ref[...].astype(F32)
    return blk * cos + pltpu.roll(blk, QK_ROPE, 1) * sin


def _lora_inputs(refs):
    cq = jnp.concatenate([r[...] for r in refs[:N_CQ_BLK]], axis=1).astype(F32)
    ckv = jnp.concatenate([r[...] for r in refs[N_CQ_BLK:N_CQ_BLK + N_CKV_BLK]], axis=1
                          ).astype(F32)
    return cq, ckv, refs[N_CQ_BLK + N_CKV_BLK:]


def _lora_specs(tm, rb0):
    def blk(c0, k):
        return pl.BlockSpec((tm, LORA_BLK), lambda i: (rb0 + i, c0 // LORA_BLK + k))

    return ([blk(COL_CQ, k) for k in range(N_CQ_BLK)]
            + [blk(COL_CKV, k) for k in range(N_CKV_BLK)]
            + [pl.BlockSpec((tm, LANE), lambda i: (rb0 + i, COL_KR // LANE))])


def _mla_pre_prompt_kernel(*refs):
    cq, ckv_in, refs = _lora_inputs(refs)
    (kr_ref, qg_ref, kvg_ref, wqm_ref, wqr_ref, wuk_ref, wuv_ref, cos_ref, sin_ref,
     q_ref, k_ref, v_ref, ckv_out_ref, kr_out_ref) = refs
    cos, sin = cos_ref[...], sin_ref[...]
    cqn = _rms(cq, qg_ref[...]).astype(BF16)
    qm = _dot(cqn, wqm_ref[...])
    qr = _dot(cqn, wqr_ref[...])
    ckv = _rms(ckv_in, kvg_ref[...])
    ckv_out_ref[...] = ckv
    ckb = ckv.astype(BF16)
    kn = _dot(ckb, wuk_ref[...])
    v_ref[...] = _dot(ckb, wuv_ref[...]).astype(v_ref.dtype)
    krp = _rope_k(kr_ref, cos, sin)
    kr_out_ref[...] = krp[:, :QK_ROPE]
    krb = krp.astype(BF16)
    for h in range(MLA_HEADS):
        nope, rope = _rope_q(qm, qr, cos, sin, h)
        q_ref[:, h * QK_PAD:h * QK_PAD + QK_NOPE] = nope.astype(BF16)
        q_ref[:, h * QK_PAD + QK_NOPE:(h + 1) * QK_PAD] = rope.astype(BF16)
        k_ref[:, h * QK_PAD:h * QK_PAD + QK_NOPE] = kn[:, h * LANE:(h + 1) * LANE].astype(BF16)
        k_ref[:, h * QK_PAD + QK_NOPE:(h + 1) * QK_PAD] = krb


def _mla_pre_prompt(proj, lw, cos, sin, mp, t):
    tm = _tile(t, 512)
    nt = t // tm
    full = lambda shape: pl.BlockSpec(shape, lambda i: (0,) * len(shape))
    hq, hv = MLA_HEADS * QK_PAD, MLA_HEADS * V_HEAD
    return pl.pallas_call(
        _mla_pre_prompt_kernel,
        grid=(mp // tm,),
        in_specs=_lora_specs(tm, 0) + [
                  full((1, Q_LORA)), full((1, KV_LORA)),
                  full((Q_LORA, hq)), full((Q_LORA, MLA_HEADS * LANE)),
                  full((KV_LORA, hv)), full((KV_LORA, hv)),
                  pl.BlockSpec((tm, LANE), lambda i: (i % nt, 0)),
                  pl.BlockSpec((tm, LANE), lambda i: (i % nt, 0))],
        out_specs=[pl.BlockSpec((tm, hq), lambda i: (i, 0)),
                   pl.BlockSpec((tm, hq), lambda i: (i, 0)),
                   pl.BlockSpec((tm, hv), lambda i: (i, 0)),
                   pl.BlockSpec((tm, KV_LORA), lambda i: (i, 0)),
                   pl.BlockSpec((tm, QK_ROPE), lambda i: (i, 0))],
        out_shape=[jax.ShapeDtypeStruct((mp, hq), BF16),
                   jax.ShapeDtypeStruct((mp, hq), BF16),
                   jax.ShapeDtypeStruct((mp, hv), BF16),
                   jax.ShapeDtypeStruct((mp, KV_LORA), F32),
                   jax.ShapeDtypeStruct((mp, QK_ROPE), F32)],
        compiler_params=_params(1),
        name="mla_pre_prompt",
    )(*[proj] * (N_CQ_BLK + N_CKV_BLK + 1), lw["q_norm_g"], lw["kv_norm_g"], lw["wq_main"], lw["wq_rot"],
      lw["w_uk"], lw["w_uv"], cos, sin)


def _mla_pre_sample_kernel(*refs):
    cq, ckv_in, refs = _lora_inputs(refs)
    (kr_ref, qg_ref, kvg_ref, wqm_ref, wqr_ref, wukt_ref, cos_ref, sin_ref,
     ql_ref, qr_ref, ckv_out_ref, kr_out_ref) = refs
    cos, sin = cos_ref[...], sin_ref[...]
    cqn = _rms(cq, qg_ref[...]).astype(BF16)
    qm = _dot(cqn, wqm_ref[...])
    qr = _dot(cqn, wqr_ref[...])
    ckv_out_ref[...] = _rms(ckv_in, kvg_ref[...])
    kr_out_ref[...] = _rope_k(kr_ref, cos, sin)[:, :QK_ROPE]
    for h in range(MLA_HEADS):
        nope, rope = _rope_q(qm, qr, cos, sin, h)
        ql_ref[h] = _dot(nope.astype(BF16), wukt_ref[h]).astype(BF16)
        qr_ref[h] = rope[:, :QK_ROPE].astype(BF16)


def _mla_pre_sample(proj, lw, cos, sin, mp, ms, tm):
    rb0 = mp // tm
    full = lambda shape: pl.BlockSpec(shape, lambda i: (0,) * len(shape))
    return pl.pallas_call(
        _mla_pre_sample_kernel,
        grid=(ms // tm,),
        in_specs=_lora_specs(tm, rb0) + [
                  full((1, Q_LORA)), full((1, KV_LORA)),
                  full((Q_LORA, MLA_HEADS * QK_PAD)), full((Q_LORA, MLA_HEADS * LANE)),
                  full((MLA_HEADS, QK_NOPE, KV_LORA)),
                  full((tm, LANE)), full((tm, LANE))],
        out_specs=[pl.BlockSpec((MLA_HEADS, tm, KV_LORA), lambda i: (0, i, 0)),
                   pl.BlockSpec((MLA_HEADS, tm, QK_ROPE), lambda i: (0, i, 0)),
                   pl.BlockSpec((tm, KV_LORA), lambda i: (i, 0)),
                   pl.BlockSpec((tm, QK_ROPE), lambda i: (i, 0))],
        out_shape=[jax.ShapeDtypeStruct((MLA_HEADS, ms, KV_LORA), BF16),
                   jax.ShapeDtypeStruct((MLA_HEADS, ms, QK_ROPE), BF16),
                   jax.ShapeDtypeStruct((ms, KV_LORA), F32),
                   jax.ShapeDtypeStruct((ms, QK_ROPE), F32)],
        compiler_params=_params(1),
        name="mla_pre_sample",
    )(*[proj] * (N_CQ_BLK + N_CKV_BLK + 1), lw["q_norm_g"], lw["kv_norm_g"], lw["wq_main"], lw["wq_rot"],
      lw["w_ukt"], cos, sin)


ATTN_Q_BLOCK = 512
ATTN_HEADS_PER_STEP = 4


def _attn_prompt_kernel(q_ref, k_ref, v_ref, g_ref, o_ref, *, tq):
    t = q_ref.shape[0]
    rc = lax.broadcasted_iota(I32, (tq, tq), 0) // CHUNK
    cc = lax.broadcasted_iota(I32, (tq, tq), 1) // CHUNK
    visible = cc <= rc
    for h in range(ATTN_HEADS_PER_STEP):
        qk = slice(h * QK_PAD, (h + 1) * QK_PAD)
        hv = slice(h * V_HEAD, (h + 1) * V_HEAD)
        g = g_ref[:, hv]
        for i in range(t // tq):
            lo, hi = i * tq, (i + 1) * tq
            q = q_ref[lo:hi, qk]
            sd = jnp.where(visible, _dot_nt(q, k_ref[lo:hi, qk]), -jnp.inf)
            m = jnp.max(sd, axis=-1, keepdims=True)
            if i:
                so = _dot_nt(q, k_ref[0:lo, qk])
                m = jnp.maximum(m, jnp.max(so, axis=-1, keepdims=True))
            pd = jnp.exp(sd - m)
            l = jnp.sum(pd, axis=-1, keepdims=True)
            o = _dot(pd.astype(BF16), v_ref[lo:hi, hv])
            if i:
                po = jnp.exp(so - m)
                l = l + jnp.sum(po, axis=-1, keepdims=True)
                o = o + _dot(po.astype(BF16), v_ref[0:lo, hv])
            o_ref[lo:hi, hv] = _rms(o / l, g).astype(o_ref.dtype)


def _attn_prompt(q, k, v, g_mix, nb, t):
    tq = _tile(t, ATTN_Q_BLOCK)
    nh = ATTN_HEADS_PER_STEP
    c0 = (CONV_DIM + SGU_DIM) // (nh * HEAD)
    return pl.pallas_call(
        functools.partial(_attn_prompt_kernel, tq=tq),
        grid=(nb, MLA_HEADS // nh),
        in_specs=[pl.BlockSpec((t, nh * QK_PAD), lambda b, h: (b, h)),
                  pl.BlockSpec((t, nh * QK_PAD), lambda b, h: (b, h)),
                  pl.BlockSpec((t, nh * V_HEAD), lambda b, h: (b, h)),
                  pl.BlockSpec((1, nh * HEAD), lambda b, h: (0, c0 + h))],
        out_specs=pl.BlockSpec((t, nh * HEAD), lambda b, h: (b, h)),
        out_shape=jax.ShapeDtypeStruct((nb * t, MLA_HEADS * HEAD), BF16),
        compiler_params=_params(2),
        name="attn_prompt",
    )(q, k, v, g_mix)


def _attn_sample_kernel(ql_ref, qr_ref, cc_ref, ck_ref, nc_ref, nk_ref, wuv_ref, g_ref, o_ref):
    s_len = ql_ref.shape[1]
    ql = ql_ref[...].reshape(MLA_HEADS * s_len, KV_LORA)
    qr = qr_ref[...].reshape(MLA_HEADS * s_len, QK_ROPE)
    cc = cc_ref[0, 0].astype(BF16)
    ck = ck_ref[0, 0].astype(BF16)
    nc = nc_ref[...].astype(BF16)
    nk = nk_ref[...].astype(BF16)
    sp = _dot_nt(ql, cc) + _dot_nt(qr, ck)
    sn = _dot_nt(ql, nc) + _dot_nt(qr, nk)
    m = jnp.maximum(jnp.max(sp, axis=-1, keepdims=True), jnp.max(sn, axis=-1, keepdims=True))
    pp, pn = jnp.exp(sp - m), jnp.exp(sn - m)
    l = jnp.sum(pp, axis=-1, keepdims=True) + jnp.sum(pn, axis=-1, keepdims=True)
    o_lat = ((_dot(pp.astype(BF16), cc) + _dot(pn.astype(BF16), nc)) / l).astype(BF16)
    for h in range(MLA_HEADS):
        oh = _dot(o_lat[h * s_len:(h + 1) * s_len], wuv_ref[h])
        o_ref[:, h * HEAD:(h + 1) * HEAD] = _rms(oh, g_ref[:, h * HEAD:(h + 1) * HEAD]
                                                 ).astype(o_ref.dtype)


def _attn_sample(q_lat, q_rope, cache_ckv, cache_krope, ckv_new, kr_new, wuv_h, g_mix,
                 layer, nb, s_len):
    past = cache_ckv.shape[2]
    full = lambda shape: pl.BlockSpec(shape, lambda b: (0,) * len(shape))
    return pl.pallas_call(
        _attn_sample_kernel,
        grid=(nb,),
        in_specs=[pl.BlockSpec((MLA_HEADS, s_len, KV_LORA), lambda b: (0, b, 0)),
                  pl.BlockSpec((MLA_HEADS, s_len, QK_ROPE), lambda b: (0, b, 0)),
                  pl.BlockSpec((1, 1, past, KV_LORA), lambda b: (layer, b, 0, 0)),
                  pl.BlockSpec((1, 1, past, QK_ROPE), lambda b: (layer, b, 0, 0)),
                  pl.BlockSpec((s_len, KV_LORA), lambda b: (b, 0)),
                  pl.BlockSpec((s_len, QK_ROPE), lambda b: (b, 0)),
                  full((MLA_HEADS, KV_LORA, V_HEAD)),
                  pl.BlockSpec((1, MLA_HEADS * HEAD), lambda b: (0, 1))],
        out_specs=pl.BlockSpec((s_len, MLA_HEADS * HEAD), lambda b: (b, 0)),
        out_shape=jax.ShapeDtypeStruct((nb * s_len, MLA_HEADS * HEAD), BF16),
        compiler_params=_params(1),
        name="attn_sample",
    )(q_lat, q_rope, cache_ckv, cache_krope, ckv_new, kr_new, wuv_h, g_mix)


def _out_proj_kernel(ap_ref, bp_ref, cp_ref, as_ref, bs_ref, cs_ref, w_ref, xp_ref, xs_ref,
                     g_ref, b_ref, op_ref, os_ref, *rest, n_p, packed):
    mx_ref = rest[-1]
    i = pl.program_id(0)
    c1, c2 = CONV_DIM, CONV_DIM + SGU_DIM

    @pl.when(i < n_p)
    def _():
        mx_ref[:, :c1] = ap_ref[...]
        mx_ref[:, c1:c2] = bp_ref[...]
        mx_ref[:, c2:] = cp_ref[...]

    @pl.when(i >= n_p)
    def _():
        mx_ref[:, :c1] = as_ref[...]
        mx_ref[:, c1:c2] = bs_ref[...]
        mx_ref[:, c2:] = cs_ref[...]

    y = ALPHA * _load2(xp_ref, xs_ref, i, n_p) + _dot(mx_ref[...], w_ref[...])
    out = _layernorm(y, g_ref[...], b_ref[...])
    _store2(op_ref, os_ref, i, n_p, out)
    if packed:
        rest[0][...] = _pack_halves(out)
    else:
        _store2(rest[0], rest[1], i, n_p, out)


def _out_proj(mix_p, mix_s, w_o, x_p, x_s, g, b, packed):
    (mp, d), ms = x_p.shape, x_s.shape[0]
    tm = _tile(ms, 256)
    n_p = mp // tm
    m = mp + ms
    full = lambda shape: pl.BlockSpec(shape, lambda i: (0,) * len(shape))
    in_specs = []
    for arrs, clamp in ((mix_p, lambda i: (jnp.minimum(i, n_p - 1), 0)),
                        (mix_s, lambda i: (jnp.maximum(i - n_p, 0), 0))):
        in_specs += [pl.BlockSpec((tm, a.shape[1]), clamp) for a in arrs]
    xp_spec, xs_spec = _two_stream_specs(tm, d, n_p)
    if packed:
        low_specs = [pl.BlockSpec((tm, d // 2), lambda i: (i, 0))]
        low_shapes = [jax.ShapeDtypeStruct((m, d // 2), I32)]
    else:
        low_specs = [xp_spec, xs_spec]
        low_shapes = [jax.ShapeDtypeStruct((mp, d), BF16), jax.ShapeDtypeStruct((ms, d), BF16)]
    return pl.pallas_call(
        functools.partial(_out_proj_kernel, n_p=n_p, packed=packed),
        grid=(m // tm,),
        in_specs=in_specs + [full((d, d)), xp_spec, xs_spec, full((1, d)), full((1, d))],
        out_specs=[xp_spec, xs_spec] + low_specs,
        out_shape=[jax.ShapeDtypeStruct((mp, d), F32), jax.ShapeDtypeStruct((ms, d), F32)]
                  + low_shapes,
        scratch_shapes=[pltpu.VMEM((tm, d), BF16)],
        compiler_params=_params(1),
        name="out_proj_ln",
    )(*mix_p, *mix_s, w_o, x_p, x_s, g, b)


def _bf16_weights(w_ref):
    w = w_ref[...]
    return pltpu.bitcast(w, BF16) if w.dtype == I32 else w


def _swiglu_accumulate(xb_ref, wg_ref, wu_ref, wd_ref, acc_ref):
    xb = xb_ref[...]
    a = _dot(xb, _bf16_weights(wg_ref))
    h = a * (1.0 / (1.0 + jnp.exp(-a))) * _dot(xb, _bf16_weights(wu_ref))
    acc_ref[...] += _dot(h.astype(BF16), _bf16_weights(wd_ref))


def _ffn_kernel(xb_ref, wg_ref, wu_ref, wd_ref, x_ref, g_ref, b_ref, *rest):
    o_ref, ob_ref, acc_ref = rest[-3:]
    f = pl.program_id(1)

    @pl.when(f == 0)
    def _():
        acc_ref[...] = jnp.zeros_like(acc_ref)

    _swiglu_accumulate(xb_ref, wg_ref, wu_ref, wd_ref, acc_ref)

    @pl.when(f == pl.num_programs(1) - 1)
    def _():
        out = _layernorm(ALPHA * x_ref[...] + acc_ref[...], g_ref[...], b_ref[...])
        o_ref[...] = out
        ob_ref[...] = out.astype(BF16)


def _ffn(xb, x, wg, wu, wd, g, b, after=()):
    m, d = x.shape
    ff = wg.shape[1]
    tm, tf = _tile(m, 512), _tile(ff, 512)
    row = lambda i, f: (i, 0)
    vec = pl.BlockSpec((1, d), lambda i, f: (0, 0))
    return pl.pallas_call(
        _ffn_kernel,
        grid=(m // tm, ff // tf),
        in_specs=[pl.BlockSpec((tm, d), row),
                  pl.BlockSpec((d, tf), lambda i, f: (0, f)),
                  pl.BlockSpec((d, tf), lambda i, f: (0, f)),
                  pl.BlockSpec((tf, d), lambda i, f: (f, 0)),
                  pl.BlockSpec((tm, d), row), vec, vec]
                 + [pl.BlockSpec(memory_space=pl.ANY)] * len(after),
        out_specs=[pl.BlockSpec((tm, d), row), pl.BlockSpec((tm, d), row)],
        out_shape=[jax.ShapeDtypeStruct((m, d), F32), jax.ShapeDtypeStruct((m, d), BF16)],
        scratch_shapes=[pltpu.VMEM((tm, d), F32)],
        compiler_params=_params(2),
        name="ffn_dense",
    )(xb, wg, wu, wd, x, g, b, *after)


EXPERT_ROWS = 16
ROW_TILE = 512


def _router_kernel(xp_ref, xs_ref, wh_ref, wl_ref, gate_ref, rank_ref, cnt_ref, carry_ref, *, n_p):
    i = pl.program_id(0)

    @pl.when(i == 0)
    def _():
        carry_ref[...] = jnp.zeros_like(carry_ref)

    x = _load2(xp_ref, xs_ref, i, n_p)
    ts = x.shape[0]
    xh = x.astype(BF16)
    xl = (x - xh.astype(F32)).astype(BF16)
    wh, wl = wh_ref[...], wl_ref[...]
    logits = _dot_nt(wh, xh) + (_dot_nt(wh, xl) + _dot_nt(wl, xh))
    row = lax.broadcasted_iota(I32, logits.shape, 0)
    logits = jnp.where(row < N_EXPERTS, logits, -jnp.inf)
    m1 = jnp.max(logits, axis=0, keepdims=True)
    i1 = jnp.min(jnp.where(logits == m1, row, EXPERT_ROWS), axis=0, keepdims=True)
    rest = jnp.where(row == i1, -jnp.inf, logits)
    m2 = jnp.max(rest, axis=0, keepdims=True)
    i2 = jnp.min(jnp.where(rest == m2, row, EXPERT_ROWS), axis=0, keepdims=True)
    e2 = jnp.exp(m2 - m1)
    g1 = 1.0 / (1.0 + e2)
    g2 = e2 / (1.0 + e2)
    gate_ref[...] = jnp.where(row == i1, g1, jnp.where(row == i2, g2, 0.0))
    sel = jnp.where(row == i1, 1.0, jnp.where(row == i2, 1.0, 0.0))
    src = lax.broadcasted_iota(I32, (ts, ts), 0)
    dst = lax.broadcasted_iota(I32, (ts, ts), 1)
    incl = _dot(sel.astype(BF16), jnp.where(src <= dst, 1.0, 0.0).astype(BF16))
    carry = carry_ref[:, 0:1]
    rank_ref[...] = jnp.where(sel > 0.0, carry + incl - sel, -1.0).astype(I32)
    carry = carry + jnp.sum(sel, axis=1, keepdims=True)
    carry_ref[...] = jnp.broadcast_to(carry, carry_ref.shape)
    cnt_ref[...] = jnp.broadcast_to(carry, carry_ref.shape).astype(I32)


def _router(x_p, x_s, wh, wl):
    (mp, d), ms = x_p.shape, x_s.shape[0]
    m = mp + ms
    ts = _tile(ms, 512)
    n_p = mp // ts
    full = pl.BlockSpec((EXPERT_ROWS, d), lambda i: (0, 0))
    col = pl.BlockSpec((EXPERT_ROWS, ts), lambda i: (0, i))
    xp_spec, xs_spec = _two_stream_specs(ts, d, n_p)
    return pl.pallas_call(
        functools.partial(_router_kernel, n_p=n_p),
        grid=(m // ts,),
        in_specs=[xp_spec, xs_spec, full, full],
        out_specs=[col, col, pl.BlockSpec((EXPERT_ROWS, LANE), lambda i: (0, 0))],
        out_shape=[jax.ShapeDtypeStruct((EXPERT_ROWS, m), F32),
                   jax.ShapeDtypeStruct((EXPERT_ROWS, m), I32),
                   jax.ShapeDtypeStruct((EXPERT_ROWS, LANE), I32)],
        scratch_shapes=[pltpu.VMEM((EXPERT_ROWS, LANE), F32)],
        compiler_params=_params(1),
        name="router",
    )(x_p, x_s, wh, wl)


def _route_plan(gate_t, rank_t, cnt, m, td):
    ne = N_EXPERTS
    total = cnt[:ne, 0]
    padded = (total + td - 1) // td * td
    ends = jnp.cumsum(padded)
    off = ends - padded
    n_act = (ends[-1] // td).astype(I32)
    rt = -(-(2 * m + ne * (td - 1)) // td)
    tiles = jnp.minimum(jnp.arange(rt, dtype=I32), n_act - 1)
    tile_expert = jnp.sum(tiles[:, None] >= (ends // td)[None, :], axis=1).astype(I32)

    rank, gate = rank_t[:ne], gate_t[:ne]
    chosen = rank >= 0
    pos = off[:, None] + rank
    pos_a = jnp.min(jnp.where(chosen, pos, rt * td), axis=0).astype(I32)
    pos_b = jnp.max(jnp.where(chosen, pos, -1), axis=0).astype(I32)
    gate_a = jnp.sum(jnp.where(chosen & (pos == pos_a[None]), gate, 0.0), axis=0)
    gate_b = jnp.sum(jnp.where(chosen & (pos == pos_b[None]), gate, 0.0), axis=0)
    return pos_a, pos_b, gate_a[:, None], gate_b[:, None], tile_expert, n_act.reshape(1), rt


def _sc_mesh():
    return plsc.VectorSubcoreMesh(core_axis_name="c", subcore_axis_name="s")


def _sc_worker_base(per_worker):
    wid = lax.axis_index("s") * SC_CORES + lax.axis_index("c")
    return wid * per_worker


def _sc_scatter_rows(x, idx_a, idx_b, n_rows):
    m, w = x.shape
    per_worker = m // (SC_CORES * SC_SUBCORES)
    assert per_worker * SC_CORES * SC_SUBCORES == m and per_worker % SC_ROWS == 0, m

    @functools.partial(
        pl.kernel, mesh=_sc_mesh(),
        out_type=jax.ShapeDtypeStruct((n_rows, w), x.dtype),
        scratch_types=[pltpu.VMEM((SC_ROWS,), I32), pltpu.VMEM((SC_ROWS,), I32),
                       pltpu.VMEM((SC_ROWS, w), x.dtype),
                       pltpu.SemaphoreType.DMA, pltpu.SemaphoreType.DMA])
    def scatter(x_hbm, ia_hbm, ib_hbm, out_hbm, ia_v, ib_v, rows_v, sem_a, sem_b):
        base = _sc_worker_base(per_worker)

        @pl.loop(0, per_worker // SC_ROWS)
        def _(c):
            lo = pl.multiple_of(base + c * SC_ROWS, 8)
            pltpu.sync_copy(ia_hbm.at[pl.ds(lo, SC_ROWS)], ia_v)
            pltpu.sync_copy(ib_hbm.at[pl.ds(lo, SC_ROWS)], ib_v)
            pltpu.sync_copy(x_hbm.at[pl.ds(lo, SC_ROWS)], rows_v)
            put_a = pltpu.async_copy(rows_v, out_hbm.at[ia_v], sem_a)
            put_b = pltpu.async_copy(rows_v, out_hbm.at[ib_v], sem_b)
            put_a.wait()
            put_b.wait()

    return scatter(x, idx_a, idx_b)


def _sc_pack_bf16_rows(w, block_cols):
    r, c = w.shape
    lanes, unroll, block_rows = SC_LANES, 8, 8
    assert r % (2 * block_rows) == 0 and c % block_cols == 0 and block_cols % (lanes * unroll) == 0

    @functools.partial(
        pl.kernel, mesh=_sc_mesh(),
        out_type=jax.ShapeDtypeStruct((r // 2, c), I32), scratch_types=[],
        compiler_params=pltpu.CompilerParams(needs_layout_passes=False),
        cost_estimate=pl.CostEstimate(flops=r * c, transcendentals=0,
                                      bytes_accessed=r * c * 4 + r * c * 2))
    def pack_rows(w_hbm, o_hbm):
        def body(in_v, out_v):
            @pl.loop(0, block_rows)
            def _(i):
                @pl.loop(0, block_cols, step=lanes * unroll)
                def _(c0):
                    for u in range(unroll):
                        cs = pl.ds(c0 + u * lanes, lanes)
                        pair = plsc.pack(in_v[2 * i, cs], in_v[2 * i + 1, cs],
                                         format=plsc.PackFormat.INTERLEAVED)
                        out_v[i, cs] = plsc.bitcast(pair, I32)

        pltpu.emit_pipeline(
            body, grid=(r // (2 * block_rows), c // block_cols),
            in_specs=[pl.BlockSpec((2 * block_rows, block_cols), lambda i, j: (i, j))],
            out_specs=[pl.BlockSpec((block_rows, block_cols), lambda i, j: (i, j))],
            core_axis_name=("c", "s"),
            dimension_semantics=(pltpu.PARALLEL, pltpu.PARALLEL))(w_hbm, o_hbm)

    return pack_rows(w)


def _pack_expert_weights(w, block_cols):
    e, r, c = w.shape
    return _sc_pack_bf16_rows(w.reshape(e * r, c), block_cols).reshape(e, r // 2, c)


def _sc_gather_rows(table, idx):
    m, w = idx.shape[0], table.shape[1]
    per_worker = m // (SC_CORES * SC_SUBCORES)
    assert per_worker * SC_CORES * SC_SUBCORES == m and per_worker % SC_ROWS == 0, m

    @functools.partial(
        pl.kernel, mesh=_sc_mesh(),
        out_type=jax.ShapeDtypeStruct((m, w), table.dtype),
        scratch_types=[pltpu.VMEM((SC_ROWS,), I32), pltpu.VMEM((SC_ROWS, w), table.dtype),
                       pltpu.SemaphoreType.DMA])
    def gather(table_hbm, idx_hbm, out_hbm, idx_v, rows_v, sem):
        base = _sc_worker_base(per_worker)

        @pl.loop(0, per_worker // SC_ROWS)
        def _(c):
            lo = pl.multiple_of(base + c * SC_ROWS, 8)
            pltpu.sync_copy(idx_hbm.at[pl.ds(lo, SC_ROWS)], idx_v)
            pltpu.async_copy(table_hbm.at[idx_v], rows_v, sem).wait()
            pltpu.sync_copy(rows_v, out_hbm.at[pl.ds(lo, SC_ROWS)])

    return gather(table, idx)


def _experts_kernel(te_ref, na_ref, xs_ref, wg_ref, wu_ref, wd_ref, o_ref, xb_ref, acc_ref):
    del te_ref
    j, f = pl.program_id(0), pl.program_id(1)
    half = xs_ref.shape[1]

    @pl.when(j < na_ref[0])
    def _():
        @pl.when(f == 0)
        def _():
            lo, hi = _unpack_halves(xs_ref[...])
            xb_ref[:, :half] = lo.astype(BF16)
            xb_ref[:, half:] = hi.astype(BF16)
            acc_ref[...] = jnp.zeros_like(acc_ref)

        _swiglu_accumulate(xb_ref, wg_ref.at[0], wu_ref.at[0], wd_ref.at[0], acc_ref)

        @pl.when(f == pl.num_programs(1) - 1)
        def _():
            o_ref[...] = _pack_halves(acc_ref[...])


def _experts(tile_expert, n_act, xs, wg, wu, wd, td):
    rows, half = xs.shape
    d = 2 * half
    ff = wg.shape[2]
    tf = _tile(ff, 512)
    nf = ff // tf
    row = lambda j, f, te, na: (jnp.minimum(j, na[0] - 1), 0)
    fcol = lambda j, f, na: jnp.where(j < na[0], f, nf - 1)
    return pl.pallas_call(
        _experts_kernel,
        grid_spec=pltpu.PrefetchScalarGridSpec(
            num_scalar_prefetch=2,
            grid=(rows // td, nf),
            in_specs=[pl.BlockSpec((td, half), row),
                      pl.BlockSpec((1, half, tf), lambda j, f, te, na: (te[j], 0, fcol(j, f, na))),
                      pl.BlockSpec((1, half, tf), lambda j, f, te, na: (te[j], 0, fcol(j, f, na))),
                      pl.BlockSpec((1, tf // 2, d),
                                   lambda j, f, te, na: (te[j], fcol(j, f, na), 0))],
            out_specs=pl.BlockSpec((td, half), row),
            scratch_shapes=[pltpu.VMEM((td, d), BF16), pltpu.VMEM((td, d), F32)]),
        out_shape=jax.ShapeDtypeStruct((rows, half), I32),
        compiler_params=_params(2),
        name="moe_experts",
    )(tile_expert, n_act, xs, wg, wu, wd)


def _moe_finish_kernel(oa_ref, ob_ref, ga_ref, gb_ref, xp_ref, xs_ref, g_ref, b_ref,
                       op_ref, os_ref, *, n_p):
    i = pl.program_id(0)
    a_lo, a_hi = _unpack_halves(oa_ref[...])
    b_lo, b_hi = _unpack_halves(ob_ref[...])
    ga, gb = ga_ref[...], gb_ref[...]
    y = jnp.concatenate([ga * a_lo + gb * b_lo, ga * a_hi + gb * b_hi], axis=1)
    out = _layernorm(ALPHA * _load2(xp_ref, xs_ref, i, n_p) + y, g_ref[...], b_ref[...])
    _store2(op_ref, os_ref, i, n_p, out)


def _moe_finish(o_a, o_b, gate_a, gate_b, x_p, x_s, g, b):
    (mp, d), ms = x_p.shape, x_s.shape[0]
    m = mp + ms
    tm = _tile(ms, 512)
    n_p = mp // tm
    row = lambda i: (i, 0)
    vec = pl.BlockSpec((1, d), lambda i: (0, 0))
    xp_spec, xs_spec = _two_stream_specs(tm, d, n_p)
    return pl.pallas_call(
        functools.partial(_moe_finish_kernel, n_p=n_p),
        grid=(m // tm,),
        in_specs=[pl.BlockSpec((tm, d // 2), row), pl.BlockSpec((tm, d // 2), row),
                  pl.BlockSpec((tm, 1), row), pl.BlockSpec((tm, 1), row),
                  xp_spec, xs_spec, vec, vec],
        out_specs=[xp_spec, xs_spec],
        out_shape=[jax.ShapeDtypeStruct((mp, d), F32), jax.ShapeDtypeStruct((ms, d), F32)],
        compiler_params=_params(1),
        name="moe_finish",
    )(o_a, o_b, gate_a, gate_b, x_p, x_s, g, b)


def _pack_moe_weights(w_gate, w_up, w_down):
    return (_pack_expert_weights(w_gate, w_gate.shape[2] // 4),
            _pack_expert_weights(w_up, w_up.shape[2] // 4),
            _pack_expert_weights(w_down, w_down.shape[2] // 2))


def _moe(x_p, x_s, x_packed, router_w, wg, wu, wd, g, b):
    m = x_p.shape[0] + x_s.shape[0]
    rw = jnp.pad(jnp.transpose(router_w), ((0, EXPERT_ROWS - N_EXPERTS), (0, 0)))
    rwh = rw.astype(BF16)
    rwl = (rw - rwh.astype(F32)).astype(BF16)
    gate_t, rank_t, cnt = _router(x_p, x_s, rwh, rwl)
    td = min(ROW_TILE, _tile(m, ROW_TILE))
    pos_a, pos_b, gate_a, gate_b, tile_expert, n_act, rt = _route_plan(gate_t, rank_t, cnt, m, td)
    xs = _sc_scatter_rows(x_packed, pos_a, pos_b, rt * td)
    o_sorted = _experts(tile_expert, n_act, xs, wg, wu, wd, td)
    o_a = _sc_gather_rows(o_sorted, pos_a)
    o_b = _sc_gather_rows(o_sorted, pos_b)
    return _moe_finish(o_a, o_b, gate_a, gate_b, x_p, x_s, g, b)


def _rot_cols(w):
    half = QK_ROPE // 2
    return jnp.concatenate([-w[..., half:], w[..., :half]], axis=-1)


def _layer_weights(l, w_in, conv_w, sgu_ln_g, sgu_ln_b, sgu_w, sgu_b, q_norm_g, w_uq, kv_norm_g,
                   w_uk, w_uv, mix_norm_g, w_o):
    d = w_in.shape[1]
    wi = w_in[l]
    k_r = wi[:, COL_KR:COL_KR + QK_ROPE]
    w_in_pad = jnp.concatenate(
        [wi.astype(BF16), _rot_cols(k_r).astype(BF16),
         jnp.zeros((d, D_IN_PAD - COL_KR - 2 * QK_ROPE), BF16)], axis=1)
    uq = w_uq[l]
    zeros = jnp.zeros((Q_LORA, MLA_HEADS, QK_PAD - QK_NOPE - QK_ROPE), F32)
    wq_main = jnp.concatenate([uq, zeros], axis=-1).reshape(Q_LORA, MLA_HEADS * QK_PAD)
    wq_rot = jnp.concatenate([_rot_cols(uq[..., QK_NOPE:]), zeros], axis=-1
                             ).reshape(Q_LORA, MLA_HEADS * LANE)
    return {
        "w_in": w_in_pad,
        "conv_w": conv_w[l],
        "sgu_w": sgu_w[l],
        "sgu_bt": jnp.transpose(sgu_b[l]),
        "sgu_ln_g": sgu_ln_g[l][None], "sgu_ln_b": sgu_ln_b[l][None],
        "q_norm_g": q_norm_g[l][None], "kv_norm_g": kv_norm_g[l][None],
        "wq_main": wq_main.astype(BF16), "wq_rot": wq_rot.astype(BF16),
        "w_uk": w_uk[l].reshape(KV_LORA, MLA_HEADS * QK_NOPE).astype(BF16),
        "w_uv": w_uv[l].reshape(KV_LORA, MLA_HEADS * V_HEAD).astype(BF16),
        "w_ukt": jnp.transpose(w_uk[l], (1, 2, 0)).astype(BF16),
        "w_uv_h": jnp.transpose(w_uv[l], (1, 0, 2)).astype(BF16),
        "g_mix": mix_norm_g[l][None],
        "w_o": w_o[l].astype(BF16),
    }


def _rope_tables(pos):
    half = QK_ROPE // 2
    inv = ROPE_THETA ** (-jnp.arange(half, dtype=F32) / half)
    ang = pos.astype(F32)[:, None] * inv[None, :]
    zeros = jnp.zeros((pos.shape[0], LANE - QK_ROPE), F32)
    cos = jnp.concatenate([jnp.cos(ang), jnp.cos(ang), zeros], axis=-1)
    sin = jnp.concatenate([jnp.sin(ang), jnp.sin(ang), zeros], axis=-1)
    return cos, sin


def kernel(x_prompt, x_sample, state_conv, cache_ckv, cache_krope, w_in, conv_w, sgu_ln_g,
           sgu_ln_b, sgu_w, sgu_b, q_norm_g, w_uq, kv_norm_g, w_uk, w_uv, mix_norm_g, w_o,
           ln1_g, ln1_b, ln2_g, ln2_b, ffn_w_gate, ffn_w_up, ffn_w_down, router_w,
           moe_w_gate, moe_w_up, moe_w_down):
    nbp, t, d = x_prompt.shape
    nbs, s_len, _ = x_sample.shape
    past = cache_ckv.shape[2]
    depth = w_in.shape[0]
    mp, ms = nbp * t, nbs * s_len

    x_p, x_s = x_prompt.reshape(mp, d), x_sample.reshape(ms, d)
    xb_p, xb_s = x_p, x_s

    cos_p, sin_p = _rope_tables(jnp.arange(t))
    tms = _tile(ms, 256)
    cos_s, sin_s = _rope_tables(past + jnp.arange(s_len))
    cos_s, sin_s = jnp.tile(cos_s, (tms // s_len, 1)), jnp.tile(sin_s, (tms // s_len, 1))
    zero_state = jnp.zeros((nbp, CONV_WIDTH - 1, CONV_DIM), F32)

    outs = [[] for _ in range(7)]
    packed_experts = None
    for l in range(depth):
        lw = _layer_weights(l, w_in, conv_w, sgu_ln_g, sgu_ln_b, sgu_w, sgu_b, q_norm_g, w_uq,
                            kv_norm_g, w_uk, w_uv, mix_norm_g, w_o)
        routed = l % 2 == 1
        proj = _in_proj(xb_p, xb_s, lw["w_in"])

        a_p, conv_p = _conv_mixer(proj, zero_state, lw["conv_w"], lw["g_mix"], nbp, t, 0)
        a_s, conv_s = _conv_mixer(proj, state_conv[l], lw["conv_w"], lw["g_mix"], nbs, s_len, mp)
        sgu_args = (lw["sgu_w"], lw["sgu_bt"], lw["sgu_ln_g"], lw["sgu_ln_b"], lw["g_mix"])
        chunk_p = min(MLP_CHUNK, t)
        (b_p,) = _sgu_mixer(proj, *sgu_args, mp, 0, chunk_p,
                            _tile(t, 512) if t >= 512 else chunk_p, False)
        chunk_s = min(MLP_CHUNK, s_len)
        b_s, v_rows = _sgu_mixer(proj, *sgu_args, ms, mp, chunk_s, chunk_s, True)
        q, k, v, ckv_p, kr_p = _mla_pre_prompt(proj, lw, cos_p, sin_p, mp, t)
        c_p = _attn_prompt(q, k, v, lw["g_mix"], nbp, t)
        q_lat, q_rope, ckv_s, kr_s = _mla_pre_sample(proj, lw, cos_s, sin_s, mp, ms, tms)
        c_s = _attn_sample(q_lat, q_rope, cache_ckv, cache_krope, ckv_s, kr_s, lw["w_uv_h"],
                           lw["g_mix"], l, nbs, s_len)

        x_p, x_s, *low = _out_proj((a_p, b_p, c_p), (a_s, b_s, c_s), lw["w_o"], x_p, x_s,
                                   ln1_g[l][None], ln1_b[l][None], routed)

        i = l // 2
        if routed:
            if packed_experts is None:
                packed_experts = _pack_moe_weights(moe_w_gate[i], moe_w_up[i], moe_w_down[i])
            x_p, x_s = _moe(x_p, x_s, low[0], router_w[i], *packed_experts,
                            ln2_g[l][None], ln2_b[l][None])
            packed_experts = None
            xb_p, xb_s = x_p, x_s
        else:
            ffn_w = (ffn_w_gate[i].astype(BF16), ffn_w_up[i].astype(BF16),
                     ffn_w_down[i].astype(BF16), ln2_g[l][None], ln2_b[l][None])
            x_p, xb_p = _ffn(low[0], x_p, *ffn_w)
            if l + 1 < depth:
                nxt = (l + 1) // 2
                packed_experts = _pack_moe_weights(moe_w_gate[nxt], moe_w_up[nxt],
                                                   moe_w_down[nxt])
            x_s, xb_s = _ffn(low[1], x_s, *ffn_w, after=packed_experts or ())

        for lst, val in zip(outs, (conv_p, ckv_p.reshape(nbp, t, KV_LORA),
                                   kr_p.reshape(nbp, t, QK_ROPE), conv_s,
                                   ckv_s.reshape(nbs, s_len, KV_LORA),
                                   kr_s.reshape(nbs, s_len, QK_ROPE),
                                   v_rows.reshape(nbs, s_len, SGU_DIM))):
            lst.append(val)

    return (x_p.reshape(nbp, t, d), x_s.reshape(nbs, s_len, d), *[jnp.stack(o) for o in outs])
```

```python
import functools

import numpy as np
import jax
import jax.numpy as jnp
from jax import lax
from jax.experimental import pallas as pl
from jax.experimental.pallas import tpu as pltpu
from jax.experimental.pallas import tpu_sc as plsc

F32 = jnp.float32
BF16 = jnp.bfloat16
I32 = jnp.int32

D_MODEL = 2048
CONV_DIM = 512
CONV_WIDTH = 3
SGU_DIM = 512
SGU_HEADS = 4
MLP_CHUNK = 128
MLA_HEADS = 8
QK_NOPE = 128
QK_ROPE = 64
V_HEAD = 128
Q_LORA = 768
KV_LORA = 512
ROPE_THETA = 10000.0
CHUNK = 64
HEAD = 128
N_EXPERTS = 8
DEPTH = 2
SM_SCALE = (QK_NOPE + QK_ROPE) ** -0.5
ALPHA = (2 * DEPTH) ** 0.25

COL_BG, COL_CG, COL_HC, COL_U, COL_V = 0, 512, 1024, 1536, 2048
COL_CQ, COL_CKV, COL_KR = 2560, 3328, 3840
D_IN_PAD = 4096
LORA_BLK = 256
N_CQ_BLK, N_CKV_BLK = Q_LORA // LORA_BLK, KV_LORA // LORA_BLK
QK_PAD = 256

LANE = 128
VMEM_LIMIT = 56 * 1024 * 1024
SC_CORES, SC_SUBCORES = 2, 16
SC_ROWS = 32
SC_LANES = 16


def _params(n_axes):
    return pltpu.CompilerParams(dimension_semantics=("arbitrary",) * n_axes,
                                vmem_limit_bytes=VMEM_LIMIT)


def _tile(n, pref):
    t = pref
    while n % t:
        t //= 2
    assert t >= 8, (n, pref)
    return t


def _rms(x, g, eps=1e-6):
    return x * lax.rsqrt(jnp.mean(x * x, axis=-1, keepdims=True) + eps) * g


def _layernorm(y, g, b, eps=1e-5):
    yc = y - jnp.mean(y, axis=-1, keepdims=True)
    var = jnp.mean(yc * yc, axis=-1, keepdims=True)
    return yc * lax.rsqrt(var + eps) * g + b


def _gelu(x):
    c = np.sqrt(2.0 / np.pi).astype(np.float32)
    return 0.5 * x * (1.0 + jnp.tanh(c * (x + 0.044715 * (x * x * x))))


def _dot(a, b):
    return jnp.dot(a, b, preferred_element_type=F32)


def _dot_nt(a, b):
    return lax.dot_general(a, b, (((1,), (1,)), ((), ())), preferred_element_type=F32)


def _two_stream_specs(tm, width, n_p, single_buffer=False):
    def p_map(i, *_):
        return (jnp.minimum(i, n_p - 1), 0)

    def s_map(i, *_):
        return (jnp.maximum(i - n_p, 0), 0)

    mode = {"pipeline_mode": pl.Buffered(1)} if single_buffer else {}
    return (pl.BlockSpec((tm, width), p_map, **mode), pl.BlockSpec((tm, width), s_map, **mode))


def _load2(p_ref, s_ref, i, n_p):
    return jnp.where(i < n_p, p_ref[...], s_ref[...])


def _store2(p_ref, s_ref, i, n_p, val):
    @pl.when(i < n_p)
    def _():
        p_ref[...] = val.astype(p_ref.dtype)

    @pl.when(i >= n_p)
    def _():
        s_ref[...] = val.astype(s_ref.dtype)


def _pack_halves(y):
    w = y.shape[1] // 2
    lo = lax.bitcast_convert_type(y[:, :w].astype(BF16).astype(F32), jnp.uint32)
    hi = lax.bitcast_convert_type(y[:, w:].astype(BF16).astype(F32), jnp.uint32)
    return lax.bitcast_convert_type(hi | (lo >> 16), I32)


def _unpack_halves(p):
    u = lax.bitcast_convert_type(p, jnp.uint32)
    lo = lax.bitcast_convert_type(u << 16, F32)
    hi = lax.bitcast_convert_type(u & jnp.uint32(0xFFFF0000), F32)
    return lo, hi


def _in_proj_kernel(xp_ref, xs_ref, w_ref, o_ref, xb_ref, *, n_p):
    @pl.when(pl.program_id(1) == 0)
    def _():
        xb_ref[...] = _load2(xp_ref, xs_ref, pl.program_id(0), n_p).astype(BF16)

    o_ref[...] = _dot(xb_ref[...], w_ref[...]).astype(o_ref.dtype)


def _in_proj(x_p, x_s, w):
    (mp, k), ms = x_p.shape, x_s.shape[0]
    m, n = mp + ms, w.shape[1]
    tm, tn = _tile(ms, 1024), _tile(n, 1024)
    n_p = mp // tm
    xp_spec, xs_spec = _two_stream_specs(tm, k, n_p)
    return pl.pallas_call(
        functools.partial(_in_proj_kernel, n_p=n_p),
        grid=(m // tm, n // tn),
        in_specs=[xp_spec, xs_spec, pl.BlockSpec((k, tn), lambda i, j: (0, j))],
        out_specs=pl.BlockSpec((tm, tn), lambda i, j: (i, j)),
        out_shape=jax.ShapeDtypeStruct((m, n), BF16),
        scratch_shapes=[pltpu.VMEM((tm, k), BF16)],
        compiler_params=_params(2),
        name="in_proj",
    )(x_p, x_s, w)


def _conv_kernel(bg_ref, cg_ref, hc_ref, st_ref, w_ref, g_ref, o_ref, ns_ref, *, t):
    rows = bg_ref.shape[0]
    nseq = rows // t
    row = lax.broadcasted_iota(I32, (rows, LANE), 0)
    for c in range(CONV_DIM // LANE):
        cs = slice(c * LANE, (c + 1) * LANE)
        z = cg_ref[:, cs].astype(F32) * hc_ref[:, cs].astype(F32)
        z1, z2 = pltpu.roll(z, 1, 0), pltpu.roll(z, 2, 0)
        for s in range(nseq):
            prev2, prev1 = st_ref[s, 0:1, cs], st_ref[s, 1:2, cs]
            z1 = jnp.where(row == s * t, prev1, z1)
            z2 = jnp.where(row == s * t, prev2, jnp.where(row == s * t + 1, prev1, z2))
            ns_ref[s, :, cs] = z[(s + 1) * t - 2:(s + 1) * t, :]
        w = w_ref[:, cs]
        a = bg_ref[:, cs].astype(F32) * (w[0:1] * z2 + w[1:2] * z1 + w[2:3] * z)
        o_ref[:, cs] = _rms(a, g_ref[:, cs]).astype(o_ref.dtype)


def _conv_mixer(proj, state, conv_w, g_mix, nb, t, row0):
    nseq = max(1, min(nb, 256 // t))
    rows = nseq * t
    rb0 = row0 // rows

    def col(c0):
        return pl.BlockSpec((rows, CONV_DIM), lambda b: (rb0 + b, c0 // CONV_DIM))

    state_spec = pl.BlockSpec((nseq, CONV_WIDTH - 1, CONV_DIM), lambda b: (b, 0, 0))
    return pl.pallas_call(
        functools.partial(_conv_kernel, t=t),
        grid=(nb // nseq,),
        in_specs=[col(COL_BG), col(COL_CG), col(COL_HC), state_spec,
                  pl.BlockSpec((CONV_WIDTH, CONV_DIM), lambda b: (0, 0)),
                  pl.BlockSpec((1, CONV_DIM), lambda b: (0, 0))],
        out_specs=[pl.BlockSpec((rows, CONV_DIM), lambda b: (b, 0)), state_spec],
        out_shape=[jax.ShapeDtypeStruct((nb * t, CONV_DIM), BF16),
                   jax.ShapeDtypeStruct((nb, CONV_WIDTH - 1, CONV_DIM), F32)],
        compiler_params=_params(1),
        name="conv_mixer",
    )(proj, proj, proj, state, conv_w, g_mix)


def _sgu_kernel(u_ref, v_ref, w_ref, bt_ref, lg_ref, lb_ref, g_ref, o_ref, *vr_refs, chunk):
    tr = u_ref.shape[0]
    r = lax.broadcasted_iota(I32, (chunk, chunk), 0)
    c = lax.broadcasted_iota(I32, (chunk, chunk), 1)
    for h in range(SGU_HEADS):
        cs = slice(h * HEAD, (h + 1) * HEAD)
        wm = jnp.where(c <= r, w_ref[h][:chunk, :chunk], 0.0).astype(BF16)
        bias = bt_ref[:chunk, h:h + 1]
        for k in range(tr // chunk):
            rs = slice(k * chunk, (k + 1) * chunk)
            vg = _gelu(v_ref[rs, cs].astype(F32))
            xc = vg - jnp.mean(vg, axis=-1, keepdims=True)
            var = jnp.mean(xc * xc, axis=-1, keepdims=True)
            vh = xc * lax.rsqrt(var + 1e-5) * lg_ref[:, cs] + lb_ref[:, cs]
            if vr_refs:
                vr_refs[0][rs, cs] = vh
            s = _dot(wm, vh.astype(BF16)) + bias
            out = _gelu(u_ref[rs, cs].astype(F32)) * s
            o_ref[rs, cs] = _rms(out, g_ref[:, cs]).astype(o_ref.dtype)


def _sgu_mixer(proj, sgu_w, sgu_bt, ln_g, ln_b, g_mix, nrows, row0, chunk, tr, want_v):
    rb0 = row0 // tr
    full = lambda shape: pl.BlockSpec(shape, lambda i: (0,) * len(shape))
    out_specs = [pl.BlockSpec((tr, SGU_DIM), lambda i: (i, 0))]
    out_shape = [jax.ShapeDtypeStruct((nrows, SGU_DIM), BF16)]
    if want_v:
        out_specs.append(pl.BlockSpec((tr, SGU_DIM), lambda i: (i, 0)))
        out_shape.append(jax.ShapeDtypeStruct((nrows, SGU_DIM), F32))
    return pl.pallas_call(
        functools.partial(_sgu_kernel, chunk=chunk),
        grid=(nrows // tr,),
        in_specs=[pl.BlockSpec((tr, SGU_DIM), lambda i: (rb0 + i, COL_U // SGU_DIM)),
                  pl.BlockSpec((tr, SGU_DIM), lambda i: (rb0 + i, COL_V // SGU_DIM)),
                  full((SGU_HEADS, MLP_CHUNK, MLP_CHUNK)),
                  full((MLP_CHUNK, SGU_HEADS)),
                  full((1, SGU_DIM)), full((1, SGU_DIM)),
                  pl.BlockSpec((1, SGU_DIM), lambda i: (0, 1))],
        out_specs=out_specs,
        out_shape=out_shape,
        compiler_params=_params(1),
        name="sgu_mixer",
    )(proj, proj, sgu_w, sgu_bt, ln_g, ln_b, g_mix)


def _rope_q(qm, cos, sin, h):
    nope = qm[:, h * QK_PAD:h * QK_PAD + QK_NOPE]
    x = qm[:, h * QK_PAD + QK_NOPE:(h + 1) * QK_PAD]
    half = QK_ROPE // 2
    lane = lax.broadcasted_iota(I32, x.shape, 1)
    rot = jnp.where(lane < half, -pltpu.roll(x, LANE - half, 1), pltpu.roll(x, half, 1))
    return nope * SM_SCALE, (x * cos + rot * sin) * SM_SCALE


def _rope_k(kr_ref, cos, sin):
    blk = kr_ref[...].astype(F32)
    return blk * cos + pltpu.roll(blk, QK_ROPE, 1) * sin


def _lora_inputs(refs):
    cq = jnp.concatenate([r[...] for r in refs[:N_CQ_BLK]], axis=1).astype(F32)
    ckv = jnp.concatenate([r[...] for r in refs[N_CQ_BLK:N_CQ_BLK + N_CKV_BLK]], axis=1
                          ).astype(F32)
    return cq, ckv, refs[N_CQ_BLK + N_CKV_BLK:]


def _lora_specs(tm, rb0):
    def blk(c0, k):
        return pl.BlockSpec((tm, LORA_BLK), lambda i: (rb0 + i, c0 // LORA_BLK + k))

    return ([blk(COL_CQ, k) for k in range(N_CQ_BLK)]
            + [blk(COL_CKV, k) for k in range(N_CKV_BLK)]
            + [pl.BlockSpec((tm, LANE), lambda i: (rb0 + i, COL_KR // LANE))])


def _mla_pre_prompt_kernel(*refs):
    cq, ckv_in, refs = _lora_inputs(refs)
    (kr_ref, qg_ref, kvg_ref, wqm_ref, wuk_ref, wuv_ref, cos_ref, sin_ref,
     q_ref, k_ref, v_ref, ckv_out_ref, kr_out_ref) = refs
    cos, sin = cos_ref[...], sin_ref[...]
    cqn = _rms(cq, qg_ref[...]).astype(BF16)
    qm = _dot(cqn, wqm_ref[...])
    ckv = _rms(ckv_in, kvg_ref[...])
    ckv_out_ref[...] = ckv
    ckb = ckv.astype(BF16)
    kn = _dot(ckb, wuk_ref[...])
    v_ref[...] = _dot(ckb, wuv_ref[...]).astype(v_ref.dtype)
    krp = _rope_k(kr_ref, cos, sin)
    kr_out_ref[...] = krp[:, :QK_ROPE]
    krb = krp.astype(BF16)
    for h in range(MLA_HEADS):
        nope, rope = _rope_q(qm, cos, sin, h)
        q_ref[:, h * QK_PAD:h * QK_PAD + QK_NOPE] = nope.astype(BF16)
        q_ref[:, h * QK_PAD + QK_NOPE:(h + 1) * QK_PAD] = rope.astype(BF16)
        k_ref[:, h * QK_PAD:h * QK_PAD + QK_NOPE] = kn[:, h * LANE:(h + 1) * LANE].astype(BF16)
        k_ref[:, h * QK_PAD + QK_NOPE:(h + 1) * QK_PAD] = krb


def _mla_pre_prompt(proj, lw, cos, sin, mp, t):
    tm = _tile(t, 512)
    nt = t // tm
    full = lambda shape: pl.BlockSpec(shape, lambda i: (0,) * len(shape))
    hq, hv = MLA_HEADS * QK_PAD, MLA_HEADS * V_HEAD
    return pl.pallas_call(
        _mla_pre_prompt_kernel,
        grid=(mp // tm,),
        in_specs=_lora_specs(tm, 0) + [
                  full((1, Q_LORA)), full((1, KV_LORA)),
                  full((Q_LORA, hq)),
                  full((KV_LORA, hv)), full((KV_LORA, hv)),
                  pl.BlockSpec((tm, LANE), lambda i: (i % nt, 0)),
                  pl.BlockSpec((tm, LANE), lambda i: (i % nt, 0))],
        out_specs=[pl.BlockSpec((tm, hq), lambda i: (i, 0)),
                   pl.BlockSpec((tm, hq), lambda i: (i, 0)),
                   pl.BlockSpec((tm, hv), lambda i: (i, 0)),
                   pl.BlockSpec((tm, KV_LORA), lambda i: (i, 0)),
                   pl.BlockSpec((tm, QK_ROPE), lambda i: (i, 0))],
        out_shape=[jax.ShapeDtypeStruct((mp, hq), BF16),
                   jax.ShapeDtypeStruct((mp, hq), BF16),
                   jax.ShapeDtypeStruct((mp, hv), BF16),
                   jax.ShapeDtypeStruct((mp, KV_LORA), F32),
                   jax.ShapeDtypeStruct((mp, QK_ROPE), F32)],
        compiler_params=_params(1),
        name="mla_pre_prompt",
    )(*[proj] * (N_CQ_BLK + N_CKV_BLK + 1), lw["q_norm_g"], lw["kv_norm_g"], lw["wq_main"],
      lw["w_uk"], lw["w_uv"], cos, sin)


def _mla_pre_sample_kernel(*refs):
    cq, ckv_in, refs = _lora_inputs(refs)
    (kr_ref, qg_ref, kvg_ref, wqm_ref, wukt_ref, cos_ref, sin_ref,
     ql_ref, qr_ref, ckv_out_ref, kr_out_ref) = refs
    cos, sin = cos_ref[...], sin_ref[...]
    cqn = _rms(cq, qg_ref[...]).astype(BF16)
    qm = _dot(cqn, wqm_ref[...])
    ckv_out_ref[...] = _rms(ckv_in, kvg_ref[...])
    kr_out_ref[...] = _rope_k(kr_ref, cos, sin)[:, :QK_ROPE]
    for h in range(MLA_HEADS):
        nope, rope = _rope_q(qm, cos, sin, h)
        ql_ref[h] = _dot(nope.astype(BF16), wukt_ref[h]).astype(BF16)
        qr_ref[h] = rope[:, :QK_ROPE].astype(BF16)


def _mla_pre_sample(proj, lw, cos, sin, mp, ms, tm):
    rb0 = mp // tm
    full = lambda shape: pl.BlockSpec(shape, lambda i: (0,) * len(shape))
    return pl.pallas_call(
        _mla_pre_sample_kernel,
        grid=(ms // tm,),
        in_specs=_lora_specs(tm, rb0) + [
                  full((1, Q_LORA)), full((1, KV_LORA)),
                  full((Q_LORA, MLA_HEADS * QK_PAD)),
                  full((MLA_HEADS, QK_NOPE, KV_LORA)),
                  full((tm, LANE)), full((tm, LANE))],
        out_specs=[pl.BlockSpec((MLA_HEADS, tm, KV_LORA), lambda i: (0, i, 0)),
                   pl.BlockSpec((MLA_HEADS, tm, QK_ROPE), lambda i: (0, i, 0)),
                   pl.BlockSpec((tm, KV_LORA), lambda i: (i, 0)),
                   pl.BlockSpec((tm, QK_ROPE), lambda i: (i, 0))],
        out_shape=[jax.ShapeDtypeStruct((MLA_HEADS, ms, KV_LORA), BF16),
                   jax.ShapeDtypeStruct((MLA_HEADS, ms, QK_ROPE), BF16),
                   jax.ShapeDtypeStruct((ms, KV_LORA), F32),
                   jax.ShapeDtypeStruct((ms, QK_ROPE), F32)],
        compiler_params=_params(1),
        name="mla_pre_sample",
    )(*[proj] * (N_CQ_BLK + N_CKV_BLK + 1), lw["q_norm_g"], lw["kv_norm_g"], lw["wq_main"],
      lw["w_ukt"], cos, sin)


ATTN_Q_BLOCK = 512
ATTN_HEADS_PER_STEP = 4


def _attn_prompt_kernel(q_ref, k_ref, v_ref, g_ref, o_ref, *, tq):
    t = q_ref.shape[0]
    rc = lax.broadcasted_iota(I32, (tq, tq), 0) // CHUNK
    cc = lax.broadcasted_iota(I32, (tq, tq), 1) // CHUNK
    visible = cc <= rc
    for h in range(ATTN_HEADS_PER_STEP):
        qk = slice(h * QK_PAD, (h + 1) * QK_PAD)
        hv = slice(h * V_HEAD, (h + 1) * V_HEAD)
        g = g_ref[:, hv]
        for i in range(t // tq):
            lo, hi = i * tq, (i + 1) * tq
            q = q_ref[lo:hi, qk]
            sd = jnp.where(visible, _dot_nt(q, k_ref[lo:hi, qk]), -jnp.inf)
            m = jnp.max(sd, axis=-1, keepdims=True)
            if i:
                so = _dot_nt(q, k_ref[0:lo, qk])
                m = jnp.maximum(m, jnp.max(so, axis=-1, keepdims=True))
            pd = jnp.exp(sd - m)
            l = jnp.sum(pd, axis=-1, keepdims=True)
            o = _dot(pd.astype(BF16), v_ref[lo:hi, hv])
            if i:
                po = jnp.exp(so - m)
                l = l + jnp.sum(po, axis=-1, keepdims=True)
                o = o + _dot(po.astype(BF16), v_ref[0:lo, hv])
            o_ref[lo:hi, hv] = _rms(o / l, g).astype(o_ref.dtype)


def _attn_prompt(q, k, v, g_mix, nb, t):
    tq = _tile(t, ATTN_Q_BLOCK)
    nh = ATTN_HEADS_PER_STEP
    c0 = (CONV_DIM + SGU_DIM) // (nh * HEAD)
    return pl.pallas_call(
        functools.partial(_attn_prompt_kernel, tq=tq),
        grid=(nb, MLA_HEADS // nh),
        in_specs=[pl.BlockSpec((t, nh * QK_PAD), lambda b, h: (b, h)),
                  pl.BlockSpec((t, nh * QK_PAD), lambda b, h: (b, h)),
                  pl.BlockSpec((t, nh * V_HEAD), lambda b, h: (b, h)),
                  pl.BlockSpec((1, nh * HEAD), lambda b, h: (0, c0 + h))],
        out_specs=pl.BlockSpec((t, nh * HEAD), lambda b, h: (b, h)),
        out_shape=jax.ShapeDtypeStruct((nb * t, MLA_HEADS * HEAD), BF16),
        compiler_params=_params(2),
        name="attn_prompt",
    )(q, k, v, g_mix)


def _attn_sample_kernel(ql_ref, qr_ref, cc_ref, ck_ref, nc_ref, nk_ref, wuv_ref, g_ref, o_ref):
    s_len = ql_ref.shape[1]
    ql = ql_ref[...].reshape(MLA_HEADS * s_len, KV_LORA)
    qr = qr_ref[...].reshape(MLA_HEADS * s_len, QK_ROPE)
    cc = cc_ref[0, 0].astype(BF16)
    ck = ck_ref[0, 0].astype(BF16)
    nc = nc_ref[...].astype(BF16)
    nk = nk_ref[...].astype(BF16)
    sp = _dot_nt(ql, cc) + _dot_nt(qr, ck)
    sn = _dot_nt(ql, nc) + _dot_nt(qr, nk)
    m = jnp.maximum(jnp.max(sp, axis=-1, keepdims=True), jnp.max(sn, axis=-1, keepdims=True))
    pp, pn = jnp.exp(sp - m), jnp.exp(sn - m)
    l = jnp.sum(pp, axis=-1, keepdims=True) + jnp.sum(pn, axis=-1, keepdims=True)
    o_lat = ((_dot(pp.astype(BF16), cc) + _dot(pn.astype(BF16), nc)) / l).astype(BF16)
    for h in range(MLA_HEADS):
        oh = _dot(o_lat[h * s_len:(h + 1) * s_len], wuv_ref[h])
        o_ref[:, h * HEAD:(h + 1) * HEAD] = _rms(oh, g_ref[:, h * HEAD:(h + 1) * HEAD]
                                                 ).astype(o_ref.dtype)


def _attn_sample(q_lat, q_rope, cache_ckv, cache_krope, ckv_new, kr_new, wuv_h, g_mix,
                 layer, nb, s_len):
    past = cache_ckv.shape[2]
    full = lambda shape: pl.BlockSpec(shape, lambda b: (0,) * len(shape))
    return pl.pallas_call(
        _attn_sample_kernel,
        grid=(nb,),
        in_specs=[pl.BlockSpec((MLA_HEADS, s_len, KV_LORA), lambda b: (0, b, 0)),
                  pl.BlockSpec((MLA_HEADS, s_len, QK_ROPE), lambda b: (0, b, 0)),
                  pl.BlockSpec((1, 1, past, KV_LORA), lambda b: (layer, b, 0, 0)),
                  pl.BlockSpec((1, 1, past, QK_ROPE), lambda b: (layer, b, 0, 0)),
                  pl.BlockSpec((s_len, KV_LORA), lambda b: (b, 0)),
                  pl.BlockSpec((s_len, QK_ROPE), lambda b: (b, 0)),
                  full((MLA_HEADS, KV_LORA, V_HEAD)),
                  pl.BlockSpec((1, MLA_HEADS * HEAD), lambda b: (0, 1))],
        out_specs=pl.BlockSpec((s_len, MLA_HEADS * HEAD), lambda b: (b, 0)),
        out_shape=jax.ShapeDtypeStruct((nb * s_len, MLA_HEADS * HEAD), BF16),
        compiler_params=_params(1),
        name="attn_sample",
    )(q_lat, q_rope, cache_ckv, cache_krope, ckv_new, kr_new, wuv_h, g_mix)


def _out_proj_kernel(ap_ref, bp_ref, cp_ref, as_ref, bs_ref, cs_ref, w_ref, xp_ref, xs_ref,
                     g_ref, b_ref, op_ref, os_ref, *rest, n_p, packed):
    mx_ref = rest[-1]
    i = pl.program_id(0)
    c1, c2 = CONV_DIM, CONV_DIM + SGU_DIM

    @pl.when(i < n_p)
    def _():
        mx_ref[:, :c1] = ap_ref[...]
        mx_ref[:, c1:c2] = bp_ref[...]
        mx_ref[:, c2:] = cp_ref[...]

    @pl.when(i >= n_p)
    def _():
        mx_ref[:, :c1] = as_ref[...]
        mx_ref[:, c1:c2] = bs_ref[...]
        mx_ref[:, c2:] = cs_ref[...]

    y = ALPHA * _load2(xp_ref, xs_ref, i, n_p) + _dot(mx_ref[...], w_ref[...])
    out = _layernorm(y, g_ref[...], b_ref[...])
    _store2(op_ref, os_ref, i, n_p, out)
    if packed:
        rest[0][...] = _pack_halves(out)
    else:
        _store2(rest[0], rest[1], i, n_p, out)


def _out_proj(mix_p, mix_s, w_o, x_p, x_s, g, b, packed):
    (mp, d), ms = x_p.shape, x_s.shape[0]
    tm = _tile(ms, 256)
    n_p = mp // tm
    m = mp + ms
    full = lambda shape: pl.BlockSpec(shape, lambda i: (0,) * len(shape))
    in_specs = []
    for arrs, clamp in ((mix_p, lambda i: (jnp.minimum(i, n_p - 1), 0)),
                        (mix_s, lambda i: (jnp.maximum(i - n_p, 0), 0))):
        in_specs += [pl.BlockSpec((tm, a.shape[1]), clamp) for a in arrs]
    xp_spec, xs_spec = _two_stream_specs(tm, d, n_p)
    if packed:
        low_specs = [pl.BlockSpec((tm, d // 2), lambda i: (i, 0))]
        low_shapes = [jax.ShapeDtypeStruct((m, d // 2), I32)]
    else:
        low_specs = [xp_spec, xs_spec]
        low_shapes = [jax.ShapeDtypeStruct((mp, d), BF16), jax.ShapeDtypeStruct((ms, d), BF16)]
    return pl.pallas_call(
        functools.partial(_out_proj_kernel, n_p=n_p, packed=packed),
        grid=(m // tm,),
        in_specs=in_specs + [full((d, d)), xp_spec, xs_spec, full((1, d)), full((1, d))],
        out_specs=[xp_spec, xs_spec] + low_specs,
        out_shape=[jax.ShapeDtypeStruct((mp, d), F32), jax.ShapeDtypeStruct((ms, d), F32)]
                  + low_shapes,
        scratch_shapes=[pltpu.VMEM((tm, d), BF16)],
        compiler_params=_params(1),
        name="out_proj_ln",
    )(*mix_p, *mix_s, w_o, x_p, x_s, g, b)


def _bf16_weights(w_ref):
    w = w_ref[...]
    return pltpu.bitcast(w, BF16) if w.dtype == I32 else w


def _swiglu_accumulate(xb_ref, wg_ref, wu_ref, wd_ref, acc_ref):
    xb = xb_ref[...]
    a = _dot(xb, _bf16_weights(wg_ref))
    h = a * (1.0 / (1.0 + jnp.exp(-a))) * _dot(xb, _bf16_weights(wu_ref))
    acc_ref[...] += _dot(h.astype(BF16), _bf16_weights(wd_ref))


def _ffn_kernel(xb_ref, wg_ref, wu_ref, wd_ref, x_ref, g_ref, b_ref, *rest):
    o_ref, ob_ref, acc_ref = rest[-3:]
    f = pl.program_id(1)

    @pl.when(f == 0)
    def _():
        acc_ref[...] = jnp.zeros_like(acc_ref)

    _swiglu_accumulate(xb_ref, wg_ref, wu_ref, wd_ref, acc_ref)

    @pl.when(f == pl.num_programs(1) - 1)
    def _():
        out = _layernorm(ALPHA * x_ref[...] + acc_ref[...], g_ref[...], b_ref[...])
        o_ref[...] = out
        ob_ref[...] = out.astype(BF16)


def _ffn(xb, x, wg, wu, wd, g, b, after=()):
    m, d = x.shape
    ff = wg.shape[1]
    tm, tf = _tile(m, 512), _tile(ff, 512)
    row = lambda i, f: (i, 0)
    vec = pl.BlockSpec((1, d), lambda i, f: (0, 0))
    return pl.pallas_call(
        _ffn_kernel,
        grid=(m // tm, ff // tf),
        in_specs=[pl.BlockSpec((tm, d), row),
                  pl.BlockSpec((d, tf), lambda i, f: (0, f)),
                  pl.BlockSpec((d, tf), lambda i, f: (0, f)),
                  pl.BlockSpec((tf, d), lambda i, f: (f, 0)),
                  pl.BlockSpec((tm, d), row), vec, vec]
                 + [pl.BlockSpec(memory_space=pl.ANY)] * len(after),
        out_specs=[pl.BlockSpec((tm, d), row), pl.BlockSpec((tm, d), row)],
        out_shape=[jax.ShapeDtypeStruct((m, d), F32), jax.ShapeDtypeStruct((m, d), BF16)],
        scratch_shapes=[pltpu.VMEM((tm, d), F32)],
        compiler_params=_params(2),
        name="ffn_dense",
    )(xb, wg, wu, wd, x, g, b, *after)


EXPERT_ROWS = 16
ROW_TILE = 512


def _router_kernel(xp_ref, xs_ref, wh_ref, wl_ref, gate_ref, rank_ref, cnt_ref, carry_ref, *, n_p):
    i = pl.program_id(0)

    @pl.when(i == 0)
    def _():
        carry_ref[...] = jnp.zeros_like(carry_ref)

    x = _load2(xp_ref, xs_ref, i, n_p)
    ts = x.shape[0]
    xh = x.astype(BF16)
    xl = (x - xh.astype(F32)).astype(BF16)
    wh, wl = wh_ref[...], wl_ref[...]
    logits = _dot_nt(wh, xh) + (_dot_nt(wh, xl) + _dot_nt(wl, xh))
    row = lax.broadcasted_iota(I32, logits.shape, 0)
    logits = jnp.where(row < N_EXPERTS, logits, -jnp.inf)
    m1 = jnp.max(logits, axis=0, keepdims=True)
    i1 = jnp.min(jnp.where(logits == m1, row, EXPERT_ROWS), axis=0, keepdims=True)
    rest = jnp.where(row == i1, -jnp.inf, logits)
    m2 = jnp.max(rest, axis=0, keepdims=True)
    i2 = jnp.min(jnp.where(rest == m2, row, EXPERT_ROWS), axis=0, keepdims=True)
    e2 = jnp.exp(m2 - m1)
    g1 = 1.0 / (1.0 + e2)
    g2 = e2 / (1.0 + e2)
    gate_ref[...] = jnp.where(row == i1, g1, jnp.where(row == i2, g2, 0.0))
    sel = jnp.where(row == i1, 1.0, jnp.where(row == i2, 1.0, 0.0))
    src = lax.broadcasted_iota(I32, (ts, ts), 0)
    dst = lax.broadcasted_iota(I32, (ts, ts), 1)
    incl = _dot(sel.astype(BF16), jnp.where(src <= dst, 1.0, 0.0).astype(BF16))
    carry = carry_ref[:, 0:1]
    rank_ref[...] = jnp.where(sel > 0.0, carry + incl - sel, -1.0).astype(I32)
    carry = carry + jnp.sum(sel, axis=1, keepdims=True)
    carry_ref[...] = jnp.broadcast_to(carry, carry_ref.shape)
    cnt_ref[...] = jnp.broadcast_to(carry, carry_ref.shape).astype(I32)


def _router(x_p, x_s, wh, wl):
    (mp, d), ms = x_p.shape, x_s.shape[0]
    m = mp + ms
    ts = _tile(ms, 512)
    n_p = mp // ts
    full = pl.BlockSpec((EXPERT_ROWS, d), lambda i: (0, 0))
    col = pl.BlockSpec((EXPERT_ROWS, ts), lambda i: (0, i))
    xp_spec, xs_spec = _two_stream_specs(ts, d, n_p)
    return pl.pallas_call(
        functools.partial(_router_kernel, n_p=n_p),
        grid=(m // ts,),
        in_specs=[xp_spec, xs_spec, full, full],
        out_specs=[col, col, pl.BlockSpec((EXPERT_ROWS, LANE), lambda i: (0, 0))],
        out_shape=[jax.ShapeDtypeStruct((EXPERT_ROWS, m), F32),
                   jax.ShapeDtypeStruct((EXPERT_ROWS, m), I32),
                   jax.ShapeDtypeStruct((EXPERT_ROWS, LANE), I32)],
        scratch_shapes=[pltpu.VMEM((EXPERT_ROWS, LANE), F32)],
        compiler_params=_params(1),
        name="router",
    )(x_p, x_s, wh, wl)


def _route_plan(gate_t, rank_t, cnt, m, td):
    ne = N_EXPERTS
    total = cnt[:ne, 0]
    padded = (total + td - 1) // td * td
    ends = jnp.cumsum(padded)
    off = ends - padded
    n_act = (ends[-1] // td).astype(I32)
    rt = -(-(2 * m + ne * (td - 1)) // td)
    tiles = jnp.minimum(jnp.arange(rt, dtype=I32), n_act - 1)
    tile_expert = jnp.sum(tiles[:, None] >= (ends // td)[None, :], axis=1).astype(I32)

    rank, gate = rank_t[:ne], gate_t[:ne]
    chosen = rank >= 0
    pos = off[:, None] + rank
    pos_a = jnp.min(jnp.where(chosen, pos, rt * td), axis=0).astype(I32)
    pos_b = jnp.max(jnp.where(chosen, pos, -1), axis=0).astype(I32)
    gate_a = jnp.sum(jnp.where(chosen & (pos == pos_a[None]), gate, 0.0), axis=0)
    gate_b = jnp.sum(jnp.where(chosen & (pos == pos_b[None]), gate, 0.0), axis=0)
    return pos_a, pos_b, gate_a[:, None], gate_b[:, None], tile_expert, n_act.reshape(1), rt


def _sc_mesh():
    return plsc.VectorSubcoreMesh(core_axis_name="c", subcore_axis_name="s")


def _sc_worker_base(per_worker):
    wid = lax.axis_index("s") * SC_CORES + lax.axis_index("c")
    return wid * per_worker


def _sc_scatter_rows(x, idx_a, idx_b, n_rows):
    m, w = x.shape
    per_worker = m // (SC_CORES * SC_SUBCORES)
    assert per_worker * SC_CORES * SC_SUBCORES == m and per_worker % SC_ROWS == 0, m

    @functools.partial(
        pl.kernel, mesh=_sc_mesh(),
        out_type=jax.ShapeDtypeStruct((n_rows, w), x.dtype),
        scratch_types=[pltpu.VMEM((SC_ROWS,), I32), pltpu.VMEM((SC_ROWS,), I32),
                       pltpu.VMEM((SC_ROWS, w), x.dtype),
                       pltpu.SemaphoreType.DMA, pltpu.SemaphoreType.DMA])
    def scatter(x_hbm, ia_hbm, ib_hbm, out_hbm, ia_v, ib_v, rows_v, sem_a, sem_b):
        base = _sc_worker_base(per_worker)

        @pl.loop(0, per_worker // SC_ROWS)
        def _(c):
            lo = pl.multiple_of(base + c * SC_ROWS, 8)
            pltpu.sync_copy(ia_hbm.at[pl.ds(lo, SC_ROWS)], ia_v)
            pltpu.sync_copy(ib_hbm.at[pl.ds(lo, SC_ROWS)], ib_v)
            pltpu.sync_copy(x_hbm.at[pl.ds(lo, SC_ROWS)], rows_v)
            put_a = pltpu.async_copy(rows_v, out_hbm.at[ia_v], sem_a)
            put_b = pltpu.async_copy(rows_v, out_hbm.at[ib_v], sem_b)
            put_a.wait()
            put_b.wait()

    return scatter(x, idx_a, idx_b)


def _sc_pack_bf16_rows(w, block_cols):
    r, c = w.shape
    lanes, unroll, block_rows = SC_LANES, 8, 8
    assert r % (2 * block_rows) == 0 and c % block_cols == 0 and block_cols % (lanes * unroll) == 0

    @functools.partial(
        pl.kernel, mesh=_sc_mesh(),
        out_type=jax.ShapeDtypeStruct((r // 2, c), I32), scratch_types=[],
        compiler_params=pltpu.CompilerParams(needs_layout_passes=False),
        cost_estimate=pl.CostEstimate(flops=r * c, transcendentals=0,
                                      bytes_accessed=r * c * 4 + r * c * 2))
    def pack_rows(w_hbm, o_hbm):
        def body(in_v, out_v):
            @pl.loop(0, block_rows)
            def _(i):
                @pl.loop(0, block_cols, step=lanes * unroll)
                def _(c0):
                    for u in range(unroll):
                        cs = pl.ds(c0 + u * lanes, lanes)
                        pair = plsc.pack(in_v[2 * i, cs], in_v[2 * i + 1, cs],
                                         format=plsc.PackFormat.INTERLEAVED)
                        out_v[i, cs] = plsc.bitcast(pair, I32)

        pltpu.emit_pipeline(
            body, grid=(r // (2 * block_rows), c // block_cols),
            in_specs=[pl.BlockSpec((2 * block_rows, block_cols), lambda i, j: (i, j))],
            out_specs=[pl.BlockSpec((block_rows, block_cols), lambda i, j: (i, j))],
            core_axis_name=("c", "s"),
            dimension_semantics=(pltpu.PARALLEL, pltpu.PARALLEL))(w_hbm, o_hbm)

    return pack_rows(w)


def _pack_expert_weights(w, block_cols):
    e, r, c = w.shape
    return _sc_pack_bf16_rows(w.reshape(e * r, c), block_cols).reshape(e, r // 2, c)


def _sc_gather_rows(table, idx):
    m, w = idx.shape[0], table.shape[1]
    per_worker = m // (SC_CORES * SC_SUBCORES)
    assert per_worker * SC_CORES * SC_SUBCORES == m and per_worker % SC_ROWS == 0, m

    @functools.partial(
        pl.kernel, mesh=_sc_mesh(),
        out_type=jax.ShapeDtypeStruct((m, w), table.dtype),
        scratch_types=[pltpu.VMEM((SC_ROWS,), I32), pltpu.VMEM((SC_ROWS, w), table.dtype),
                       pltpu.SemaphoreType.DMA])
    def gather(table_hbm, idx_hbm, out_hbm, idx_v, rows_v, sem):
        base = _sc_worker_base(per_worker)

        @pl.loop(0, per_worker // SC_ROWS)
        def _(c):
            lo = pl.multiple_of(base + c * SC_ROWS, 8)
            pltpu.sync_copy(idx_hbm.at[pl.ds(lo, SC_ROWS)], idx_v)
            pltpu.async_copy(table_hbm.at[idx_v], rows_v, sem).wait()
            pltpu.sync_copy(rows_v, out_hbm.at[pl.ds(lo, SC_ROWS)])

    return gather(table, idx)


def _experts_kernel(te_ref, na_ref, xs_ref, wg_ref, wu_ref, wd_ref, o_ref, xb_ref, acc_ref):
    del te_ref
    j, f = pl.program_id(0), pl.program_id(1)
    half = xs_ref.shape[1]

    @pl.when(j < na_ref[0])
    def _():
        @pl.when(f == 0)
        def _():
            lo, hi = _unpack_halves(xs_ref[...])
            xb_ref[:, :half] = lo.astype(BF16)
            xb_ref[:, half:] = hi.astype(BF16)
            acc_ref[...] = jnp.zeros_like(acc_ref)

        _swiglu_accumulate(xb_ref, wg_ref.at[0], wu_ref.at[0], wd_ref.at[0], acc_ref)

        @pl.when(f == pl.num_programs(1) - 1)
        def _():
            o_ref[...] = _pack_halves(acc_ref[...])


def _experts(tile_expert, n_act, xs, wg, wu, wd, td):
    rows, half = xs.shape
    d = 2 * half
    ff = wg.shape[2]
    tf = _tile(ff, 512)
    nf = ff // tf
    row = lambda j, f, te, na: (jnp.minimum(j, na[0] - 1), 0)
    fcol = lambda j, f, na: jnp.where(j < na[0], f, nf - 1)
    return pl.pallas_call(
        _experts_kernel,
        grid_spec=pltpu.PrefetchScalarGridSpec(
            num_scalar_prefetch=2,
            grid=(rows // td, nf),
            in_specs=[pl.BlockSpec((td, half), row),
                      pl.BlockSpec((1, half, tf), lambda j, f, te, na: (te[j], 0, fcol(j, f, na))),
                      pl.BlockSpec((1, half, tf), lambda j, f, te, na: (te[j], 0, fcol(j, f, na))),
                      pl.BlockSpec((1, tf // 2, d),
                                   lambda j, f, te, na: (te[j], fcol(j, f, na), 0))],
            out_specs=pl.BlockSpec((td, half), row),
            scratch_shapes=[pltpu.VMEM((td, d), BF16), pltpu.VMEM((td, d), F32)]),
        out_shape=jax.ShapeDtypeStruct((rows, half), I32),
        compiler_params=_params(2),
        name="moe_experts",
    )(tile_expert, n_act, xs, wg, wu, wd)


def _moe_finish_kernel(oa_ref, ob_ref, ga_ref, gb_ref, xp_ref, xs_ref, g_ref, b_ref,
                       op_ref, os_ref, *, n_p):
    i = pl.program_id(0)
    a_lo, a_hi = _unpack_halves(oa_ref[...])
    b_lo, b_hi = _unpack_halves(ob_ref[...])
    ga, gb = ga_ref[...], gb_ref[...]
    y = jnp.concatenate([ga * a_lo + gb * b_lo, ga * a_hi + gb * b_hi], axis=1)
    out = _layernorm(ALPHA * _load2(xp_ref, xs_ref, i, n_p) + y, g_ref[...], b_ref[...])
    _store2(op_ref, os_ref, i, n_p, out)


def _moe_finish(o_a, o_b, gate_a, gate_b, x_p, x_s, g, b):
    (mp, d), ms = x_p.shape, x_s.shape[0]
    m = mp + ms
    tm = _tile(ms, 512)
    n_p = mp // tm
    row = lambda i: (i, 0)
    vec = pl.BlockSpec((1, d), lambda i: (0, 0))
    xp_spec, xs_spec = _two_stream_specs(tm, d, n_p)
    return pl.pallas_call(
        functools.partial(_moe_finish_kernel, n_p=n_p),
        grid=(m // tm,),
        in_specs=[pl.BlockSpec((tm, d // 2), row), pl.BlockSpec((tm, d // 2), row),
                  pl.BlockSpec((tm, 1), row), pl.BlockSpec((tm, 1), row),
                  xp_spec, xs_spec, vec, vec],
        out_specs=[xp_spec, xs_spec],
        out_shape=[jax.ShapeDtypeStruct((mp, d), F32), jax.ShapeDtypeStruct((ms, d), F32)],
        compiler_params=_params(1),
        name="moe_finish",
    )(o_a, o_b, gate_a, gate_b, x_p, x_s, g, b)


def _pack_moe_weights(w_gate, w_up, w_down):
    return (_pack_expert_weights(w_gate, w_gate.shape[2] // 4),
            _pack_expert_weights(w_up, w_up.shape[2] // 4),
            _pack_expert_weights(w_down, w_down.shape[2] // 2))


def _moe(x_p, x_s, x_packed, router_w, wg, wu, wd, g, b):
    m = x_p.shape[0] + x_s.shape[0]
    rw = jnp.pad(jnp.transpose(router_w), ((0, EXPERT_ROWS - N_EXPERTS), (0, 0)))
    rwh = rw.astype(BF16)
    rwl = (rw - rwh.astype(F32)).astype(BF16)
    gate_t, rank_t, cnt = _router(x_p, x_s, rwh, rwl)
    td = min(ROW_TILE, _tile(m, ROW_TILE))
    pos_a, pos_b, gate_a, gate_b, tile_expert, n_act, rt = _route_plan(gate_t, rank_t, cnt, m, td)
    xs = _sc_scatter_rows(x_packed, pos_a, pos_b, rt * td)
    o_sorted = _experts(tile_expert, n_act, xs, wg, wu, wd, td)
    o_a = _sc_gather_rows(o_sorted, pos_a)
    o_b = _sc_gather_rows(o_sorted, pos_b)
    return _moe_finish(o_a, o_b, gate_a, gate_b, x_p, x_s, g, b)


def _rot_cols(w):
    half = QK_ROPE // 2
    return jnp.concatenate([-w[..., half:], w[..., :half]], axis=-1)


def _layer_weights(l, w_in, conv_w, sgu_ln_g, sgu_ln_b, sgu_w, sgu_b, q_norm_g, w_uq, kv_norm_g,
                   w_uk, w_uv, mix_norm_g, w_o):
    d = w_in.shape[1]
    wi = w_in[l]
    k_r = wi[:, COL_KR:COL_KR + QK_ROPE]
    w_in_pad = jnp.concatenate(
        [wi.astype(BF16), _rot_cols(k_r).astype(BF16),
         jnp.zeros((d, D_IN_PAD - COL_KR - 2 * QK_ROPE), BF16)], axis=1)
    uq = w_uq[l]
    zeros = jnp.zeros((Q_LORA, MLA_HEADS, QK_PAD - QK_NOPE - QK_ROPE), F32)
    wq_main = jnp.concatenate([uq, zeros], axis=-1).reshape(Q_LORA, MLA_HEADS * QK_PAD)
    return {
        "w_in": w_in_pad,
        "conv_w": conv_w[l],
        "sgu_w": sgu_w[l],
        "sgu_bt": jnp.transpose(sgu_b[l]),
        "sgu_ln_g": sgu_ln_g[l][None], "sgu_ln_b": sgu_ln_b[l][None],
        "q_norm_g": q_norm_g[l][None], "kv_norm_g": kv_norm_g[l][None],
        "wq_main": wq_main.astype(BF16),
        "w_uk": w_uk[l].reshape(KV_LORA, MLA_HEADS * QK_NOPE).astype(BF16),
        "w_uv": w_uv[l].reshape(KV_LORA, MLA_HEADS * V_HEAD).astype(BF16),
        "w_ukt": jnp.transpose(w_uk[l], (1, 2, 0)).astype(BF16),
        "w_uv_h": jnp.transpose(w_uv[l], (1, 0, 2)).astype(BF16),
        "g_mix": mix_norm_g[l][None],
        "w_o": w_o[l].astype(BF16),
    }


def _rope_tables(pos):
    half = QK_ROPE // 2
    inv = ROPE_THETA ** (-jnp.arange(half, dtype=F32) / half)
    ang = pos.astype(F32)[:, None] * inv[None, :]
    zeros = jnp.zeros((pos.shape[0], LANE - QK_ROPE), F32)
    cos = jnp.concatenate([jnp.cos(ang), jnp.cos(ang), zeros], axis=-1)
    sin = jnp.concatenate([jnp.sin(ang), jnp.sin(ang), zeros], axis=-1)
    return cos, sin


def kernel(x_prompt, x_sample, state_conv, cache_ckv, cache_krope, w_in, conv_w, sgu_ln_g,
           sgu_ln_b, sgu_w, sgu_b, q_norm_g, w_uq, kv_norm_g, w_uk, w_uv, mix_norm_g, w_o,
           ln1_g, ln1_b, ln2_g, ln2_b, ffn_w_gate, ffn_w_up, ffn_w_down, router_w,
           moe_w_gate, moe_w_up, moe_w_down):
    nbp, t, d = x_prompt.shape
    nbs, s_len, _ = x_sample.shape
    past = cache_ckv.shape[2]
    depth = w_in.shape[0]
    mp, ms = nbp * t, nbs * s_len

    x_p, x_s = x_prompt.reshape(mp, d), x_sample.reshape(ms, d)
    xb_p, xb_s = x_p, x_s

    cos_p, sin_p = _rope_tables(jnp.arange(t))
    tms = _tile(ms, 256)
    cos_s, sin_s = _rope_tables(past + jnp.arange(s_len))
    cos_s, sin_s = jnp.tile(cos_s, (tms // s_len, 1)), jnp.tile(sin_s, (tms // s_len, 1))
    zero_state = jnp.zeros((nbp, CONV_WIDTH - 1, CONV_DIM), F32)

    outs = [[] for _ in range(7)]
    packed_experts = None
    for l in range(depth):
        lw = _layer_weights(l, w_in, conv_w, sgu_ln_g, sgu_ln_b, sgu_w, sgu_b, q_norm_g, w_uq,
                            kv_norm_g, w_uk, w_uv, mix_norm_g, w_o)
        routed = l % 2 == 1
        proj = _in_proj(xb_p, xb_s, lw["w_in"])

        a_p, conv_p = _conv_mixer(proj, zero_state, lw["conv_w"], lw["g_mix"], nbp, t, 0)
        a_s, conv_s = _conv_mixer(proj, state_conv[l], lw["conv_w"], lw["g_mix"], nbs, s_len, mp)
        sgu_args = (lw["sgu_w"], lw["sgu_bt"], lw["sgu_ln_g"], lw["sgu_ln_b"], lw["g_mix"])
        chunk_p = min(MLP_CHUNK, t)
        (b_p,) = _sgu_mixer(proj, *sgu_args, mp, 0, chunk_p,
                            _tile(t, 512) if t >= 512 else chunk_p, False)
        chunk_s = min(MLP_CHUNK, s_len)
        b_s, v_rows = _sgu_mixer(proj, *sgu_args, ms, mp, chunk_s, chunk_s, True)
        q, k, v, ckv_p, kr_p = _mla_pre_prompt(proj, lw, cos_p, sin_p, mp, t)
        c_p = _attn_prompt(q, k, v, lw["g_mix"], nbp, t)
        q_lat, q_rope, ckv_s, kr_s = _mla_pre_sample(proj, lw, cos_s, sin_s, mp, ms, tms)
        c_s = _attn_sample(q_lat, q_rope, cache_ckv, cache_krope, ckv_s, kr_s, lw["w_uv_h"],
                           lw["g_mix"], l, nbs, s_len)

        x_p, x_s, *low = _out_proj((a_p, b_p, c_p), (a_s, b_s, c_s), lw["w_o"], x_p, x_s,
                                   ln1_g[l][None], ln1_b[l][None], routed)

        i = l // 2
        if routed:
            if packed_experts is None:
                packed_experts = _pack_moe_weights(moe_w_gate[i], moe_w_up[i], moe_w_down[i])
            x_p, x_s = _moe(x_p, x_s, low[0], router_w[i], *packed_experts,
                            ln2_g[l][None], ln2_b[l][None])
            packed_experts = None
            xb_p, xb_s = x_p, x_s
        else:
            ffn_w = (ffn_w_gate[i].astype(BF16), ffn_w_up[i].astype(BF16),
                     ffn_w_down[i].astype(BF16), ln2_g[l][None], ln2_b[l][None])
            x_p, xb_p = _ffn(low[0], x_p, *ffn_w)
            if l + 1 < depth:
                nxt = (l + 1) // 2
                packed_experts = _pack_moe_weights(moe_w_gate[nxt], moe_w_up[nxt],
                                                   moe_w_down[nxt])
            x_s, xb_s = _ffn(low[1], x_s, *ffn_w, after=packed_experts or ())

        for lst, val in zip(outs, (conv_p, ckv_p.reshape(nbp, t, KV_LORA),
                                   kr_p.reshape(nbp, t, QK_ROPE), conv_s,
                                   ckv_s.reshape(nbs, s_len, KV_LORA),
                                   kr_s.reshape(nbs, s_len, QK_ROPE),
                                   v_rows.reshape(nbs, s_len, SGU_DIM))):
            lst.append(val)

    return (x_p.reshape(nbp, t, d), x_s.reshape(nbs, s_len, d), *[jnp.stack(o) for o in outs])
```

```python
import functools

import numpy as np
import jax
import jax.numpy as jnp
from jax import lax
from jax.experimental import pallas as pl
from jax.experimental.pallas import tpu as pltpu
from jax.experimental.pallas import tpu_sc as plsc

F32 = jnp.float32
BF16 = jnp.bfloat16
I32 = jnp.int32

D_MODEL = 2048
CONV_DIM = 512
CONV_WIDTH = 3
SGU_DIM = 512
SGU_HEADS = 4
MLP_CHUNK = 128
MLA_HEADS = 8
QK_NOPE = 128
QK_ROPE = 64
V_HEAD = 128
Q_LORA = 768
KV_LORA = 512
ROPE_THETA = 10000.0
CHUNK = 64
HEAD = 128
N_EXPERTS = 8
DEPTH = 2
SM_SCALE = (QK_NOPE + QK_ROPE) ** -0.5
ALPHA = (2 * DEPTH) ** 0.25

COL_BG, COL_CG, COL_HC, COL_U, COL_V = 0, 512, 1024, 1536, 2048
COL_CQ, COL_CKV, COL_KR = 2560, 3328, 3840
D_IN_PAD = 4096
LORA_BLK = 256
N_CQ_BLK, N_CKV_BLK = Q_LORA // LORA_BLK, KV_LORA // LORA_BLK
QK_PAD = 256

LANE = 128
VMEM_LIMIT = 56 * 1024 * 1024
SC_CORES, SC_SUBCORES = 2, 16
SC_ROWS = 32
SC_LANES = 16


def _params(n_axes):
    return pltpu.CompilerParams(dimension_semantics=("arbitrary",) * n_axes,
                                vmem_limit_bytes=VMEM_LIMIT)


def _tile(n, pref):
    t = pref
    while n % t:
        t //= 2
    assert t >= 8, (n, pref)
    return t


def _rms(x, g, eps=1e-6):
    return x * lax.rsqrt(jnp.mean(x * x, axis=-1, keepdims=True) + eps) * g


def _layernorm(y, g, b, eps=1e-5):
    yc = y - jnp.mean(y, axis=-1, keepdims=True)
    var = jnp.mean(yc * yc, axis=-1, keepdims=True)
    return yc * lax.rsqrt(var + eps) * g + b


def _gelu(x):
    c = np.sqrt(2.0 / np.pi).astype(np.float32)
    return 0.5 * x * (1.0 + jnp.tanh(c * (x + 0.044715 * (x * x * x))))


def _dot(a, b):
    return jnp.dot(a, b, preferred_element_type=F32)


def _dot_nt(a, b):
    return lax.dot_general(a, b, (((1,), (1,)), ((), ())), preferred_element_type=F32)


def _two_stream_specs(tm, width, n_p, single_buffer=False):
    def p_map(i, *_):
        return (jnp.minimum(i, n_p - 1), 0)

    def s_map(i, *_):
        return (jnp.maximum(i - n_p, 0), 0)

    mode = {"pipeline_mode": pl.Buffered(1)} if single_buffer else {}
    return (pl.BlockSpec((tm, width), p_map, **mode), pl.BlockSpec((tm, width), s_map, **mode))


def _load2(p_ref, s_ref, i, n_p):
    return jnp.where(i < n_p, p_ref[...], s_ref[...])


def _store2(p_ref, s_ref, i, n_p, val):
    @pl.when(i < n_p)
    def _():
        p_ref[...] = val.astype(p_ref.dtype)

    @pl.when(i >= n_p)
    def _():
        s_ref[...] = val.astype(s_ref.dtype)


def _pack_halves(y):
    w = y.shape[1] // 2
    lo = lax.bitcast_convert_type(y[:, :w].astype(BF16).astype(F32), jnp.uint32)
    hi = lax.bitcast_convert_type(y[:, w:].astype(BF16).astype(F32), jnp.uint32)
    return lax.bitcast_convert_type(hi | (lo >> 16), I32)


def _unpack_halves(p):
    u = lax.bitcast_convert_type(p, jnp.uint32)
    lo = lax.bitcast_convert_type(u << 16, F32)
    hi = lax.bitcast_convert_type(u & jnp.uint32(0xFFFF0000), F32)
    return lo, hi


def _in_proj_kernel(xp_ref, xs_ref, w_ref, o_ref, xb_ref, *, n_p):
    @pl.when(pl.program_id(1) == 0)
    def _():
        xb_ref[...] = _load2(xp_ref, xs_ref, pl.program_id(0), n_p).astype(BF16)

    o_ref[...] = _dot(xb_ref[...], w_ref[...]).astype(o_ref.dtype)


def _in_proj(x_p, x_s, w):
    (mp, k), ms = x_p.shape, x_s.shape[0]
    m, n = mp + ms, w.shape[1]
    tm, tn = _tile(ms, 1024), _tile(n, 1024)
    n_p = mp // tm
    xp_spec, xs_spec = _two_stream_specs(tm, k, n_p)
    return pl.pallas_call(
        functools.partial(_in_proj_kernel, n_p=n_p),
        grid=(m // tm, n // tn),
        in_specs=[xp_spec, xs_spec, pl.BlockSpec((k, tn), lambda i, j: (0, j))],
        out_specs=pl.BlockSpec((tm, tn), lambda i, j: (i, j)),
        out_shape=jax.ShapeDtypeStruct((m, n), BF16),
        scratch_shapes=[pltpu.VMEM((tm, k), BF16)],
        compiler_params=_params(2),
        name="in_proj",
    )(x_p, x_s, w)


def _conv_kernel(bg_ref, cg_ref, hc_ref, st_ref, w_ref, g_ref, o_ref, ns_ref, *, t):
    rows = bg_ref.shape[0]
    nseq = rows // t
    row = lax.broadcasted_iota(I32, (rows, LANE), 0)
    for c in range(CONV_DIM // LANE):
        cs = slice(c * LANE, (c + 1) * LANE)
        z = cg_ref[:, cs].astype(F32) * hc_ref[:, cs].astype(F32)
        z1, z2 = pltpu.roll(z, 1, 0), pltpu.roll(z, 2, 0)
        for s in range(nseq):
            prev2, prev1 = st_ref[s, 0:1, cs], st_ref[s, 1:2, cs]
            z1 = jnp.where(row == s * t, prev1, z1)
            z2 = jnp.where(row == s * t, prev2, jnp.where(row == s * t + 1, prev1, z2))
            ns_ref[s, :, cs] = z[(s + 1) * t - 2:(s + 1) * t, :]
        w = w_ref[:, cs]
        a = bg_ref[:, cs].astype(F32) * (w[0:1] * z2 + w[1:2] * z1 + w[2:3] * z)
        o_ref[:, cs] = _rms(a, g_ref[:, cs]).astype(o_ref.dtype)


def _conv_mixer(proj, state, conv_w, g_mix, nb, t, row0):
    nseq = max(1, min(nb, 256 // t))
    rows = nseq * t
    rb0 = row0 // rows

    def col(c0):
        return pl.BlockSpec((rows, CONV_DIM), lambda b: (rb0 + b, c0 // CONV_DIM))

    state_spec = pl.BlockSpec((nseq, CONV_WIDTH - 1, CONV_DIM), lambda b: (b, 0, 0))
    return pl.pallas_call(
        functools.partial(_conv_kernel, t=t),
        grid=(nb // nseq,),
        in_specs=[col(COL_BG), col(COL_CG), col(COL_HC), state_spec,
                  pl.BlockSpec((CONV_WIDTH, CONV_DIM), lambda b: (0, 0)),
                  pl.BlockSpec((1, CONV_DIM), lambda b: (0, 0))],
        out_specs=[pl.BlockSpec((rows, CONV_DIM), lambda b: (b, 0)), state_spec],
        out_shape=[jax.ShapeDtypeStruct((nb * t, CONV_DIM), BF16),
                   jax.ShapeDtypeStruct((nb, CONV_WIDTH - 1, CONV_DIM), F32)],
        compiler_params=_params(1),
        name="conv_mixer",
    )(proj, proj, proj, state, conv_w, g_mix)


def _sgu_kernel(u_ref, v_ref, w_ref, bt_ref, lg_ref, lb_ref, g_ref, o_ref, *vr_refs, chunk):
    tr = u_ref.shape[0]
    r = lax.broadcasted_iota(I32, (chunk, chunk), 0)
    c = lax.broadcasted_iota(I32, (chunk, chunk), 1)
    for h in range(SGU_HEADS):
        cs = slice(h * HEAD, (h + 1) * HEAD)
        wm = jnp.where(c <= r, w_ref[h][:chunk, :chunk], 0.0).astype(BF16)
        bias = bt_ref[:chunk, h:h + 1]
        for k in range(tr // chunk):
            rs = slice(k * chunk, (k + 1) * chunk)
            vg = _gelu(v_ref[rs, cs].astype(F32))
            xc = vg - jnp.mean(vg, axis=-1, keepdims=True)
            var = jnp.mean(xc * xc, axis=-1, keepdims=True)
            vh = xc * lax.rsqrt(var + 1e-5) * lg_ref[:, cs] + lb_ref[:, cs]
            if vr_refs:
                vr_refs[0][rs, cs] = vh
            s = _dot(wm, vh.astype(BF16)) + bias
            out = _gelu(u_ref[rs, cs].astype(F32)) * s
            o_ref[rs, cs] = _rms(out, g_ref[:, cs]).astype(o_ref.dtype)


def _sgu_mixer(proj, sgu_w, sgu_bt, ln_g, ln_b, g_mix, nrows, row0, chunk, tr, want_v):
    rb0 = row0 // tr
    full = lambda shape: pl.BlockSpec(shape, lambda i: (0,) * len(shape))
    out_specs = [pl.BlockSpec((tr, SGU_DIM), lambda i: (i, 0))]
    out_shape = [jax.ShapeDtypeStruct((nrows, SGU_DIM), BF16)]
    if want_v:
        out_specs.append(pl.BlockSpec((tr, SGU_DIM), lambda i: (i, 0)))
        out_shape.append(jax.ShapeDtypeStruct((nrows, SGU_DIM), F32))
    return pl.pallas_call(
        functools.partial(_sgu_kernel, chunk=chunk),
        grid=(nrows // tr,),
        in_specs=[pl.BlockSpec((tr, SGU_DIM), lambda i: (rb0 + i, COL_U // SGU_DIM)),
                  pl.BlockSpec((tr, SGU_DIM), lambda i: (rb0 + i, COL_V // SGU_DIM)),
                  full((SGU_HEADS, MLP_CHUNK, MLP_CHUNK)),
                  full((MLP_CHUNK, SGU_HEADS)),
                  full((1, SGU_DIM)), full((1, SGU_DIM)),
                  pl.BlockSpec((1, SGU_DIM), lambda i: (0, 1))],
        out_specs=out_specs,
        out_shape=out_shape,
        compiler_params=_params(1),
        name="sgu_mixer",
    )(proj, proj, sgu_w, sgu_bt, ln_g, ln_b, g_mix)


def _rope_q(qm, cos, sin, h):
    nope = qm[:, h * QK_PAD:h * QK_PAD + QK_NOPE]
    x = qm[:, h * QK_PAD + QK_NOPE:(h + 1) * QK_PAD]
    half = QK_ROPE // 2
    lane = lax.broadcasted_iota(I32, x.shape, 1)
    rot = jnp.where(lane < half, -pltpu.roll(x, LANE - half, 1), pltpu.roll(x, half, 1))
    return nope * SM_SCALE, (x * cos + rot * sin) * SM_SCALE


def _rope_k(kr_ref, cos, sin):
    blk = kr_ref[...].astype(F32)
    return blk * cos + pltpu.roll(blk, QK_ROPE, 1) * sin


def _lora_inputs(refs):
    cq = jnp.concatenate([r[...] for r in refs[:N_CQ_BLK]], axis=1).astype(F32)
    ckv = jnp.concatenate([r[...] for r in refs[N_CQ_BLK:N_CQ_BLK + N_CKV_BLK]], axis=1
                          ).astype(F32)
    return cq, ckv, refs[N_CQ_BLK + N_CKV_BLK:]


def _lora_specs(tm, rb0):
    def blk(c0, k):
        return pl.BlockSpec((tm, LORA_BLK), lambda i: (rb0 + i, c0 // LORA_BLK + k))

    return ([blk(COL_CQ, k) for k in range(N_CQ_BLK)]
            + [blk(COL_CKV, k) for k in range(N_CKV_BLK)]
            + [pl.BlockSpec((tm, LANE), lambda i: (rb0 + i, COL_KR // LANE))])


def _mla_pre_prompt_kernel(*refs):
    cq, ckv_in, refs = _lora_inputs(refs)
    (kr_ref, qg_ref, kvg_ref, wqm_ref, wuk_ref, wuv_ref, cos_ref, sin_ref,
     q_ref, k_ref, v_ref, ckv_out_ref, kr_out_ref) = refs
    cos, sin = cos_ref[...], sin_ref[...]
    cqn = _rms(cq, qg_ref[...]).astype(BF16)
    qm = _dot(cqn, wqm_ref[...])
    ckv = _rms(ckv_in, kvg_ref[...])
    ckv_out_ref[...] = ckv
    ckb = ckv.astype(BF16)
    kn = _dot(ckb, wuk_ref[...])
    v_ref[...] = _dot(ckb, wuv_ref[...]).astype(v_ref.dtype)
    krp = _rope_k(kr_ref, cos, sin)
    kr_out_ref[...] = krp[:, :QK_ROPE]
    krb = krp.astype(BF16)
    for h in range(MLA_HEADS):
        nope, rope = _rope_q(qm, cos, sin, h)
        q_ref[:, h * QK_PAD:h * QK_PAD + QK_NOPE] = nope.astype(BF16)
        q_ref[:, h * QK_PAD + QK_NOPE:(h + 1) * QK_PAD] = rope.astype(BF16)
        k_ref[:, h * QK_PAD:h * QK_PAD + QK_NOPE] = kn[:, h * LANE:(h + 1) * LANE].astype(BF16)
        k_ref[:, h * QK_PAD + QK_NOPE:(h + 1) * QK_PAD] = krb


def _mla_pre_prompt(proj, lw, cos, sin, mp, t):
    tm = _tile(t, 512)
    nt = t // tm
    full = lambda shape: pl.BlockSpec(shape, lambda i: (0,) * len(shape))
    hq, hv = MLA_HEADS * QK_PAD, MLA_HEADS * V_HEAD
    return pl.pallas_call(
        _mla_pre_prompt_kernel,
        grid=(mp // tm,),
        in_specs=_lora_specs(tm, 0) + [
                  full((1, Q_LORA)), full((1, KV_LORA)),
                  full((Q_LORA, hq)),
                  full((KV_LORA, hv)), full((KV_LORA, hv)),
                  pl.BlockSpec((tm, LANE), lambda i: (i % nt, 0)),
                  pl.BlockSpec((tm, LANE), lambda i: (i % nt, 0))],
        out_specs=[pl.BlockSpec((tm, hq), lambda i: (i, 0)),
                   pl.BlockSpec((tm, hq), lambda i: (i, 0)),
                   pl.BlockSpec((tm, hv), lambda i: (i, 0)),
                   pl.BlockSpec((tm, KV_LORA), lambda i: (i, 0)),
                   pl.BlockSpec((tm, QK_ROPE), lambda i: (i, 0))],
        out_shape=[jax.ShapeDtypeStruct((mp, hq), BF16),
                   jax.ShapeDtypeStruct((mp, hq), BF16),
                   jax.ShapeDtypeStruct((mp, hv), BF16),
                   jax.ShapeDtypeStruct((mp, KV_LORA), F32),
                   jax.ShapeDtypeStruct((mp, QK_ROPE), F32)],
        compiler_params=_params(1),
        name="mla_pre_prompt",
    )(*[proj] * (N_CQ_BLK + N_CKV_BLK + 1), lw["q_norm_g"], lw["kv_norm_g"], lw["wq_main"],
      lw["w_uk"], lw["w_uv"], cos, sin)


def _mla_pre_sample_kernel(*refs):
    cq, ckv_in, refs = _lora_inputs(refs)
    (kr_ref, qg_ref, kvg_ref, wqm_ref, wukt_ref, cos_ref, sin_ref,
     ql_ref, qr_ref, ckv_out_ref, kr_out_ref) = refs
    cos, sin = cos_ref[...], sin_ref[...]
    cqn = _rms(cq, qg_ref[...]).astype(BF16)
    qm = _dot(cqn, wqm_ref[...])
    ckv_out_ref[...] = _rms(ckv_in, kvg_ref[...])
    kr_out_ref[...] = _rope_k(kr_ref, cos, sin)[:, :QK_ROPE]
    for h in range(MLA_HEADS):
        nope, rope = _rope_q(qm, cos, sin, h)
        ql_ref[h] = _dot(nope.astype(BF16), wukt_ref[h]).astype(BF16)
        qr_ref[h] = rope[:, :QK_ROPE].astype(BF16)


def _mla_pre_sample(proj, lw, cos, sin, mp, ms, tm):
    rb0 = mp // tm
    full = lambda shape: pl.BlockSpec(shape, lambda i: (0,) * len(shape))
    return pl.pallas_call(
        _mla_pre_sample_kernel,
        grid=(ms // tm,),
        in_specs=_lora_specs(tm, rb0) + [
                  full((1, Q_LORA)), full((1, KV_LORA)),
                  full((Q_LORA, MLA_HEADS * QK_PAD)),
                  full((MLA_HEADS, QK_NOPE, KV_LORA)),
                  full((tm, LANE)), full((tm, LANE))],
        out_specs=[pl.BlockSpec((MLA_HEADS, tm, KV_LORA), lambda i: (0, i, 0)),
                   pl.BlockSpec((MLA_HEADS, tm, QK_ROPE), lambda i: (0, i, 0)),
                   pl.BlockSpec((tm, KV_LORA), lambda i: (i, 0)),
                   pl.BlockSpec((tm, QK_ROPE), lambda i: (i, 0))],
        out_shape=[jax.ShapeDtypeStruct((MLA_HEADS, ms, KV_LORA), BF16),
                   jax.ShapeDtypeStruct((MLA_HEADS, ms, QK_ROPE), BF16),
                   jax.ShapeDtypeStruct((ms, KV_LORA), F32),
                   jax.ShapeDtypeStruct((ms, QK_ROPE), F32)],
        compiler_params=_params(1),
        name="mla_pre_sample",
    )(*[proj] * (N_CQ_BLK + N_CKV_BLK + 1), lw["q_norm_g"], lw["kv_norm_g"], lw["wq_main"],
      lw["w_ukt"], cos, sin)


ATTN_Q_BLOCK = 512
ATTN_HEADS_PER_STEP = 4


def _attn_prompt_kernel(q_ref, k_ref, v_ref, g_ref, o_ref, *, tq):
    t = q_ref.shape[0]
    rc = lax.broadcasted_iota(I32, (tq, tq), 0) // CHUNK
    cc = lax.broadcasted_iota(I32, (tq, tq), 1) // CHUNK
    visible = cc <= rc
    for h in range(ATTN_HEADS_PER_STEP):
        qk = slice(h * QK_PAD, (h + 1) * QK_PAD)
        hv = slice(h * V_HEAD, (h + 1) * V_HEAD)
        g = g_ref[:, hv]
        for i in range(t // tq):
            lo, hi = i * tq, (i + 1) * tq
            q = q_ref[lo:hi, qk]
            sd = jnp.where(visible, _dot_nt(q, k_ref[lo:hi, qk]), -jnp.inf)
            m = jnp.max(sd, axis=-1, keepdims=True)
            if i:
                so = _dot_nt(q, k_ref[0:lo, qk])
                m = jnp.maximum(m, jnp.max(so, axis=-1, keepdims=True))
            pd = jnp.exp(sd - m)
            l = jnp.sum(pd, axis=-1, keepdims=True)
            o = _dot(pd.astype(BF16), v_ref[lo:hi, hv])
            if i:
                po = jnp.exp(so - m)
                l = l + jnp.sum(po, axis=-1, keepdims=True)
                o = o + _dot(po.astype(BF16), v_ref[0:lo, hv])
            o_ref[lo:hi, hv] = _rms(o / l, g).astype(o_ref.dtype)


def _attn_prompt(q, k, v, g_mix, nb, t):
    tq = _tile(t, ATTN_Q_BLOCK)
    nh = ATTN_HEADS_PER_STEP
    c0 = (CONV_DIM + SGU_DIM) // (nh * HEAD)
    return pl.pallas_call(
        functools.partial(_attn_prompt_kernel, tq=tq),
        grid=(nb, MLA_HEADS // nh),
        in_specs=[pl.BlockSpec((t, nh * QK_PAD), lambda b, h: (b, h)),
                  pl.BlockSpec((t, nh * QK_PAD), lambda b, h: (b, h)),
                  pl.BlockSpec((t, nh * V_HEAD), lambda b, h: (b, h)),
                  pl.BlockSpec((1, nh * HEAD), lambda b, h: (0, c0 + h))],
        out_specs=pl.BlockSpec((t, nh * HEAD), lambda b, h: (b, h)),
        out_shape=jax.ShapeDtypeStruct((nb * t, MLA_HEADS * HEAD), BF16),
        compiler_params=_params(2),
        name="attn_prompt",
    )(q, k, v, g_mix)


def _attn_sample_kernel(ql_ref, qr_ref, cc_ref, ck_ref, nc_ref, nk_ref, wuv_ref, g_ref, o_ref):
    s_len = ql_ref.shape[1]
    ql = ql_ref[...].reshape(MLA_HEADS * s_len, KV_LORA)
    qr = qr_ref[...].reshape(MLA_HEADS * s_len, QK_ROPE)
    cc = cc_ref[0, 0].astype(BF16)
    ck = ck_ref[0, 0].astype(BF16)
    nc = nc_ref[...].astype(BF16)
    nk = nk_ref[...].astype(BF16)
    sp = _dot_nt(ql, cc) + _dot_nt(qr, ck)
    sn = _dot_nt(ql, nc) + _dot_nt(qr, nk)
    m = jnp.maximum(jnp.max(sp, axis=-1, keepdims=True), jnp.max(sn, axis=-1, keepdims=True))
    pp, pn = jnp.exp(sp - m), jnp.exp(sn - m)
    l = jnp.sum(pp, axis=-1, keepdims=True) + jnp.sum(pn, axis=-1, keepdims=True)
    o_lat = ((_dot(pp.astype(BF16), cc) + _dot(pn.astype(BF16), nc)) / l).astype(BF16)
    for h in range(MLA_HEADS):
        oh = _dot(o_lat[h * s_len:(h + 1) * s_len], wuv_ref[h])
        o_ref[:, h * HEAD:(h + 1) * HEAD] = _rms(oh, g_ref[:, h * HEAD:(h + 1) * HEAD]
                                                 ).astype(o_ref.dtype)


def _attn_sample(q_lat, q_rope, cache_ckv, cache_krope, ckv_new, kr_new, wuv_h, g_mix,
                 layer, nb, s_len):
    past = cache_ckv.shape[2]
    full = lambda shape: pl.BlockSpec(shape, lambda b: (0,) * len(shape))
    return pl.pallas_call(
        _attn_sample_kernel,
        grid=(nb,),
        in_specs=[pl.BlockSpec((MLA_HEADS, s_len, KV_LORA), lambda b: (0, b, 0)),
                  pl.BlockSpec((MLA_HEADS, s_len, QK_ROPE), lambda b: (0, b, 0)),
                  pl.BlockSpec((1, 1, past, KV_LORA), lambda b: (layer, b, 0, 0)),
                  pl.BlockSpec((1, 1, past, QK_ROPE), lambda b: (layer, b, 0, 0)),
                  pl.BlockSpec((s_len, KV_LORA), lambda b: (b, 0)),
                  pl.BlockSpec((s_len, QK_ROPE), lambda b: (b, 0)),
                  full((MLA_HEADS, KV_LORA, V_HEAD)),
                  pl.BlockSpec((1, MLA_HEADS * HEAD), lambda b: (0, 1))],
        out_specs=pl.BlockSpec((s_len, MLA_HEADS * HEAD), lambda b: (b, 0)),
        out_shape=jax.ShapeDtypeStruct((nb * s_len, MLA_HEADS * HEAD), BF16),
        compiler_params=_params(1),
        name="attn_sample",
    )(q_lat, q_rope, cache_ckv, cache_krope, ckv_new, kr_new, wuv_h, g_mix)


def _out_proj_kernel(ap_ref, bp_ref, cp_ref, as_ref, bs_ref, cs_ref, w_ref, xp_ref, xs_ref,
                     g_ref, b_ref, op_ref, os_ref, *rest, n_p, packed):
    mx_ref = rest[-1]
    i = pl.program_id(0)
    c1, c2 = CONV_DIM, CONV_DIM + SGU_DIM

    @pl.when(i < n_p)
    def _():
        mx_ref[:, :c1] = ap_ref[...]
        mx_ref[:, c1:c2] = bp_ref[...]
        mx_ref[:, c2:] = cp_ref[...]

    @pl.when(i >= n_p)
    def _():
        mx_ref[:, :c1] = as_ref[...]
        mx_ref[:, c1:c2] = bs_ref[...]
        mx_ref[:, c2:] = cs_ref[...]

    y = ALPHA * _load2(xp_ref, xs_ref, i, n_p) + _dot(mx_ref[...], w_ref[...])
    out = _layernorm(y, g_ref[...], b_ref[...])
    _store2(op_ref, os_ref, i, n_p, out)
    if packed:
        rest[0][...] = _pack_halves(out)
    else:
        _store2(rest[0], rest[1], i, n_p, out)


def _out_proj(mix_p, mix_s, w_o, x_p, x_s, g, b, packed):
    (mp, d), ms = x_p.shape, x_s.shape[0]
    tm = _tile(ms, 256)
    n_p = mp // tm
    m = mp + ms
    full = lambda shape: pl.BlockSpec(shape, lambda i: (0,) * len(shape))
    in_specs = []
    for arrs, clamp in ((mix_p, lambda i: (jnp.minimum(i, n_p - 1), 0)),
                        (mix_s, lambda i: (jnp.maximum(i - n_p, 0), 0))):
        in_specs += [pl.BlockSpec((tm, a.shape[1]), clamp) for a in arrs]
    xp_spec, xs_spec = _two_stream_specs(tm, d, n_p)
    if packed:
        low_specs = [pl.BlockSpec((tm, d // 2), lambda i: (i, 0))]
        low_shapes = [jax.ShapeDtypeStruct((m, d // 2), I32)]
    else:
        low_specs = [xp_spec, xs_spec]
        low_shapes = [jax.ShapeDtypeStruct((mp, d), BF16), jax.ShapeDtypeStruct((ms, d), BF16)]
    return pl.pallas_call(
        functools.partial(_out_proj_kernel, n_p=n_p, packed=packed),
        grid=(m // tm,),
        in_specs=in_specs + [full((d, d)), xp_spec, xs_spec, full((1, d)), full((1, d))],
        out_specs=[xp_spec, xs_spec] + low_specs,
        out_shape=[jax.ShapeDtypeStruct((mp, d), F32), jax.ShapeDtypeStruct((ms, d), F32)]
                  + low_shapes,
        scratch_shapes=[pltpu.VMEM((tm, d), BF16)],
        compiler_params=_params(1),
        name="out_proj_ln",
    )(*mix_p, *mix_s, w_o, x_p, x_s, g, b)


def _bf16_weights(w_ref):
    w = w_ref[...]
    return pltpu.bitcast(w, BF16) if w.dtype == I32 else w


def _swiglu_accumulate(xb_ref, wg_ref, wu_ref, wd_ref, acc_ref):
    xb = xb_ref[...]
    a = _dot(xb, _bf16_weights(wg_ref))
    h = a * (1.0 / (1.0 + jnp.exp(-a))) * _dot(xb, _bf16_weights(wu_ref))
    acc_ref[...] += _dot(h.astype(BF16), _bf16_weights(wd_ref))


def _ffn_kernel(xb_ref, wg_ref, wu_ref, wd_ref, x_ref, g_ref, b_ref, *rest):
    o_ref, ob_ref, acc_ref = rest[-3:]
    f = pl.program_id(1)

    @pl.when(f == 0)
    def _():
        acc_ref[...] = jnp.zeros_like(acc_ref)

    _swiglu_accumulate(xb_ref, wg_ref, wu_ref, wd_ref, acc_ref)

    @pl.when(f == pl.num_programs(1) - 1)
    def _():
        out = _layernorm(ALPHA * x_ref[...] + acc_ref[...], g_ref[...], b_ref[...])
        o_ref[...] = out
        ob_ref[...] = out.astype(BF16)


def _ffn(xb, x, wg, wu, wd, g, b, after=()):
    m, d = x.shape
    ff = wg.shape[1]
    tm, tf = _tile(m, 512), _tile(ff, 512)
    row = lambda i, f: (i, 0)
    vec = pl.BlockSpec((1, d), lambda i, f: (0, 0))
    return pl.pallas_call(
        _ffn_kernel,
        grid=(m // tm, ff // tf),
        in_specs=[pl.BlockSpec((tm, d), row),
                  pl.BlockSpec((d, tf), lambda i, f: (0, f)),
                  pl.BlockSpec((d, tf), lambda i, f: (0, f)),
                  pl.BlockSpec((tf, d), lambda i, f: (f, 0)),
                  pl.BlockSpec((tm, d), row), vec, vec]
                 + [pl.BlockSpec(memory_space=pl.ANY)] * len(after),
        out_specs=[pl.BlockSpec((tm, d), row), pl.BlockSpec((tm, d), row)],
        out_shape=[jax.ShapeDtypeStruct((m, d), F32), jax.ShapeDtypeStruct((m, d), BF16)],
        scratch_shapes=[pltpu.VMEM((tm, d), F32)],
        compiler_params=_params(2),
        name="ffn_dense",
    )(xb, wg, wu, wd, x, g, b, *after)


EXPERT_ROWS = 16
ROW_TILE = 512


def _router_kernel(xp_ref, xs_ref, wh_ref, wl_ref, gate_ref, rank_ref, cnt_ref, carry_ref, *, n_p):
    i = pl.program_id(0)

    @pl.when(i == 0)
    def _():
        carry_ref[...] = jnp.zeros_like(carry_ref)

    x = _load2(xp_ref, xs_ref, i, n_p)
    ts = x.shape[0]
    xh = x.astype(BF16)
    xl = (x - xh.astype(F32)).astype(BF16)
    wh, wl = wh_ref[...], wl_ref[...]
    logits = _dot_nt(wh, xh) + (_dot_nt(wh, xl) + _dot_nt(wl, xh))
    row = lax.broadcasted_iota(I32, logits.shape, 0)
    logits = jnp.where(row < N_EXPERTS, logits, -jnp.inf)
    m1 = jnp.max(logits, axis=0, keepdims=True)
    i1 = jnp.min(jnp.where(logits == m1, row, EXPERT_ROWS), axis=0, keepdims=True)
    rest = jnp.where(row == i1, -jnp.inf, logits)
    m2 = jnp.max(rest, axis=0, keepdims=True)
    i2 = jnp.min(jnp.where(rest == m2, row, EXPERT_ROWS), axis=0, keepdims=True)
    e2 = jnp.exp(m2 - m1)
    g1 = 1.0 / (1.0 + e2)
    g2 = e2 / (1.0 + e2)
    gate_ref[...] = jnp.where(row == i1, g1, jnp.where(row == i2, g2, 0.0))
    sel = jnp.where(row == i1, 1.0, jnp.where(row == i2, 1.0, 0.0))
    src = lax.broadcasted_iota(I32, (ts, ts), 0)
    dst = lax.broadcasted_iota(I32, (ts, ts), 1)
    incl = _dot(sel.astype(BF16), jnp.where(src <= dst, 1.0, 0.0).astype(BF16))
    carry = carry_ref[:, 0:1]
    rank_ref[...] = jnp.where(sel > 0.0, carry + incl - sel, -1.0).astype(I32)
    carry = carry + jnp.sum(sel, axis=1, keepdims=True)
    carry_ref[...] = jnp.broadcast_to(carry, carry_ref.shape)
    cnt_ref[...] = jnp.broadcast_to(carry, carry_ref.shape).astype(I32)


def _router(x_p, x_s, wh, wl):
    (mp, d), ms = x_p.shape, x_s.shape[0]
    m = mp + ms
    ts = _tile(ms, 512)
    n_p = mp // ts
    full = pl.BlockSpec((EXPERT_ROWS, d), lambda i: (0, 0))
    col = pl.BlockSpec((EXPERT_ROWS, ts), lambda i: (0, i))
    xp_spec, xs_spec = _two_stream_specs(ts, d, n_p)
    return pl.pallas_call(
        functools.partial(_router_kernel, n_p=n_p),
        grid=(m // ts,),
        in_specs=[xp_spec, xs_spec, full, full],
        out_specs=[col, col, pl.BlockSpec((EXPERT_ROWS, LANE), lambda i: (0, 0))],
        out_shape=[jax.ShapeDtypeStruct((EXPERT_ROWS, m), F32),
                   jax.ShapeDtypeStruct((EXPERT_ROWS, m), I32),
                   jax.ShapeDtypeStruct((EXPERT_ROWS, LANE), I32)],
        scratch_shapes=[pltpu.VMEM((EXPERT_ROWS, LANE), F32)],
        compiler_params=_params(1),
        name="router",
    )(x_p, x_s, wh, wl)


def _route_plan(gate_t, rank_t, cnt, m, td):
    ne = N_EXPERTS
    total = cnt[:ne, 0]
    padded = (total + td - 1) // td * td
    ends = jnp.cumsum(padded)
    off = ends - padded
    n_act = (ends[-1] // td).astype(I32)
    rt = -(-(2 * m + ne * (td - 1)) // td)
    tiles = jnp.minimum(jnp.arange(rt, dtype=I32), n_act - 1)
    tile_expert = jnp.sum(tiles[:, None] >= (ends // td)[None, :], axis=1).astype(I32)

    rank, gate = rank_t[:ne], gate_t[:ne]
    chosen = rank >= 0
    pos = off[:, None] + rank
    pos_a = jnp.min(jnp.where(chosen, pos, rt * td), axis=0).astype(I32)
    pos_b = jnp.max(jnp.where(chosen, pos, -1), axis=0).astype(I32)
    gate_a = jnp.sum(jnp.where(chosen & (pos == pos_a[None]), gate, 0.0), axis=0)
    gate_b = jnp.sum(jnp.where(chosen & (pos == pos_b[None]), gate, 0.0), axis=0)
    return pos_a, pos_b, gate_a[:, None], gate_b[:, None], tile_expert, n_act.reshape(1), rt


def _sc_mesh():
    return plsc.VectorSubcoreMesh(core_axis_name="c", subcore_axis_name="s")


def _sc_worker_base(per_worker):
    wid = lax.axis_index("s") * SC_CORES + lax.axis_index("c")
    return wid * per_worker


def _sc_scatter_rows(x, idx_a, idx_b, n_rows):
    m, w = x.shape
    per_worker = m // (SC_CORES * SC_SUBCORES)
    assert per_worker * SC_CORES * SC_SUBCORES == m and per_worker % SC_ROWS == 0, m

    @functools.partial(
        pl.kernel, mesh=_sc_mesh(),
        out_type=jax.ShapeDtypeStruct((n_rows, w), x.dtype),
        scratch_types=[pltpu.VMEM((SC_ROWS,), I32), pltpu.VMEM((SC_ROWS,), I32),
                       pltpu.VMEM((SC_ROWS, w), x.dtype),
                       pltpu.SemaphoreType.DMA, pltpu.SemaphoreType.DMA])
    def scatter(x_hbm, ia_hbm, ib_hbm, out_hbm, ia_v, ib_v, rows_v, sem_a, sem_b):
        base = _sc_worker_base(per_worker)

        @pl.loop(0, per_worker // SC_ROWS)
        def _(c):
            lo = pl.multiple_of(base + c * SC_ROWS, 8)
            pltpu.sync_copy(ia_hbm.at[pl.ds(lo, SC_ROWS)], ia_v)
            pltpu.sync_copy(ib_hbm.at[pl.ds(lo, SC_ROWS)], ib_v)
            pltpu.sync_copy(x_hbm.at[pl.ds(lo, SC_ROWS)], rows_v)
            put_a = pltpu.async_copy(rows_v, out_hbm.at[ia_v], sem_a)
            put_b = pltpu.async_copy(rows_v, out_hbm.at[ib_v], sem_b)
            put_a.wait()
            put_b.wait()

    return scatter(x, idx_a, idx_b)


def _sc_pack_bf16_rows(w, block_cols):
    r, c = w.shape
    lanes, unroll, block_rows = SC_LANES, 8, 8
    assert r % (2 * block_rows) == 0 and c % block_cols == 0 and block_cols % (lanes * unroll) == 0

    @functools.partial(
        pl.kernel, mesh=_sc_mesh(),
        out_type=jax.ShapeDtypeStruct((r // 2, c), I32), scratch_types=[],
        compiler_params=pltpu.CompilerParams(needs_layout_passes=False),
        cost_estimate=pl.CostEstimate(flops=r * c, transcendentals=0,
                                      bytes_accessed=r * c * 4 + r * c * 2))
    def pack_rows(w_hbm, o_hbm):
        def body(in_v, out_v):
            @pl.loop(0, block_rows)
            def _(i):
                @pl.loop(0, block_cols, step=lanes * unroll)
                def _(c0):
                    for u in range(unroll):
                        cs = pl.ds(c0 + u * lanes, lanes)
                        pair = plsc.pack(in_v[2 * i, cs], in_v[2 * i + 1, cs],
                                         format=plsc.PackFormat.INTERLEAVED)
                        out_v[i, cs] = plsc.bitcast(pair, I32)

        pltpu.emit_pipeline(
            body, grid=(r // (2 * block_rows), c // block_cols),
            in_specs=[pl.BlockSpec((2 * block_rows, block_cols), lambda i, j: (i, j))],
            out_specs=[pl.BlockSpec((block_rows, block_cols), lambda i, j: (i, j))],
            core_axis_name=("c", "s"),
            dimension_semantics=(pltpu.PARALLEL, pltpu.PARALLEL))(w_hbm, o_hbm)

    return pack_rows(w)


def _pack_expert_weights(w, block_cols):
    e, r, c = w.shape
    return _sc_pack_bf16_rows(w.reshape(e * r, c), block_cols).reshape(e, r // 2, c)


def _sc_gather_row_pairs(table, idx_a, idx_b):
    m, w = idx_a.shape[0], table.shape[1]
    per_worker = m // (SC_CORES * SC_SUBCORES)
    assert per_worker * SC_CORES * SC_SUBCORES == m and per_worker % SC_ROWS == 0, m
    out = jax.ShapeDtypeStruct((m, w), table.dtype)

    @functools.partial(
        pl.kernel, mesh=_sc_mesh(), out_type=(out, out),
        scratch_types=[pltpu.VMEM((SC_ROWS,), I32), pltpu.VMEM((SC_ROWS,), I32),
                       pltpu.VMEM((SC_ROWS, w), table.dtype), pltpu.VMEM((SC_ROWS, w), table.dtype)]
                      + [pltpu.SemaphoreType.DMA] * 4)
    def gather(table_hbm, ia_hbm, ib_hbm, oa_hbm, ob_hbm, ia_v, ib_v, ra_v, rb_v,
               get_a_sem, get_b_sem, put_a_sem, put_b_sem):
        base = _sc_worker_base(per_worker)

        @pl.loop(0, per_worker // SC_ROWS)
        def _(c):
            lo = pl.multiple_of(base + c * SC_ROWS, 8)
            pltpu.sync_copy(ia_hbm.at[pl.ds(lo, SC_ROWS)], ia_v)
            pltpu.sync_copy(ib_hbm.at[pl.ds(lo, SC_ROWS)], ib_v)
            get_a = pltpu.async_copy(table_hbm.at[ia_v], ra_v, get_a_sem)
            get_b = pltpu.async_copy(table_hbm.at[ib_v], rb_v, get_b_sem)
            get_a.wait()
            put_a = pltpu.async_copy(ra_v, oa_hbm.at[pl.ds(lo, SC_ROWS)], put_a_sem)
            get_b.wait()
            put_b = pltpu.async_copy(rb_v, ob_hbm.at[pl.ds(lo, SC_ROWS)], put_b_sem)
            put_a.wait()
            put_b.wait()

    return gather(table, idx_a, idx_b)


def _experts_kernel(te_ref, na_ref, xs_ref, wg_ref, wu_ref, wd_ref, o_ref, xb_ref, acc_ref):
    del te_ref
    j, f = pl.program_id(0), pl.program_id(1)
    half = xs_ref.shape[1]

    @pl.when(j < na_ref[0])
    def _():
        @pl.when(f == 0)
        def _():
            lo, hi = _unpack_halves(xs_ref[...])
            xb_ref[:, :half] = lo.astype(BF16)
            xb_ref[:, half:] = hi.astype(BF16)
            acc_ref[...] = jnp.zeros_like(acc_ref)

        _swiglu_accumulate(xb_ref, wg_ref.at[0], wu_ref.at[0], wd_ref.at[0], acc_ref)

        @pl.when(f == pl.num_programs(1) - 1)
        def _():
            o_ref[...] = _pack_halves(acc_ref[...])


def _experts(tile_expert, n_act, xs, wg, wu, wd, td):
    rows, half = xs.shape
    d = 2 * half
    ff = wg.shape[2]
    tf = _tile(ff, 512)
    nf = ff // tf
    row = lambda j, f, te, na: (jnp.minimum(j, na[0] - 1), 0)
    fcol = lambda j, f, na: jnp.where(j < na[0], f, nf - 1)
    return pl.pallas_call(
        _experts_kernel,
        grid_spec=pltpu.PrefetchScalarGridSpec(
            num_scalar_prefetch=2,
            grid=(rows // td, nf),
            in_specs=[pl.BlockSpec((td, half), row),
                      pl.BlockSpec((1, half, tf), lambda j, f, te, na: (te[j], 0, fcol(j, f, na))),
                      pl.BlockSpec((1, half, tf), lambda j, f, te, na: (te[j], 0, fcol(j, f, na))),
                      pl.BlockSpec((1, tf // 2, d),
                                   lambda j, f, te, na: (te[j], fcol(j, f, na), 0))],
            out_specs=pl.BlockSpec((td, half), row),
            scratch_shapes=[pltpu.VMEM((td, d), BF16), pltpu.VMEM((td, d), F32)]),
        out_shape=jax.ShapeDtypeStruct((rows, half), I32),
        compiler_params=_params(2),
        name="moe_experts",
    )(tile_expert, n_act, xs, wg, wu, wd)


def _moe_finish_kernel(oa_ref, ob_ref, ga_ref, gb_ref, xp_ref, xs_ref, g_ref, b_ref,
                       op_ref, os_ref, *, n_p):
    i = pl.program_id(0)
    a_lo, a_hi = _unpack_halves(oa_ref[...])
    b_lo, b_hi = _unpack_halves(ob_ref[...])
    ga, gb = ga_ref[...], gb_ref[...]
    y = jnp.concatenate([ga * a_lo + gb * b_lo, ga * a_hi + gb * b_hi], axis=1)
    out = _layernorm(ALPHA * _load2(xp_ref, xs_ref, i, n_p) + y, g_ref[...], b_ref[...])
    _store2(op_ref, os_ref, i, n_p, out)


def _moe_finish(o_a, o_b, gate_a, gate_b, x_p, x_s, g, b):
    (mp, d), ms = x_p.shape, x_s.shape[0]
    m = mp + ms
    tm = _tile(ms, 512)
    n_p = mp // tm
    row = lambda i: (i, 0)
    vec = pl.BlockSpec((1, d), lambda i: (0, 0))
    xp_spec, xs_spec = _two_stream_specs(tm, d, n_p)
    return pl.pallas_call(
        functools.partial(_moe_finish_kernel, n_p=n_p),
        grid=(m // tm,),
        in_specs=[pl.BlockSpec((tm, d // 2), row), pl.BlockSpec((tm, d // 2), row),
                  pl.BlockSpec((tm, 1), row), pl.BlockSpec((tm, 1), row),
                  xp_spec, xs_spec, vec, vec],
        out_specs=[xp_spec, xs_spec],
        out_shape=[jax.ShapeDtypeStruct((mp, d), F32), jax.ShapeDtypeStruct((ms, d), F32)],
        compiler_params=_params(1),
        name="moe_finish",
    )(o_a, o_b, gate_a, gate_b, x_p, x_s, g, b)


def _pack_moe_weights(w_gate, w_up, w_down):
    return (_pack_expert_weights(w_gate, w_gate.shape[2] // 4),
            _pack_expert_weights(w_up, w_up.shape[2] // 4),
            _pack_expert_weights(w_down, w_down.shape[2] // 2))


def _moe(x_p, x_s, x_packed, router_w, wg, wu, wd, g, b):
    m = x_p.shape[0] + x_s.shape[0]
    rw = jnp.pad(jnp.transpose(router_w), ((0, EXPERT_ROWS - N_EXPERTS), (0, 0)))
    rwh = rw.astype(BF16)
    rwl = (rw - rwh.astype(F32)).astype(BF16)
    gate_t, rank_t, cnt = _router(x_p, x_s, rwh, rwl)
    td = min(ROW_TILE, _tile(m, ROW_TILE))
    pos_a, pos_b, gate_a, gate_b, tile_expert, n_act, rt = _route_plan(gate_t, rank_t, cnt, m, td)
    xs = _sc_scatter_rows(x_packed, pos_a, pos_b, rt * td)
    o_sorted = _experts(tile_expert, n_act, xs, wg, wu, wd, td)
    o_a, o_b = _sc_gather_row_pairs(o_sorted, pos_a, pos_b)
    return _moe_finish(o_a, o_b, gate_a, gate_b, x_p, x_s, g, b)


def _rot_cols(w):
    half = QK_ROPE // 2
    return jnp.concatenate([-w[..., half:], w[..., :half]], axis=-1)


def _layer_weights(l, w_in, conv_w, sgu_ln_g, sgu_ln_b, sgu_w, sgu_b, q_norm_g, w_uq, kv_norm_g,
                   w_uk, w_uv, mix_norm_g, w_o):
    d = w_in.shape[1]
    wi = w_in[l]
    k_r = wi[:, COL_KR:COL_KR + QK_ROPE]
    w_in_pad = jnp.concatenate(
        [wi.astype(BF16), _rot_cols(k_r).astype(BF16),
         jnp.zeros((d, D_IN_PAD - COL_KR - 2 * QK_ROPE), BF16)], axis=1)
    uq = w_uq[l]
    zeros = jnp.zeros((Q_LORA, MLA_HEADS, QK_PAD - QK_NOPE - QK_ROPE), F32)
    wq_main = jnp.concatenate([uq, zeros], axis=-1).reshape(Q_LORA, MLA_HEADS * QK_PAD)
    return {
        "w_in": w_in_pad,
        "conv_w": conv_w[l],
        "sgu_w": sgu_w[l],
        "sgu_bt": jnp.transpose(sgu_b[l]),
        "sgu_ln_g": sgu_ln_g[l][None], "sgu_ln_b": sgu_ln_b[l][None],
        "q_norm_g": q_norm_g[l][None], "kv_norm_g": kv_norm_g[l][None],
        "wq_main": wq_main.astype(BF16),
        "w_uk": w_uk[l].reshape(KV_LORA, MLA_HEADS * QK_NOPE).astype(BF16),
        "w_uv": w_uv[l].reshape(KV_LORA, MLA_HEADS * V_HEAD).astype(BF16),
        "w_ukt": jnp.transpose(w_uk[l], (1, 2, 0)).astype(BF16),
        "w_uv_h": jnp.transpose(w_uv[l], (1, 0, 2)).astype(BF16),
        "g_mix": mix_norm_g[l][None],
        "w_o": w_o[l].astype(BF16),
    }


def _rope_tables(pos):
    half = QK_ROPE // 2
    inv = ROPE_THETA ** (-jnp.arange(half, dtype=F32) / half)
    ang = pos.astype(F32)[:, None] * inv[None, :]
    zeros = jnp.zeros((pos.shape[0], LANE - QK_ROPE), F32)
    cos = jnp.concatenate([jnp.cos(ang), jnp.cos(ang), zeros], axis=-1)
    sin = jnp.concatenate([jnp.sin(ang), jnp.sin(ang), zeros], axis=-1)
    return cos, sin


def kernel(x_prompt, x_sample, state_conv, cache_ckv, cache_krope, w_in, conv_w, sgu_ln_g,
           sgu_ln_b, sgu_w, sgu_b, q_norm_g, w_uq, kv_norm_g, w_uk, w_uv, mix_norm_g, w_o,
           ln1_g, ln1_b, ln2_g, ln2_b, ffn_w_gate, ffn_w_up, ffn_w_down, router_w,
           moe_w_gate, moe_w_up, moe_w_down):
    nbp, t, d = x_prompt.shape
    nbs, s_len, _ = x_sample.shape
    past = cache_ckv.shape[2]
    depth = w_in.shape[0]
    mp, ms = nbp * t, nbs * s_len

    x_p, x_s = x_prompt.reshape(mp, d), x_sample.reshape(ms, d)
    xb_p, xb_s = x_p, x_s

    cos_p, sin_p = _rope_tables(jnp.arange(t))
    tms = _tile(ms, 256)
    cos_s, sin_s = _rope_tables(past + jnp.arange(s_len))
    cos_s, sin_s = jnp.tile(cos_s, (tms // s_len, 1)), jnp.tile(sin_s, (tms // s_len, 1))
    zero_state = jnp.zeros((nbp, CONV_WIDTH - 1, CONV_DIM), F32)

    outs = [[] for _ in range(7)]
    packed_experts = None
    for l in range(depth):
        lw = _layer_weights(l, w_in, conv_w, sgu_ln_g, sgu_ln_b, sgu_w, sgu_b, q_norm_g, w_uq,
                            kv_norm_g, w_uk, w_uv, mix_norm_g, w_o)
        routed = l % 2 == 1
        proj = _in_proj(xb_p, xb_s, lw["w_in"])

        a_p, conv_p = _conv_mixer(proj, zero_state, lw["conv_w"], lw["g_mix"], nbp, t, 0)
        a_s, conv_s = _conv_mixer(proj, state_conv[l], lw["conv_w"], lw["g_mix"], nbs, s_len, mp)
        sgu_args = (lw["sgu_w"], lw["sgu_bt"], lw["sgu_ln_g"], lw["sgu_ln_b"], lw["g_mix"])
        chunk_p = min(MLP_CHUNK, t)
        (b_p,) = _sgu_mixer(proj, *sgu_args, mp, 0, chunk_p,
                            _tile(t, 512) if t >= 512 else chunk_p, False)
        chunk_s = min(MLP_CHUNK, s_len)
        b_s, v_rows = _sgu_mixer(proj, *sgu_args, ms, mp, chunk_s, chunk_s, True)
        q, k, v, ckv_p, kr_p = _mla_pre_prompt(proj, lw, cos_p, sin_p, mp, t)
        c_p = _attn_prompt(q, k, v, lw["g_mix"], nbp, t)
        q_lat, q_rope, ckv_s, kr_s = _mla_pre_sample(proj, lw, cos_s, sin_s, mp, ms, tms)
        c_s = _attn_sample(q_lat, q_rope, cache_ckv, cache_krope, ckv_s, kr_s, lw["w_uv_h"],
                           lw["g_mix"], l, nbs, s_len)

        x_p, x_s, *low = _out_proj((a_p, b_p, c_p), (a_s, b_s, c_s), lw["w_o"], x_p, x_s,
                                   ln1_g[l][None], ln1_b[l][None], routed)

        i = l // 2
        if routed:
            if packed_experts is None:
                packed_experts = _pack_moe_weights(moe_w_gate[i], moe_w_up[i], moe_w_down[i])
            x_p, x_s = _moe(x_p, x_s, low[0], router_w[i], *packed_experts,
                            ln2_g[l][None], ln2_b[l][None])
            packed_experts = None
            xb_p, xb_s = x_p, x_s
        else:
            ffn_w = (ffn_w_gate[i].astype(BF16), ffn_w_up[i].astype(BF16),
                     ffn_w_down[i].astype(BF16), ln2_g[l][None], ln2_b[l][None])
            x_p, xb_p = _ffn(low[0], x_p, *ffn_w)
            if l + 1 < depth:
                nxt = (l + 1) // 2
                packed_experts = _pack_moe_weights(moe_w_gate[nxt], moe_w_up[nxt],
                                                   moe_w_down[nxt])
            x_s, xb_s = _ffn(low[1], x_s, *ffn_w, after=packed_experts or ())

        for lst, val in zip(outs, (conv_p, ckv_p.reshape(nbp, t, KV_LORA),
                                   kr_p.reshape(nbp, t, QK_ROPE), conv_s,
                                   ckv_s.reshape(nbs, s_len, KV_LORA),
                                   kr_s.reshape(nbs, s_len, QK_ROPE),
                                   v_rows.reshape(nbs, s_len, SGU_DIM))):
            lst.append(val)

    return (x_p.reshape(nbp, t, d), x_s.reshape(nbs, s_len, d), *[jnp.stack(o) for o in outs])
```
